```python
import jax, jax.numpy as jnp
from jax import lax
import numpy as np

D_MODEL = 1024
BATCH = 8
SEQ = 2048
DEPTH = 1
DEC_BATCH = 2
DEC_SEQ = 16384
PAST_LEN = 128

MIX_WIDTH = D_MODEL
POOL_WIDTH = MIX_WIDTH // 4
POOL_WINDOWS = (2, 4, 8, 16)
POOL_GROUPS = len(POOL_WINDOWS)
POOL_GROUP_DIM = POOL_WIDTH // POOL_GROUPS
ATTN_WIDTH = MIX_WIDTH - POOL_WIDTH
HEAD_DIM = 64
N_HEADS = ATTN_WIDTH // HEAD_DIM
DILATED_PATTERNS = ((128, 1), (512, 4), (2048, 16))
BAND_BLOCK = max(w // (2 * d) for w, d in DILATED_PATTERNS)
ROPE_THETA = 500000.0
ROPE_DIM = HEAD_DIM // 4
D_FF = 2816
CONV_WIDTH = 3
EPS = 1e-6
IN_WIDTH = POOL_WIDTH + 3 * ATTN_WIDTH

kernel_name = "hymba_pool_dilated_encoder"


def rmsnorm(x, g):
    xf = x.astype(jnp.float32)
    y = xf * lax.rsqrt(jnp.mean(xf * xf, axis=-1, keepdims=True) + EPS)
    return (y * g.astype(jnp.float32)).astype(x.dtype)


def partial_rope(t, positions):
    inv_freq = ROPE_THETA ** (-jnp.arange(0, ROPE_DIM, 2, dtype=jnp.float32) / ROPE_DIM)
    ang = positions[:, None] * inv_freq[None, :]
    ang = jnp.concatenate([ang, ang], axis=-1)[None, :, None, :]
    cos, sin = jnp.cos(ang), jnp.sin(ang)
    tr, tp = t[..., :ROPE_DIM], t[..., ROPE_DIM:]
    half = ROPE_DIM // 2
    rot = jnp.concatenate([-tr[..., half:], tr[..., :half]], axis=-1)
    return jnp.concatenate([tr * cos + rot * sin, tp], axis=-1)


def pool_mixer(p, pool_w, pool_scale):
    B, S, _ = p.shape
    pg = p.astype(jnp.float32).reshape(B, S, POOL_GROUPS, POOL_GROUP_DIM)
    cs = jnp.concatenate([jnp.zeros_like(pg[:, :1]), jnp.cumsum(pg, axis=1)], axis=1)
    idx = jnp.arange(S)
    outs = []
    for g, w in enumerate(POOL_WINDOWS):
        lo = jnp.clip(idx - w // 2, 0, S)
        hi = jnp.clip(idx + w // 2, 0, S)
        total = cs[:, hi, g] - cs[:, lo, g]
        mean = total / (hi - lo).astype(jnp.float32)[None, :, None]
        outs.append(mean - pg[:, :, g])
    pooled = jnp.stack(outs, axis=2)
    mixed = jnp.einsum('bsgc,gce->bsge', pooled, pool_w.astype(jnp.float32))
    return mixed.reshape(B, S, POOL_WIDTH) * pool_scale.astype(jnp.float32)


def dilated_band_attention(q, k, v, dilation, radius):
    B, S, H, E = q.shape
    L = S // dilation

    def split(t):
        return t.reshape(B, L, dilation, H, E).transpose(0, 2, 3, 1, 4)

    qs, ks, vs = split(q), split(k), split(v)
    nb = -(-L // BAND_BLOCK)
    Lp = nb * BAND_BLOCK
    pad = Lp - L
    qs = jnp.pad(qs, ((0, 0), (0, 0), (0, 0), (0, pad), (0, 0)))
    kpad = ((0, 0), (0, 0), (0, 0), (BAND_BLOCK, pad + BAND_BLOCK), (0, 0))
    ks, vs = jnp.pad(ks, kpad), jnp.pad(vs, kpad)
    qb = qs.reshape(B, dilation, H, nb, BAND_BLOCK, E)

    def neighbours(t):
        tb = t.reshape(B, dilation, H, nb + 2, BAND_BLOCK, E)
        return jnp.concatenate([tb[:, :, :, :-2], tb[:, :, :, 1:-1], tb[:, :, :, 2:]], axis=-2)

    kb, vb = neighbours(ks), neighbours(vs)
    scores = jnp.einsum('bdhnqe,bdhnke->bdhnqk', qb, kb) * (E ** -0.5)
    blk = jnp.arange(nb)[:, None, None]
    qpos = blk * BAND_BLOCK + jnp.arange(BAND_BLOCK)[None, :, None]
    kpos = (blk - 1) * BAND_BLOCK + jnp.arange(3 * BAND_BLOCK)[None, None, :]
    valid = (jnp.abs(qpos - kpos) <= radius) & (kpos >= 0) & (kpos < L)
    scores = jnp.where(valid, scores, -jnp.inf)
    lse = jax.nn.logsumexp(scores, axis=-1)
    probs = jnp.exp(scores - lse[..., None])
    out = jnp.einsum('bdhnqk,bdhnke->bdhnqe', probs, vb)
    out = out.reshape(B, dilation, H, Lp, E)[:, :, :, :L].transpose(0, 3, 1, 2, 4).reshape(B, S, H, E)
    lse = lse.reshape(B, dilation, H, Lp)[..., :L].transpose(0, 3, 1, 2).reshape(B, S, H)
    return out, lse


def dilated_mixture_attention(q, k, v):
    outs, lses = [], []
    for window, dilation in DILATED_PATTERNS:
        o, l = dilated_band_attention(q, k, v, dilation, window // (2 * dilation))
        outs.append(o)
        lses.append(l)
    wts = jax.nn.softmax(jnp.stack(lses, axis=-1), axis=-1)
    return jnp.einsum('bshp,pbshe->bshe', wts, jnp.stack(outs, axis=0))


def encoder_layer(x, g_mix_pre, g_mix_post, w_in, pool_w, pool_scale, g_pool_out, g_attn_out,
                  w_out, g_ffn_pre, g_ffn_post, w_up, conv_w, conv_b, w_down):
    B, S, _ = x.shape
    h = rmsnorm(x, g_mix_pre)
    proj = h @ w_in
    p_in = proj[..., :POOL_WIDTH]
    q = proj[..., POOL_WIDTH:POOL_WIDTH + ATTN_WIDTH]
    k = proj[..., POOL_WIDTH + ATTN_WIDTH:POOL_WIDTH + 2 * ATTN_WIDTH]
    v = proj[..., POOL_WIDTH + 2 * ATTN_WIDTH:]
    pool_out = pool_mixer(p_in, pool_w, pool_scale)
    pos = jnp.arange(S, dtype=jnp.float32)
    q = partial_rope(q.astype(jnp.float32).reshape(B, S, N_HEADS, HEAD_DIM), pos)
    k = partial_rope(k.astype(jnp.float32).reshape(B, S, N_HEADS, HEAD_DIM), pos)
    v = v.astype(jnp.float32).reshape(B, S, N_HEADS, HEAD_DIM)
    attn_out = dilated_mixture_attention(q, k, v).reshape(B, S, ATTN_WIDTH)
    mixed = jnp.concatenate([rmsnorm(pool_out, g_pool_out), rmsnorm(attn_out, g_attn_out)],
                            axis=-1).astype(x.dtype)
    x = x + rmsnorm(mixed @ w_out, g_mix_post)
    h = rmsnorm(x, g_ffn_pre)
    u = h @ w_up
    half = CONV_WIDTH // 2
    up = jnp.pad(u, ((0, 0), (half, half), (0, 0)))
    uc = conv_b
    for j in range(CONV_WIDTH):
        uc = uc + up[:, j:j + S] * conv_w[j]
    gate, val = uc[..., :D_FF], uc[..., D_FF:]
    act = jax.nn.gelu(gate, approximate=False) * val
    x = x + rmsnorm(act @ w_down, g_ffn_post)
    return x


def setup_inputs(seed: int = 0) -> dict:
    key = jax.random.key(seed)
    ks = jax.random.split(key, 16)
    f32 = jnp.float32

    def nrm(k, shape, scale):
        return jax.random.normal(k, shape, f32) * scale

    def gain(k, shape):
        return 1.0 + 0.02 * jax.random.normal(k, shape, f32)

    return {
        "x_prompt": jax.random.normal(ks[0], (BATCH, SEQ, D_MODEL), f32),
        "x_sample": jax.random.normal(ks[1], (DEC_BATCH, DEC_SEQ, D_MODEL), f32),
        "g_mix_pre": gain(ks[2], (DEPTH, D_MODEL)),
        "g_mix_post": gain(ks[3], (DEPTH, D_MODEL)),
        "w_in": nrm(ks[4], (DEPTH, D_MODEL, IN_WIDTH), D_MODEL ** -0.5),
        "pool_w": nrm(ks[5], (DEPTH, POOL_GROUPS, POOL_GROUP_DIM, POOL_GROUP_DIM), POOL_GROUP_DIM ** -0.5),
        "pool_scale": gain(ks[6], (DEPTH, POOL_WIDTH)),
        "g_pool_out": gain(ks[7], (DEPTH, POOL_WIDTH)),
        "g_attn_out": gain(ks[8], (DEPTH, ATTN_WIDTH)),
        "w_out": nrm(ks[9], (DEPTH, MIX_WIDTH, D_MODEL), MIX_WIDTH ** -0.5),
        "g_ffn_pre": gain(ks[10], (DEPTH, D_MODEL)),
        "g_ffn_post": gain(ks[11], (DEPTH, D_MODEL)),
        "w_up": nrm(ks[12], (DEPTH, D_MODEL, 2 * D_FF), D_MODEL ** -0.5),
        "conv_w": nrm(ks[13], (DEPTH, CONV_WIDTH, 2 * D_FF), CONV_WIDTH ** -0.5),
        "conv_b": nrm(ks[14], (DEPTH, 2 * D_FF), 0.02),
        "w_down": nrm(ks[15], (DEPTH, D_FF, D_MODEL), D_FF ** -0.5),
    }


def reference(x_prompt, x_sample, g_mix_pre, g_mix_post, w_in, pool_w, pool_scale, g_pool_out,
              g_attn_out, w_out, g_ffn_pre, g_ffn_post, w_up, conv_w, conv_b, w_down):
    def run(x):
        for l in range(DEPTH):
            x = encoder_layer(x, g_mix_pre[l], g_mix_post[l], w_in[l], pool_w[l], pool_scale[l],
                              g_pool_out[l], g_attn_out[l], w_out[l], g_ffn_pre[l], g_ffn_post[l],
                              w_up[l], conv_w[l], conv_b[l], w_down[l])
        return x

    y_prompt = run(x_prompt)
    y_sample = run(x_sample)
    return (y_prompt, y_sample)
```

```python
import functools

import jax
import jax.numpy as jnp
import numpy as np
from jax import lax
from jax.experimental import pallas as pl
from jax.experimental.pallas import tpu as pltpu

D_MODEL = 1024
POOL_WIDTH = 256
POOL_WINDOWS = (2, 4, 8, 16)
POOL_GROUP_DIM = 64
ATTN_WIDTH = 768
HEAD_DIM = 64
DILATIONS = (1, 4, 16)
BAND_RADIUS = 64
ROPE_THETA = 500000.0
ROPE_DIM = 16
D_FF = 2816
EPS = 1e-6
IN_WIDTH = POOL_WIDTH + 3 * ATTN_WIDTH

LANES = 128
N_LANE_GROUPS = ATTN_WIDTH // LANES
NEG_BIG = -1e30

PROJ_ROWS = 512
ATTN_ROWS = 2048
ATTN_HALO = BAND_RADIUS * DILATIONS[-1]
Q_CHUNK = 128
K_CHUNK = Q_CHUNK + 2 * BAND_RADIUS
MIX_ROWS = 512
FFN_ROWS = 512
FFN_CHUNK = 256
HALO_ROWS = 8

VMEM_LIMIT = 56 * 1024 * 1024


def _rms(x, g):
    ms = jnp.mean(x * x, axis=-1, keepdims=True)
    return x * lax.rsqrt(ms + EPS) * g


def _proj_kernel(x_ref, g_ref, w_ref, tab_ref, p_ref, q_ref, k_ref, v_ref):
    h = _rms(x_ref[...], g_ref[...]).astype(jnp.bfloat16)
    proj = jnp.dot(h, w_ref[...], preferred_element_type=jnp.float32)
    p_ref[...] = proj[:, :POOL_WIDTH]
    qc, qs1, qs2 = tab_ref[0], tab_ref[1], tab_ref[2]
    kc, ks1, ks2 = tab_ref[3], tab_ref[4], tab_ref[5]

    def rope(t, c, s1, s2):
        return t * c + pltpu.roll(t, LANES - 8, 1) * s1 + pltpu.roll(t, 8, 1) * s2

    for g in range(N_LANE_GROUPS):
        lo = POOL_WIDTH + g * LANES
        q_ref[g] = rope(proj[:, lo:lo + LANES], qc, qs1, qs2).astype(jnp.bfloat16)
        lo += ATTN_WIDTH
        k_ref[g] = rope(proj[:, lo:lo + LANES], kc, ks1, ks2).astype(jnp.bfloat16)
        lo += ATTN_WIDTH
        v_ref[g] = proj[:, lo:lo + LANES].astype(jnp.bfloat16)


def _rope_tables(seq):
    pos = jnp.arange(seq, dtype=jnp.float32)
    inv_freq = ROPE_THETA ** (-jnp.arange(0, ROPE_DIM, 2, dtype=jnp.float32) / ROPE_DIM)
    ang = pos[:, None] * inv_freq[None, :]
    cos, sin = jnp.cos(ang), jnp.sin(ang)
    half = ROPE_DIM // 2
    ones = jnp.ones((seq, HEAD_DIM - ROPE_DIM), jnp.float32)
    zeros = jnp.zeros((seq, HEAD_DIM - ROPE_DIM), jnp.float32)
    zh = jnp.zeros((seq, half), jnp.float32)
    c = jnp.concatenate([cos, cos, ones], axis=-1)
    s1 = jnp.concatenate([-sin, zh, zeros], axis=-1)
    s2 = jnp.concatenate([zh, sin, zeros], axis=-1)
    tabs = jnp.stack([jnp.tile(t, (1, LANES // HEAD_DIM)) for t in (c, s1, s2)])
    scale = HEAD_DIM ** -0.5
    return jnp.concatenate([tabs * scale, tabs], axis=0)


def _proj_call(x, g, w_bf16, tabs, seq):
    n = x.shape[0]
    t = PROJ_ROWS
    blocks_per_seq = seq // t
    qkv_shape = jax.ShapeDtypeStruct((N_LANE_GROUPS, n, LANES), jnp.bfloat16)
    qkv_spec = pl.BlockSpec((N_LANE_GROUPS, t, LANES), lambda i: (0, i, 0))
    return pl.pallas_call(
        _proj_kernel,
        grid=(n // t,),
        in_specs=[
            pl.BlockSpec((t, D_MODEL), lambda i: (i, 0)),
            pl.BlockSpec((1, D_MODEL), lambda i: (0, 0)),
            pl.BlockSpec((D_MODEL, IN_WIDTH), lambda i: (0, 0)),
            pl.BlockSpec((6, t, LANES), lambda i: (0, i % blocks_per_seq, 0)),
        ],
        out_specs=[pl.BlockSpec((t, POOL_WIDTH), lambda i: (i, 0)), qkv_spec, qkv_spec, qkv_spec],
        out_shape=[jax.ShapeDtypeStruct((n, POOL_WIDTH), jnp.float32), qkv_shape, qkv_shape, qkv_shape],
        compiler_params=pltpu.CompilerParams(
            dimension_semantics=("arbitrary",), vmem_limit_bytes=VMEM_LIMIT),
        name="proj",
    )(x, g, w_bf16, tabs)


def _attn_kernel(q_ref, k_ref, v_ref, o_ref, qw, kw, vw, outp, lsep, bias_ref, *, n_blocks):
    t = pl.program_id(2)
    rows = ATTN_ROWS
    t0 = pl.multiple_of(t * rows, rows)
    f32 = jnp.float32

    qw[...] = q_ref[...].astype(f32)
    kw[pl.ds(ATTN_HALO, rows), :] = k_ref[pl.ds(t0, rows), :].astype(f32)
    vw[pl.ds(ATTN_HALO, rows), :] = v_ref[pl.ds(t0, rows), :].astype(f32)
    zero_halo = jnp.zeros((ATTN_HALO, LANES), f32)

    def fill_halo(dst_lo, src_lo, have):
        if n_blocks == 1:
            kw[pl.ds(dst_lo, ATTN_HALO), :] = zero_halo
            vw[pl.ds(dst_lo, ATTN_HALO), :] = zero_halo
            return

        @pl.when(have)
        def _():
            src = pl.multiple_of(src_lo, ATTN_HALO)
            kw[pl.ds(dst_lo, ATTN_HALO), :] = k_ref[pl.ds(src, ATTN_HALO), :].astype(f32)
            vw[pl.ds(dst_lo, ATTN_HALO), :] = v_ref[pl.ds(src, ATTN_HALO), :].astype(f32)

        @pl.when(jnp.logical_not(have))
        def _():
            kw[pl.ds(dst_lo, ATTN_HALO), :] = zero_halo
            vw[pl.ds(dst_lo, ATTN_HALO), :] = zero_halo

    fill_halo(0, jnp.maximum(t0 - ATTN_HALO, 0), t > 0)
    fill_halo(ATTN_HALO + rows, jnp.minimum(t0 + rows, (n_blocks - 1) * rows), t < n_blocks - 1)

    qi = lax.broadcasted_iota(jnp.int32, (2 * Q_CHUNK, K_CHUNK), 0) % Q_CHUNK
    kj = lax.broadcasted_iota(jnp.int32, (2 * Q_CHUNK, K_CHUNK), 1)
    band = (kj >= qi) & (kj <= qi + 2 * BAND_RADIUS)
    for variant in range(4):
        ok = band
        if variant & 1:
            ok = ok & (kj >= BAND_RADIUS)
        if variant & 2:
            ok = ok & (kj < BAND_RADIUS + Q_CHUNK)
        bias_ref[variant] = jnp.where(ok, 0.0, NEG_BIG).astype(f32)

    lane = lax.broadcasted_iota(jnp.int32, (Q_CHUNK, LANES), 1)
    first_head = lane < HEAD_DIM

    for pi, d in enumerate(DILATIONS):
        n_chunks = rows // (d * Q_CHUNK)

        def body(it, carry, d=d, pi=pi, n_chunks=n_chunks):
            r = it // n_chunks
            c = it % n_chunks
            q_lo = r + c * (d * Q_CHUNK)
            k_lo = ATTN_HALO + q_lo - d * BAND_RADIUS
            if d == 1:
                q_idx, k_idx = pl.ds(q_lo, Q_CHUNK), pl.ds(k_lo, K_CHUNK)
            else:
                q_idx = pl.ds(q_lo, Q_CHUNK, stride=d)
                k_idx = pl.ds(k_lo, K_CHUNK, stride=d)
            qc = qw[q_idx, :]
            kc = kw[k_idx, :].astype(jnp.bfloat16)
            vc = vw[k_idx, :].astype(jnp.bfloat16)
            q2 = jnp.concatenate([jnp.where(first_head, qc, 0.0), jnp.where(first_head, 0.0, qc)],
                                 axis=0).astype(jnp.bfloat16)
            s = lax.dot_general(q2, kc, (((1,), (1,)), ((), ())), preferred_element_type=f32)
            is_first = jnp.logical_and(t == 0, c == 0).astype(jnp.int32)
            is_last = jnp.logical_and(t == n_blocks - 1, c == n_chunks - 1).astype(jnp.int32)
            s = s + bias_ref[is_first + 2 * is_last]
            m = jnp.max(s, axis=-1, keepdims=True)
            p = jnp.exp(s - m)
            l = jnp.sum(p, axis=-1, keepdims=True)
            o = jnp.dot(p.astype(jnp.bfloat16), vc, preferred_element_type=f32) / l
            lse = jnp.broadcast_to(m + jnp.log(l), (2 * Q_CHUNK, LANES))
            outp[pi, q_idx, :] = jnp.where(first_head, o[:Q_CHUNK], o[Q_CHUNK:])
            lsep[pi, q_idx, :] = jnp.where(first_head, lse[:Q_CHUNK], lse[Q_CHUNK:])
            return carry

        lax.fori_loop(0, rows // Q_CHUNK, body, 0)

    merge_rows = 256

    def merge(i, carry):
        sl = pl.ds(pl.multiple_of(i * merge_rows, merge_rows), merge_rows)
        l0, l1, l2 = lsep[0, sl, :], lsep[1, sl, :], lsep[2, sl, :]
        mx = jnp.maximum(jnp.maximum(l0, l1), l2)
        e0, e1, e2 = jnp.exp(l0 - mx), jnp.exp(l1 - mx), jnp.exp(l2 - mx)
        acc = e0 * outp[0, sl, :] + e1 * outp[1, sl, :] + e2 * outp[2, sl, :]
        o_ref[sl, :] = acc / (e0 + e1 + e2)
        return carry

    lax.fori_loop(0, rows // merge_rows, merge, 0)


def _attn_call(q, k, v, batch, seq):
    rows = ATTN_ROWS
    n_blocks = seq // rows
    q4 = q.reshape(N_LANE_GROUPS, batch, seq, LANES)
    k4 = k.reshape(N_LANE_GROUPS, batch, seq, LANES)
    v4 = v.reshape(N_LANE_GROUPS, batch, seq, LANES)
    kv_spec = pl.BlockSpec((None, None, seq, LANES), lambda b, g, t: (g, b, 0, 0))
    blk_spec = pl.BlockSpec((None, None, rows, LANES), lambda b, g, t: (g, b, t, 0))
    win = rows + 2 * ATTN_HALO
    return pl.pallas_call(
        functools.partial(_attn_kernel, n_blocks=n_blocks),
        grid=(batch, N_LANE_GROUPS, n_blocks),
        in_specs=[blk_spec, kv_spec, kv_spec],
        out_specs=blk_spec,
        out_shape=jax.ShapeDtypeStruct((N_LANE_GROUPS, batch, seq, LANES), jnp.float32),
        scratch_shapes=[
            pltpu.VMEM((rows, LANES), jnp.float32),
            pltpu.VMEM((win, LANES), jnp.float32),
            pltpu.VMEM((win, LANES), jnp.float32),
            pltpu.VMEM((len(DILATIONS), rows, LANES), jnp.float32),
            pltpu.VMEM((len(DILATIONS), rows, LANES), jnp.float32),
            pltpu.VMEM((4, 2 * Q_CHUNK, K_CHUNK), jnp.float32),
        ],
        compiler_params=pltpu.CompilerParams(
            dimension_semantics=("arbitrary", "arbitrary", "arbitrary"), vmem_limit_bytes=VMEM_LIMIT),
        name="attn",
    )(q4, k4, v4)


def _mix_kernel(x_ref, p_ref, pprev_ref, pnext_ref, a_ref, pw_ref, ps_ref, gp_ref, ga_ref,
                wo_ref, gpost_ref, o_ref, pbuf, mixed, *, seq):
    rows = MIX_ROWS
    f32 = jnp.float32
    i = pl.program_id(0)
    pos0 = (i * rows) % seq
    zero_rows = jnp.zeros((HALO_ROWS, POOL_WIDTH), f32)
    pbuf[pl.ds(0, HALO_ROWS), :] = jnp.where(pos0 > 0, pprev_ref[...], zero_rows)
    pbuf[pl.ds(HALO_ROWS, rows), :] = p_ref[...]
    pbuf[pl.ds(HALO_ROWS + rows, HALO_ROWS), :] = jnp.where(pos0 + rows < seq, pnext_ref[...], zero_rows)

    pos = pos0 + lax.broadcasted_iota(jnp.int32, (rows, LANES), 0)
    lane = lax.broadcasted_iota(jnp.int32, (rows, LANES), 1)
    means = []
    for half in range(2):
        w_small, w_big = POOL_WINDOWS[2 * half], POOL_WINDOWS[2 * half + 1]
        lanes = pl.ds(half * LANES, LANES)

        def wsum(lo, hi):
            acc = pbuf[pl.ds(HALO_ROWS + lo, rows), lanes]
            for off in range(lo + 1, hi):
                acc = acc + pbuf[pl.ds(HALO_ROWS + off, rows), lanes]
            return acc

        small = wsum(-(w_small // 2), w_small // 2)
        big = small + wsum(-(w_big // 2), -(w_small // 2)) + wsum(w_small // 2, w_big // 2)

        def count(w):
            return (jnp.minimum(pos + w // 2, seq) - jnp.maximum(pos - w // 2, 0)).astype(f32)

        first = lane < POOL_GROUP_DIM
        total = jnp.where(first, small, big)
        cnt = jnp.where(first, count(w_small), count(w_big))
        means.append(total / cnt - pbuf[pl.ds(HALO_ROWS, rows), lanes])
    pooled = jnp.concatenate(means, axis=-1).astype(jnp.bfloat16)
    pool_out = jnp.dot(pooled, pw_ref[...], preferred_element_type=f32) * ps_ref[...]
    mixed[:, pl.ds(0, POOL_WIDTH)] = _rms(pool_out, gp_ref[...]).astype(jnp.bfloat16)

    ssq = jnp.zeros((rows, 1), f32)
    for g in range(N_LANE_GROUPS):
        a = a_ref[g]
        ssq = ssq + jnp.sum(a * a, axis=-1, keepdims=True)
    inv = lax.rsqrt(ssq / ATTN_WIDTH + EPS)
    for g in range(N_LANE_GROUPS):
        ga = ga_ref[:, pl.ds(g * LANES, LANES)]
        mixed[:, pl.ds(POOL_WIDTH + g * LANES, LANES)] = (a_ref[g] * inv * ga).astype(jnp.bfloat16)

    y = jnp.dot(mixed[...], wo_ref[...], preferred_element_type=f32)
    o_ref[...] = x_ref[...] + _rms(y, gpost_ref[...])


def _mix_call(x, p_in, attn, pool_bd, pool_scale, g_pool, g_attn, w_out_bf16, g_post, seq):
    n = x.shape[0]
    rows = MIX_ROWS
    hb = rows // HALO_ROWS
    n_halo_blocks = n // HALO_ROWS
    const = lambda i: (0, 0)
    return pl.pallas_call(
        functools.partial(_mix_kernel, seq=seq),
        grid=(n // rows,),
        in_specs=[
            pl.BlockSpec((rows, D_MODEL), lambda i: (i, 0)),
            pl.BlockSpec((rows, POOL_WIDTH), lambda i: (i, 0)),
            pl.BlockSpec((HALO_ROWS, POOL_WIDTH), lambda i: (jnp.maximum(i * hb - 1, 0), 0)),
            pl.BlockSpec((HALO_ROWS, POOL_WIDTH),
                         lambda i: (jnp.minimum((i + 1) * hb, n_halo_blocks - 1), 0)),
            pl.BlockSpec((N_LANE_GROUPS, rows, LANES), lambda i: (0, i, 0)),
            pl.BlockSpec((POOL_WIDTH, POOL_WIDTH), const),
            pl.BlockSpec((1, POOL_WIDTH), const),
            pl.BlockSpec((1, POOL_WIDTH), const),
            pl.BlockSpec((1, ATTN_WIDTH), const),
            pl.BlockSpec((D_MODEL, D_MODEL), const),
            pl.BlockSpec((1, D_MODEL), const),
        ],
        out_specs=pl.BlockSpec((rows, D_MODEL), lambda i: (i, 0)),
        out_shape=jax.ShapeDtypeStruct((n, D_MODEL), jnp.float32),
        scratch_shapes=[
            pltpu.VMEM((rows + 2 * HALO_ROWS, POOL_WIDTH), jnp.float32),
            pltpu.VMEM((rows, D_MODEL), jnp.bfloat16),
        ],
        compiler_params=pltpu.CompilerParams(
            dimension_semantics=("arbitrary",), vmem_limit_bytes=VMEM_LIMIT),
        name="mix",
    )(x, p_in, p_in, p_in, attn, pool_bd, pool_scale, g_pool, g_attn, w_out_bf16, g_post)


def _ffn_kernel(x_ref, xprev_ref, xnext_ref, gpre_ref, wup_ref, cw_ref, cb_ref, wdn_ref, gpost_ref,
                o_ref, hbuf, ug, uv, *, seq):
    rows = FFN_ROWS
    f32 = jnp.float32
    i = pl.program_id(0)
    pos0 = (i * rows) % seq
    g = gpre_ref[...]
    hprev = jnp.where(pos0 > 0, _rms(xprev_ref[...], g), 0.0)
    hnext = jnp.where(pos0 + rows < seq, _rms(xnext_ref[...], g), 0.0)
    hbuf[pl.ds(0, HALO_ROWS), :] = hprev
    hbuf[pl.ds(HALO_ROWS, rows), :] = _rms(x_ref[...], g)
    hbuf[pl.ds(HALO_ROWS + rows, HALO_ROWS), :] = hnext
    h = hbuf[...].astype(jnp.bfloat16)

    def conv(u_ref, col):
        w = cw_ref[:, pl.ds(col, FFN_CHUNK)]
        acc = cb_ref[:, pl.ds(col, FFN_CHUNK)]
        for j in range(3):
            acc = acc + u_ref[pl.ds(HALO_ROWS - 1 + j, rows), :] * w[j:j + 1]
        return acc

    y = jnp.zeros((rows, D_MODEL), f32)
    for c in range(D_FF // FFN_CHUNK):
        col = c * FFN_CHUNK
        ug[...] = jnp.dot(h, wup_ref[:, pl.ds(col, FFN_CHUNK)], preferred_element_type=f32)
        uv[...] = jnp.dot(h, wup_ref[:, pl.ds(D_FF + col, FFN_CHUNK)], preferred_element_type=f32)
        gate = conv(ug, col)
        val = conv(uv, D_FF + col)
        gelu = 0.5 * gate * (1.0 + lax.erf(gate * (2.0 ** -0.5)))
        act = (gelu * val).astype(jnp.bfloat16)
        y = y + jnp.dot(act, wdn_ref[pl.ds(col, FFN_CHUNK), :], preferred_element_type=f32)
    o_ref[...] = x_ref[...] + _rms(y, gpost_ref[...])


def _ffn_call(x1, g_pre, w_up_bf16, conv_w, conv_b, w_down_bf16, g_post, seq):
    n = x1.shape[0]
    rows = FFN_ROWS
    hb = rows // HALO_ROWS
    n_halo_blocks = n // HALO_ROWS
    const = lambda i: (0, 0)
    return pl.pallas_call(
        functools.partial(_ffn_kernel, seq=seq),
        grid=(n // rows,),
        in_specs=[
            pl.BlockSpec((rows, D_MODEL), lambda i: (i, 0)),
            pl.BlockSpec((HALO_ROWS, D_MODEL), lambda i: (jnp.maximum(i * hb - 1, 0), 0)),
            pl.BlockSpec((HALO_ROWS, D_MODEL),
                         lambda i: (jnp.minimum((i + 1) * hb, n_halo_blocks - 1), 0)),
            pl.BlockSpec((1, D_MODEL), const),
            pl.BlockSpec((D_MODEL, 2 * D_FF), const, pipeline_mode=pl.Buffered(1)),
            pl.BlockSpec((3, 2 * D_FF), const),
            pl.BlockSpec((1, 2 * D_FF), const),
            pl.BlockSpec((D_FF, D_MODEL), const, pipeline_mode=pl.Buffered(1)),
            pl.BlockSpec((1, D_MODEL), const),
        ],
        out_specs=pl.BlockSpec((rows, D_MODEL), lambda i: (i, 0)),
        out_shape=jax.ShapeDtypeStruct((n, D_MODEL), jnp.float32),
        scratch_shapes=[
            pltpu.VMEM((rows + 2 * HALO_ROWS, D_MODEL), jnp.float32),
            pltpu.VMEM((rows + 2 * HALO_ROWS, FFN_CHUNK), jnp.float32),
            pltpu.VMEM((rows + 2 * HALO_ROWS, FFN_CHUNK), jnp.float32),
        ],
        compiler_params=pltpu.CompilerParams(
            dimension_semantics=("arbitrary",), vmem_limit_bytes=VMEM_LIMIT),
        name="ffn",
    )(x1, x1, x1, g_pre, w_up_bf16, conv_w, conv_b, w_down_bf16, g_post)


def _block_diag(pool_w):
    groups = pool_w.shape[0]
    bd = jnp.zeros((POOL_WIDTH, POOL_WIDTH), pool_w.dtype)
    for g in range(groups):
        lo = g * POOL_GROUP_DIM
        bd = bd.at[lo:lo + POOL_GROUP_DIM, lo:lo + POOL_GROUP_DIM].set(pool_w[g])
    return bd


def _layer(x, params):
    (g_mix_pre, g_mix_post, w_in, pool_bd, pool_scale, g_pool_out, g_attn_out, w_out,
     g_ffn_pre, g_ffn_post, w_up, conv_w, conv_b, w_down) = params
    batch, seq, _ = x.shape
    assert seq % ATTN_ROWS == 0 and seq % PROJ_ROWS == 0 and seq % MIX_ROWS == 0 and seq % FFN_ROWS == 0
    xf = x.reshape(batch * seq, D_MODEL)
    p_in, q, k, v = _proj_call(xf, g_mix_pre, w_in, _rope_tables(seq), seq)
    attn = _attn_call(q, k, v, batch, seq).reshape(N_LANE_GROUPS, batch * seq, LANES)
    x1 = _mix_call(xf, p_in, attn, pool_bd, pool_scale, g_pool_out, g_attn_out, w_out, g_mix_post, seq)
    y = _ffn_call(x1, g_ffn_pre, w_up, conv_w, conv_b, w_down, g_ffn_post, seq)
    return y.reshape(batch, seq, D_MODEL)


def kernel(x_prompt, x_sample, g_mix_pre, g_mix_post, w_in, pool_w, pool_scale, g_pool_out, g_attn_out,
           w_out, g_ffn_pre, g_ffn_post, w_up, conv_w, conv_b, w_down):
    depth = w_in.shape[0]
    bf16 = jnp.bfloat16

    def layer_params(l):
        return (g_mix_pre[l][None], g_mix_post[l][None], w_in[l].astype(bf16),
                _block_diag(pool_w[l]).astype(bf16), pool_scale[l][None], g_pool_out[l][None],
                g_attn_out[l][None], w_out[l].astype(bf16), g_ffn_pre[l][None], g_ffn_post[l][None],
                w_up[l].astype(bf16), conv_w[l], conv_b[l][None], w_down[l].astype(bf16))

    params = [layer_params(l) for l in range(depth)]

    def run(x):
        for p in params:
            x = _layer(x, p)
        return x

    return (run(x_prompt), run(x_sample))
```

```python
import functools

import jax
import jax.numpy as jnp
import numpy as np
from jax import lax
from jax.experimental import pallas as pl
from jax.experimental.pallas import tpu as pltpu

D_MODEL = 1024
POOL_WIDTH = 256
POOL_WINDOWS = (2, 4, 8, 16)
POOL_GROUP_DIM = 64
ATTN_WIDTH = 768
HEAD_DIM = 64
DILATIONS = (1, 4, 16)
BAND_RADIUS = 64
ROPE_THETA = 500000.0
ROPE_DIM = 16
D_FF = 2816
EPS = 1e-6
IN_WIDTH = POOL_WIDTH + 3 * ATTN_WIDTH

LANES = 128
N_LANE_GROUPS = ATTN_WIDTH // LANES
NEG_BIG = -1e30

PROJ_ROWS = 512
ATTN_ROWS = 2048
ATTN_HALO = BAND_RADIUS * DILATIONS[-1]
Q_CHUNK = 128
K_CHUNK = Q_CHUNK + 2 * BAND_RADIUS
CHUNK_UNROLL = 8
MIX_ROWS = 512
FFN_ROWS = 512
FFN_CHUNK = 256
HALO_ROWS = 8

VMEM_LIMIT = 56 * 1024 * 1024


def _rms(x, g):
    ms = jnp.mean(x * x, axis=-1, keepdims=True)
    return x * lax.rsqrt(ms + EPS) * g


def _proj_kernel(x_ref, g_ref, w_ref, tab_ref, p_ref, q_ref, k_ref, v_ref):
    h = _rms(x_ref[...], g_ref[...]).astype(jnp.bfloat16)
    proj = jnp.dot(h, w_ref[...], preferred_element_type=jnp.float32)
    p_ref[...] = proj[:, :POOL_WIDTH]
    qc, qs1, qs2 = tab_ref[0], tab_ref[1], tab_ref[2]
    kc, ks1, ks2 = tab_ref[3], tab_ref[4], tab_ref[5]

    def rope(t, c, s1, s2):
        return t * c + pltpu.roll(t, LANES - 8, 1) * s1 + pltpu.roll(t, 8, 1) * s2

    for g in range(N_LANE_GROUPS):
        lo = POOL_WIDTH + g * LANES
        q_ref[g] = rope(proj[:, lo:lo + LANES], qc, qs1, qs2).astype(jnp.bfloat16)
        lo += ATTN_WIDTH
        k_ref[g] = rope(proj[:, lo:lo + LANES], kc, ks1, ks2).astype(jnp.bfloat16)
        lo += ATTN_WIDTH
        v_ref[g] = proj[:, lo:lo + LANES].astype(jnp.bfloat16)


def _rope_tables(seq):
    pos = jnp.arange(seq, dtype=jnp.float32)
    inv_freq = ROPE_THETA ** (-jnp.arange(0, ROPE_DIM, 2, dtype=jnp.float32) / ROPE_DIM)
    ang = pos[:, None] * inv_freq[None, :]
    cos, sin = jnp.cos(ang), jnp.sin(ang)
    half = ROPE_DIM // 2
    ones = jnp.ones((seq, HEAD_DIM - ROPE_DIM), jnp.float32)
    zeros = jnp.zeros((seq, HEAD_DIM - ROPE_DIM), jnp.float32)
    zh = jnp.zeros((seq, half), jnp.float32)
    c = jnp.concatenate([cos, cos, ones], axis=-1)
    s1 = jnp.concatenate([-sin, zh, zeros], axis=-1)
    s2 = jnp.concatenate([zh, sin, zeros], axis=-1)
    tabs = jnp.stack([jnp.tile(t, (1, LANES // HEAD_DIM)) for t in (c, s1, s2)])
    scale = HEAD_DIM ** -0.5
    return jnp.concatenate([tabs * scale, tabs], axis=0)


def _proj_call(x, g, w_bf16, tabs, seq):
    n = x.shape[0]
    t = PROJ_ROWS
    blocks_per_seq = seq // t
    qkv_shape = jax.ShapeDtypeStruct((N_LANE_GROUPS, n, LANES), jnp.bfloat16)
    qkv_spec = pl.BlockSpec((N_LANE_GROUPS, t, LANES), lambda i: (0, i, 0))
    return pl.pallas_call(
        _proj_kernel,
        grid=(n // t,),
        in_specs=[
            pl.BlockSpec((t, D_MODEL), lambda i: (i, 0)),
            pl.BlockSpec((1, D_MODEL), lambda i: (0, 0)),
            pl.BlockSpec((D_MODEL, IN_WIDTH), lambda i: (0, 0)),
            pl.BlockSpec((6, t, LANES), lambda i: (0, i % blocks_per_seq, 0)),
        ],
        out_specs=[pl.BlockSpec((t, POOL_WIDTH), lambda i: (i, 0)), qkv_spec, qkv_spec, qkv_spec],
        out_shape=[jax.ShapeDtypeStruct((n, POOL_WIDTH), jnp.float32), qkv_shape, qkv_shape, qkv_shape],
        compiler_params=pltpu.CompilerParams(
            dimension_semantics=("arbitrary",), vmem_limit_bytes=VMEM_LIMIT),
        name="proj",
    )(x, g, w_bf16, tabs)


def _attn_kernel(q_ref, k_ref, v_ref, o_ref, qw, kw, vw, outp, lsep, bias_ref, *, n_blocks):
    t = pl.program_id(2)
    rows = ATTN_ROWS
    t0 = pl.multiple_of(t * rows, rows)
    f32 = jnp.float32

    qw[...] = q_ref[...].astype(f32)
    kw[pl.ds(ATTN_HALO, rows), :] = k_ref[pl.ds(t0, rows), :].astype(f32)
    vw[pl.ds(ATTN_HALO, rows), :] = v_ref[pl.ds(t0, rows), :].astype(f32)
    zero_halo = jnp.zeros((ATTN_HALO, LANES), f32)

    def fill_halo(dst_lo, src_lo, have):
        if n_blocks == 1:
            kw[pl.ds(dst_lo, ATTN_HALO), :] = zero_halo
            vw[pl.ds(dst_lo, ATTN_HALO), :] = zero_halo
            return

        @pl.when(have)
        def _():
            src = pl.multiple_of(src_lo, ATTN_HALO)
            kw[pl.ds(dst_lo, ATTN_HALO), :] = k_ref[pl.ds(src, ATTN_HALO), :].astype(f32)
            vw[pl.ds(dst_lo, ATTN_HALO), :] = v_ref[pl.ds(src, ATTN_HALO), :].astype(f32)

        @pl.when(jnp.logical_not(have))
        def _():
            kw[pl.ds(dst_lo, ATTN_HALO), :] = zero_halo
            vw[pl.ds(dst_lo, ATTN_HALO), :] = zero_halo

    fill_halo(0, jnp.maximum(t0 - ATTN_HALO, 0), t > 0)
    fill_halo(ATTN_HALO + rows, jnp.minimum(t0 + rows, (n_blocks - 1) * rows), t < n_blocks - 1)

    qi = lax.broadcasted_iota(jnp.int32, (2 * Q_CHUNK, K_CHUNK), 0) % Q_CHUNK
    kj = lax.broadcasted_iota(jnp.int32, (2 * Q_CHUNK, K_CHUNK), 1)
    band = (kj >= qi) & (kj <= qi + 2 * BAND_RADIUS)
    for variant in range(4):
        ok = band
        if variant & 1:
            ok = ok & (kj >= BAND_RADIUS)
        if variant & 2:
            ok = ok & (kj < BAND_RADIUS + Q_CHUNK)
        bias_ref[variant] = jnp.where(ok, 0.0, NEG_BIG).astype(f32)

    lane = lax.broadcasted_iota(jnp.int32, (Q_CHUNK, LANES), 1)
    first_head = lane < HEAD_DIM

    for pi, d in enumerate(DILATIONS):
        n_chunks = rows // (d * Q_CHUNK)

        def body(it, carry, d=d, pi=pi, n_chunks=n_chunks):
            r = it // n_chunks
            c = it % n_chunks
            q_lo = r + c * (d * Q_CHUNK)
            k_lo = ATTN_HALO + q_lo - d * BAND_RADIUS
            if d == 1:
                q_idx, k_idx = pl.ds(q_lo, Q_CHUNK), pl.ds(k_lo, K_CHUNK)
            else:
                q_idx = pl.ds(q_lo, Q_CHUNK, stride=d)
                k_idx = pl.ds(k_lo, K_CHUNK, stride=d)
            qc = qw[q_idx, :]
            kc = kw[k_idx, :].astype(jnp.bfloat16)
            vc = vw[k_idx, :].astype(jnp.bfloat16)
            q2 = jnp.concatenate([jnp.where(first_head, qc, 0.0), jnp.where(first_head, 0.0, qc)],
                                 axis=0).astype(jnp.bfloat16)
            s = lax.dot_general(q2, kc, (((1,), (1,)), ((), ())), preferred_element_type=f32)
            is_first = jnp.logical_and(t == 0, c == 0).astype(jnp.int32)
            is_last = jnp.logical_and(t == n_blocks - 1, c == n_chunks - 1).astype(jnp.int32)
            s = s + bias_ref[is_first + 2 * is_last]
            m = jnp.max(s, axis=-1, keepdims=True)
            p = jnp.exp(s - m)
            l = jnp.sum(p, axis=-1, keepdims=True)
            o = jnp.dot(p.astype(jnp.bfloat16), vc, preferred_element_type=f32) / l
            lse = jnp.broadcast_to(m + jnp.log(l), (2 * Q_CHUNK, LANES))
            outp[pi, q_idx, :] = jnp.where(first_head, o[:Q_CHUNK], o[Q_CHUNK:])
            lsep[pi, q_idx, :] = jnp.where(first_head, lse[:Q_CHUNK], lse[Q_CHUNK:])
            return carry

        lax.fori_loop(0, rows // Q_CHUNK, body, 0, unroll=CHUNK_UNROLL)

    merge_rows = 256

    def merge(i, carry):
        sl = pl.ds(pl.multiple_of(i * merge_rows, merge_rows), merge_rows)
        l0, l1, l2 = lsep[0, sl, :], lsep[1, sl, :], lsep[2, sl, :]
        mx = jnp.maximum(jnp.maximum(l0, l1), l2)
        e0, e1, e2 = jnp.exp(l0 - mx), jnp.exp(l1 - mx), jnp.exp(l2 - mx)
        acc = e0 * outp[0, sl, :] + e1 * outp[1, sl, :] + e2 * outp[2, sl, :]
        o_ref[sl, :] = acc / (e0 + e1 + e2)
        return carry

    lax.fori_loop(0, rows // merge_rows, merge, 0)


def _attn_call(q, k, v, batch, seq):
    rows = ATTN_ROWS
    n_blocks = seq // rows
    q4 = q.reshape(N_LANE_GROUPS, batch, seq, LANES)
    k4 = k.reshape(N_LANE_GROUPS, batch, seq, LANES)
    v4 = v.reshape(N_LANE_GROUPS, batch, seq, LANES)
    kv_spec = pl.BlockSpec((None, None, seq, LANES), lambda b, g, t: (g, b, 0, 0))
    blk_spec = pl.BlockSpec((None, None, rows, LANES), lambda b, g, t: (g, b, t, 0))
    win = rows + 2 * ATTN_HALO
    return pl.pallas_call(
        functools.partial(_attn_kernel, n_blocks=n_blocks),
        grid=(batch, N_LANE_GROUPS, n_blocks),
        in_specs=[blk_spec, kv_spec, kv_spec],
        out_specs=blk_spec,
        out_shape=jax.ShapeDtypeStruct((N_LANE_GROUPS, batch, seq, LANES), jnp.float32),
        scratch_shapes=[
            pltpu.VMEM((rows, LANES), jnp.float32),
            pltpu.VMEM((win, LANES), jnp.float32),
            pltpu.VMEM((win, LANES), jnp.float32),
            pltpu.VMEM((len(DILATIONS), rows, LANES), jnp.float32),
            pltpu.VMEM((len(DILATIONS), rows, LANES), jnp.float32),
            pltpu.VMEM((4, 2 * Q_CHUNK, K_CHUNK), jnp.float32),
        ],
        compiler_params=pltpu.CompilerParams(
            dimension_semantics=("arbitrary", "arbitrary", "arbitrary"), vmem_limit_bytes=VMEM_LIMIT),
        name="attn",
    )(q4, k4, v4)


def _mix_kernel(x_ref, p_ref, pprev_ref, pnext_ref, a_ref, pw_ref, ps_ref, gp_ref, ga_ref,
                wo_ref, gpost_ref, o_ref, pbuf, mixed, *, seq):
    rows = MIX_ROWS
    f32 = jnp.float32
    i = pl.program_id(0)
    pos0 = (i * rows) % seq
    zero_rows = jnp.zeros((HALO_ROWS, POOL_WIDTH), f32)
    pbuf[pl.ds(0, HALO_ROWS), :] = jnp.where(pos0 > 0, pprev_ref[...], zero_rows)
    pbuf[pl.ds(HALO_ROWS, rows), :] = p_ref[...]
    pbuf[pl.ds(HALO_ROWS + rows, HALO_ROWS), :] = jnp.where(pos0 + rows < seq, pnext_ref[...], zero_rows)

    pos = pos0 + lax.broadcasted_iota(jnp.int32, (rows, LANES), 0)
    lane = lax.broadcasted_iota(jnp.int32, (rows, LANES), 1)
    means = []
    for half in range(2):
        w_small, w_big = POOL_WINDOWS[2 * half], POOL_WINDOWS[2 * half + 1]
        lanes = pl.ds(half * LANES, LANES)

        def wsum(lo, hi):
            acc = pbuf[pl.ds(HALO_ROWS + lo, rows), lanes]
            for off in range(lo + 1, hi):
                acc = acc + pbuf[pl.ds(HALO_ROWS + off, rows), lanes]
            return acc

        small = wsum(-(w_small // 2), w_small // 2)
        big = small + wsum(-(w_big // 2), -(w_small // 2)) + wsum(w_small // 2, w_big // 2)

        def count(w):
            return (jnp.minimum(pos + w // 2, seq) - jnp.maximum(pos - w // 2, 0)).astype(f32)

        first = lane < POOL_GROUP_DIM
        total = jnp.where(first, small, big)
        cnt = jnp.where(first, count(w_small), count(w_big))
        means.append(total / cnt - pbuf[pl.ds(HALO_ROWS, rows), lanes])
    pooled = jnp.concatenate(means, axis=-1).astype(jnp.bfloat16)
    pool_out = jnp.dot(pooled, pw_ref[...], preferred_element_type=f32) * ps_ref[...]
    mixed[:, pl.ds(0, POOL_WIDTH)] = _rms(pool_out, gp_ref[...]).astype(jnp.bfloat16)

    ssq = jnp.zeros((rows, 1), f32)
    for g in range(N_LANE_GROUPS):
        a = a_ref[g]
        ssq = ssq + jnp.sum(a * a, axis=-1, keepdims=True)
    inv = lax.rsqrt(ssq / ATTN_WIDTH + EPS)
    for g in range(N_LANE_GROUPS):
        ga = ga_ref[:, pl.ds(g * LANES, LANES)]
        mixed[:, pl.ds(POOL_WIDTH + g * LANES, LANES)] = (a_ref[g] * inv * ga).astype(jnp.bfloat16)

    y = jnp.dot(mixed[...], wo_ref[...], preferred_element_type=f32)
    o_ref[...] = x_ref[...] + _rms(y, gpost_ref[...])


def _mix_call(x, p_in, attn, pool_bd, pool_scale, g_pool, g_attn, w_out_bf16, g_post, seq):
    n = x.shape[0]
    rows = MIX_ROWS
    hb = rows // HALO_ROWS
    n_halo_blocks = n // HALO_ROWS
    const = lambda i: (0, 0)
    return pl.pallas_call(
        functools.partial(_mix_kernel, seq=seq),
        grid=(n // rows,),
        in_specs=[
            pl.BlockSpec((rows, D_MODEL), lambda i: (i, 0)),
            pl.BlockSpec((rows, POOL_WIDTH), lambda i: (i, 0)),
            pl.BlockSpec((HALO_ROWS, POOL_WIDTH), lambda i: (jnp.maximum(i * hb - 1, 0), 0)),
            pl.BlockSpec((HALO_ROWS, POOL_WIDTH),
                         lambda i: (jnp.minimum((i + 1) * hb, n_halo_blocks - 1), 0)),
            pl.BlockSpec((N_LANE_GROUPS, rows, LANES), lambda i: (0, i, 0)),
            pl.BlockSpec((POOL_WIDTH, POOL_WIDTH), const),
            pl.BlockSpec((1, POOL_WIDTH), const),
            pl.BlockSpec((1, POOL_WIDTH), const),
            pl.BlockSpec((1, ATTN_WIDTH), const),
            pl.BlockSpec((D_MODEL, D_MODEL), const),
            pl.BlockSpec((1, D_MODEL), const),
        ],
        out_specs=pl.BlockSpec((rows, D_MODEL), lambda i: (i, 0)),
        out_shape=jax.ShapeDtypeStruct((n, D_MODEL), jnp.float32),
        scratch_shapes=[
            pltpu.VMEM((rows + 2 * HALO_ROWS, POOL_WIDTH), jnp.float32),
            pltpu.VMEM((rows, D_MODEL), jnp.bfloat16),
        ],
        compiler_params=pltpu.CompilerParams(
            dimension_semantics=("arbitrary",), vmem_limit_bytes=VMEM_LIMIT),
        name="mix",
    )(x, p_in, p_in, p_in, attn, pool_bd, pool_scale, g_pool, g_attn, w_out_bf16, g_post)


def _ffn_kernel(x_ref, xprev_ref, xnext_ref, gpre_ref, wup_ref, cw_ref, cb_ref, wdn_ref, gpost_ref,
                o_ref, hbuf, ug, uv, *, seq):
    rows = FFN_ROWS
    f32 = jnp.float32
    i = pl.program_id(0)
    pos0 = (i * rows) % seq
    g = gpre_ref[...]
    hprev = jnp.where(pos0 > 0, _rms(xprev_ref[...], g), 0.0)
    hnext = jnp.where(pos0 + rows < seq, _rms(xnext_ref[...], g), 0.0)
    hbuf[pl.ds(0, HALO_ROWS), :] = hprev
    hbuf[pl.ds(HALO_ROWS, rows), :] = _rms(x_ref[...], g)
    hbuf[pl.ds(HALO_ROWS + rows, HALO_ROWS), :] = hnext
    h = hbuf[...].astype(jnp.bfloat16)

    def conv(u_ref, col):
        w = cw_ref[:, pl.ds(col, FFN_CHUNK)]
        acc = cb_ref[:, pl.ds(col, FFN_CHUNK)]
        for j in range(3):
            acc = acc + u_ref[pl.ds(HALO_ROWS - 1 + j, rows), :] * w[j:j + 1]
        return acc

    y = jnp.zeros((rows, D_MODEL), f32)
    for c in range(D_FF // FFN_CHUNK):
        col = c * FFN_CHUNK
        ug[...] = jnp.dot(h, wup_ref[:, pl.ds(col, FFN_CHUNK)], preferred_element_type=f32)
        uv[...] = jnp.dot(h, wup_ref[:, pl.ds(D_FF + col, FFN_CHUNK)], preferred_element_type=f32)
        gate = conv(ug, col)
        val = conv(uv, D_FF + col)
        gelu = 0.5 * gate * (1.0 + lax.erf(gate * (2.0 ** -0.5)))
        act = (gelu * val).astype(jnp.bfloat16)
        y = y + jnp.dot(act, wdn_ref[pl.ds(col, FFN_CHUNK), :], preferred_element_type=f32)
    o_ref[...] = x_ref[...] + _rms(y, gpost_ref[...])


def _ffn_call(x1, g_pre, w_up_bf16, conv_w, conv_b, w_down_bf16, g_post, seq):
    n = x1.shape[0]
    rows = FFN_ROWS
    hb = rows // HALO_ROWS
    n_halo_blocks = n // HALO_ROWS
    const = lambda i: (0, 0)
    return pl.pallas_call(
        functools.partial(_ffn_kernel, seq=seq),
        grid=(n // rows,),
        in_specs=[
            pl.BlockSpec((rows, D_MODEL), lambda i: (i, 0)),
            pl.BlockSpec((HALO_ROWS, D_MODEL), lambda i: (jnp.maximum(i * hb - 1, 0), 0)),
            pl.BlockSpec((HALO_ROWS, D_MODEL),
                         lambda i: (jnp.minimum((i + 1) * hb, n_halo_blocks - 1), 0)),
            pl.BlockSpec((1, D_MODEL), const),
            pl.BlockSpec((D_MODEL, 2 * D_FF), const, pipeline_mode=pl.Buffered(1)),
            pl.BlockSpec((3, 2 * D_FF), const),
            pl.BlockSpec((1, 2 * D_FF), const),
            pl.BlockSpec((D_FF, D_MODEL), const, pipeline_mode=pl.Buffered(1)),
            pl.BlockSpec((1, D_MODEL), const),
        ],
        out_specs=pl.BlockSpec((rows, D_MODEL), lambda i: (i, 0)),
        out_shape=jax.ShapeDtypeStruct((n, D_MODEL), jnp.float32),
        scratch_shapes=[
            pltpu.VMEM((rows + 2 * HALO_ROWS, D_MODEL), jnp.float32),
            pltpu.VMEM((rows + 2 * HALO_ROWS, FFN_CHUNK), jnp.float32),
            pltpu.VMEM((rows + 2 * HALO_ROWS, FFN_CHUNK), jnp.float32),
        ],
        compiler_params=pltpu.CompilerParams(
            dimension_semantics=("arbitrary",), vmem_limit_bytes=VMEM_LIMIT),
        name="ffn",
    )(x1, x1, x1, g_pre, w_up_bf16, conv_w, conv_b, w_down_bf16, g_post)


def _block_diag(pool_w):
    groups = pool_w.shape[0]
    bd = jnp.zeros((POOL_WIDTH, POOL_WIDTH), pool_w.dtype)
    for g in range(groups):
        lo = g * POOL_GROUP_DIM
        bd = bd.at[lo:lo + POOL_GROUP_DIM, lo:lo + POOL_GROUP_DIM].set(pool_w[g])
    return bd


def _layer(x, params):
    (g_mix_pre, g_mix_post, w_in, pool_bd, pool_scale, g_pool_out, g_attn_out, w_out,
     g_ffn_pre, g_ffn_post, w_up, conv_w, conv_b, w_down) = params
    batch, seq, _ = x.shape
    assert seq % ATTN_ROWS == 0 and seq % PROJ_ROWS == 0 and seq % MIX_ROWS == 0 and seq % FFN_ROWS == 0
    xf = x.reshape(batch * seq, D_MODEL)
    p_in, q, k, v = _proj_call(xf, g_mix_pre, w_in, _rope_tables(seq), seq)
    attn = _attn_call(q, k, v, batch, seq).reshape(N_LANE_GROUPS, batch * seq, LANES)
    x1 = _mix_call(xf, p_in, attn, pool_bd, pool_scale, g_pool_out, g_attn_out, w_out, g_mix_post, seq)
    y = _ffn_call(x1, g_ffn_pre, w_up, conv_w, conv_b, w_down, g_ffn_post, seq)
    return y.reshape(batch, seq, D_MODEL)


def kernel(x_prompt, x_sample, g_mix_pre, g_mix_post, w_in, pool_w, pool_scale, g_pool_out, g_attn_out,
           w_out, g_ffn_pre, g_ffn_post, w_up, conv_w, conv_b, w_down):
    depth = w_in.shape[0]
    bf16 = jnp.bfloat16

    def layer_params(l):
        return (g_mix_pre[l][None], g_mix_post[l][None], w_in[l].astype(bf16),
                _block_diag(pool_w[l]).astype(bf16), pool_scale[l][None], g_pool_out[l][None],
                g_attn_out[l][None], w_out[l].astype(bf16), g_ffn_pre[l][None], g_ffn_post[l][None],
                w_up[l].astype(bf16), conv_w[l], conv_b[l][None], w_down[l].astype(bf16))

    params = [layer_params(l) for l in range(depth)]

    def run(x):
        for p in params:
            x = _layer(x, p)
        return x

    return (run(x_prompt), run(x_sample))
```

```python
import functools

import jax
import jax.numpy as jnp
import numpy as np
from jax import lax
from jax.experimental import pallas as pl
from jax.experimental.pallas import tpu as pltpu

D_MODEL = 1024
POOL_WIDTH = 256
POOL_WINDOWS = (2, 4, 8, 16)
POOL_GROUP_DIM = 64
ATTN_WIDTH = 768
HEAD_DIM = 64
DILATIONS = (1, 4, 16)
BAND_RADIUS = 64
ROPE_THETA = 500000.0
ROPE_DIM = 16
D_FF = 2816
EPS = 1e-6
IN_WIDTH = POOL_WIDTH + 3 * ATTN_WIDTH

LANES = 128
N_LANE_GROUPS = ATTN_WIDTH // LANES
NEG_BIG = -1e30

PROJ_ROWS = 512
ATTN_ROWS = 2048
ATTN_HALO = BAND_RADIUS * DILATIONS[-1]
Q_CHUNK = 128
K_CHUNK = Q_CHUNK + 2 * BAND_RADIUS
ATTN_GROUP = 4
MIX_ROWS = 512
FFN_ROWS = 512
FFN_CHUNK = 256
FFN_LOOKAHEAD = 2
FFN_DOWN_GROUP = 4
HALO_ROWS = 8

VMEM_LIMIT = 56 * 1024 * 1024


def _rms(x, g):
    ms = jnp.mean(x * x, axis=-1, keepdims=True)
    return x * lax.rsqrt(ms + EPS) * g


def _proj_kernel(x_ref, g_ref, w_ref, tab_ref, p_ref, q_ref, k_ref, v_ref):
    h = _rms(x_ref[...], g_ref[...]).astype(jnp.bfloat16)
    proj = jnp.dot(h, w_ref[...], preferred_element_type=jnp.float32)
    p_ref[...] = proj[:, :POOL_WIDTH]
    qc, qs1, qs2 = tab_ref[0], tab_ref[1], tab_ref[2]
    kc, ks1, ks2 = tab_ref[3], tab_ref[4], tab_ref[5]

    def rope(t, c, s1, s2):
        return t * c + pltpu.roll(t, LANES - 8, 1) * s1 + pltpu.roll(t, 8, 1) * s2

    for g in range(N_LANE_GROUPS):
        lo = POOL_WIDTH + g * LANES
        q_ref[g] = rope(proj[:, lo:lo + LANES], qc, qs1, qs2).astype(jnp.bfloat16)
        lo += ATTN_WIDTH
        k_ref[g] = rope(proj[:, lo:lo + LANES], kc, ks1, ks2).astype(jnp.bfloat16)
        lo += ATTN_WIDTH
        v_ref[g] = proj[:, lo:lo + LANES].astype(jnp.bfloat16)


def _rope_tables(seq):
    pos = jnp.arange(seq, dtype=jnp.float32)
    inv_freq = ROPE_THETA ** (-jnp.arange(0, ROPE_DIM, 2, dtype=jnp.float32) / ROPE_DIM)
    ang = pos[:, None] * inv_freq[None, :]
    cos, sin = jnp.cos(ang), jnp.sin(ang)
    half = ROPE_DIM // 2
    ones = jnp.ones((seq, HEAD_DIM - ROPE_DIM), jnp.float32)
    zeros = jnp.zeros((seq, HEAD_DIM - ROPE_DIM), jnp.float32)
    zh = jnp.zeros((seq, half), jnp.float32)
    c = jnp.concatenate([cos, cos, ones], axis=-1)
    s1 = jnp.concatenate([-sin, zh, zeros], axis=-1)
    s2 = jnp.concatenate([zh, sin, zeros], axis=-1)
    tabs = jnp.stack([jnp.tile(t, (1, LANES // HEAD_DIM)) for t in (c, s1, s2)])
    scale = HEAD_DIM ** -0.5
    return jnp.concatenate([tabs * scale, tabs], axis=0)


def _proj_call(x, g, w_bf16, tabs, seq):
    n = x.shape[0]
    t = PROJ_ROWS
    blocks_per_seq = seq // t
    qkv_shape = jax.ShapeDtypeStruct((N_LANE_GROUPS, n, LANES), jnp.bfloat16)
    qkv_spec = pl.BlockSpec((N_LANE_GROUPS, t, LANES), lambda i: (0, i, 0))
    return pl.pallas_call(
        _proj_kernel,
        grid=(n // t,),
        in_specs=[
            pl.BlockSpec((t, D_MODEL), lambda i: (i, 0)),
            pl.BlockSpec((1, D_MODEL), lambda i: (0, 0)),
            pl.BlockSpec((D_MODEL, IN_WIDTH), lambda i: (0, 0)),
            pl.BlockSpec((6, t, LANES), lambda i: (0, i % blocks_per_seq, 0)),
        ],
        out_specs=[pl.BlockSpec((t, POOL_WIDTH), lambda i: (i, 0)), qkv_spec, qkv_spec, qkv_spec],
        out_shape=[jax.ShapeDtypeStruct((n, POOL_WIDTH), jnp.float32), qkv_shape, qkv_shape, qkv_shape],
        compiler_params=pltpu.CompilerParams(
            dimension_semantics=("arbitrary",), vmem_limit_bytes=VMEM_LIMIT),
        name="proj",
    )(x, g, w_bf16, tabs)


def _attn_kernel(q_ref, k_ref, v_ref, o_ref, qw, kw, vw, s0, s1, s2, m0, m1, outp, mp, lp, bias_ref,
                 *, n_blocks):
    sbufs, mbufs = (s0, s1, s2), (m0, m1)
    t = pl.program_id(2)
    rows = ATTN_ROWS
    t0 = pl.multiple_of(t * rows, rows)
    f32 = jnp.float32

    qw[...] = q_ref[...].astype(f32)
    kw[pl.ds(ATTN_HALO, rows), :] = k_ref[pl.ds(t0, rows), :].astype(f32)
    vw[pl.ds(ATTN_HALO, rows), :] = v_ref[pl.ds(t0, rows), :].astype(f32)
    zero_halo = jnp.zeros((ATTN_HALO, LANES), f32)

    def fill_halo(dst_lo, src_lo, have):
        if n_blocks == 1:
            kw[pl.ds(dst_lo, ATTN_HALO), :] = zero_halo
            vw[pl.ds(dst_lo, ATTN_HALO), :] = zero_halo
            return

        @pl.when(have)
        def _():
            src = pl.multiple_of(src_lo, ATTN_HALO)
            kw[pl.ds(dst_lo, ATTN_HALO), :] = k_ref[pl.ds(src, ATTN_HALO), :].astype(f32)
            vw[pl.ds(dst_lo, ATTN_HALO), :] = v_ref[pl.ds(src, ATTN_HALO), :].astype(f32)

        @pl.when(jnp.logical_not(have))
        def _():
            kw[pl.ds(dst_lo, ATTN_HALO), :] = zero_halo
            vw[pl.ds(dst_lo, ATTN_HALO), :] = zero_halo

    fill_halo(0, jnp.maximum(t0 - ATTN_HALO, 0), t > 0)
    fill_halo(ATTN_HALO + rows, jnp.minimum(t0 + rows, (n_blocks - 1) * rows), t < n_blocks - 1)

    qi = lax.broadcasted_iota(jnp.int32, (2 * Q_CHUNK, K_CHUNK), 0) % Q_CHUNK
    kj = lax.broadcasted_iota(jnp.int32, (2 * Q_CHUNK, K_CHUNK), 1)
    band = (kj >= qi) & (kj <= qi + 2 * BAND_RADIUS)
    for variant in range(4):
        ok = band
        if variant & 1:
            ok = ok & (kj >= BAND_RADIUS)
        if variant & 2:
            ok = ok & (kj < BAND_RADIUS + Q_CHUNK)
        bias_ref[variant] = jnp.where(ok, 0.0, NEG_BIG).astype(f32)

    lane = lax.broadcasted_iota(jnp.int32, (Q_CHUNK, LANES), 1)
    first_head = lane < HEAD_DIM

    def pick_head(x):
        return jnp.where(first_head, x[:Q_CHUNK], x[Q_CHUNK:])

    tiles = []
    for pi, d in enumerate(DILATIONS):
        n_chunks = rows // (d * Q_CHUNK)
        for it in range(rows // Q_CHUNK):
            r, c = divmod(it, n_chunks)
            q_lo = r + c * (d * Q_CHUNK)
            k_lo = ATTN_HALO + q_lo - d * BAND_RADIUS
            if d == 1:
                q_idx, k_idx = pl.ds(q_lo, Q_CHUNK), pl.ds(k_lo, K_CHUNK)
            else:
                q_idx = pl.ds(q_lo, Q_CHUNK, stride=d)
                k_idx = pl.ds(k_lo, K_CHUNK, stride=d)
            tiles.append((pi, q_idx, k_idx, c == 0, c == n_chunks - 1))
    groups = [tiles[i:i + ATTN_GROUP] for i in range(0, len(tiles), ATTN_GROUP)]

    def bias_variant(at_start, at_end):
        if n_blocks == 1:
            return int(at_start) + 2 * int(at_end)
        first = (t == 0).astype(jnp.int32) if at_start else 0
        last = (t == n_blocks - 1).astype(jnp.int32) if at_end else 0
        return first + 2 * last

    dyn_zero = jnp.minimum(t, 0)

    def score_stage(gi):
        for j, (pi, q_idx, k_idx, _, _) in enumerate(groups[gi]):
            qc = qw[q_idx, :]
            kc = kw[k_idx, :].astype(jnp.bfloat16)
            q2 = jnp.concatenate([jnp.where(first_head, qc, 0.0), jnp.where(first_head, 0.0, qc)],
                                 axis=0).astype(jnp.bfloat16)
            sbufs[gi % 3][j + dyn_zero] = lax.dot_general(q2, kc, (((1,), (1,)), ((), ())),
                                              preferred_element_type=f32)

    def masked_scores(gi, j, at_start, at_end):
        return sbufs[gi % 3][j + dyn_zero] + bias_ref[bias_variant(at_start, at_end)]

    def max_stage(gi):
        for j, (pi, q_idx, k_idx, at_start, at_end) in enumerate(groups[gi]):
            m = jnp.max(masked_scores(gi, j, at_start, at_end), axis=-1, keepdims=True)
            mbufs[gi % 2][j + dyn_zero] = jnp.broadcast_to(m, (2 * Q_CHUNK, LANES))

    def value_stage(gi):
        for j, (pi, q_idx, k_idx, at_start, at_end) in enumerate(groups[gi]):
            mb = mbufs[gi % 2][j + dyn_zero]
            p = jnp.exp(masked_scores(gi, j, at_start, at_end) - jnp.concatenate([mb, mb], axis=1))
            lb = jnp.broadcast_to(jnp.sum(p, axis=-1, keepdims=True), (2 * Q_CHUNK, LANES))
            vc = vw[k_idx, :].astype(jnp.bfloat16)
            o = jnp.dot(p.astype(jnp.bfloat16), vc, preferred_element_type=f32)
            outp[pi, q_idx, :] = pick_head(o)
            mp[pi, q_idx, :] = pick_head(mb)
            lp[pi, q_idx, :] = pick_head(lb)

    n_groups = len(groups)
    for step in range(n_groups + 2):
        if step < n_groups:
            score_stage(step)
        if 0 <= step - 1 < n_groups:
            max_stage(step - 1)
        if 0 <= step - 2 < n_groups:
            value_stage(step - 2)

    merge_rows = 64
    for i in range(rows // merge_rows):
        sl = pl.ds(i * merge_rows, merge_rows)
        m0, m1, m2 = mp[0, sl, :], mp[1, sl, :], mp[2, sl, :]
        mx = jnp.maximum(jnp.maximum(m0, m1), m2)
        e0, e1, e2 = jnp.exp(m0 - mx), jnp.exp(m1 - mx), jnp.exp(m2 - mx)
        num = e0 * outp[0, sl, :] + e1 * outp[1, sl, :] + e2 * outp[2, sl, :]
        den = e0 * lp[0, sl, :] + e1 * lp[1, sl, :] + e2 * lp[2, sl, :]
        o_ref[sl, :] = num / den


def _attn_call(q, k, v, batch, seq):
    rows = ATTN_ROWS
    n_blocks = seq // rows
    q4 = q.reshape(N_LANE_GROUPS, batch, seq, LANES)
    k4 = k.reshape(N_LANE_GROUPS, batch, seq, LANES)
    v4 = v.reshape(N_LANE_GROUPS, batch, seq, LANES)
    kv_spec = pl.BlockSpec((None, None, seq, LANES), lambda b, g, t: (g, b, 0, 0))
    blk_spec = pl.BlockSpec((None, None, rows, LANES), lambda b, g, t: (g, b, t, 0))
    win = rows + 2 * ATTN_HALO
    return pl.pallas_call(
        functools.partial(_attn_kernel, n_blocks=n_blocks),
        grid=(batch, N_LANE_GROUPS, n_blocks),
        in_specs=[blk_spec, kv_spec, kv_spec],
        out_specs=blk_spec,
        out_shape=jax.ShapeDtypeStruct((N_LANE_GROUPS, batch, seq, LANES), jnp.float32),
        scratch_shapes=[
            pltpu.VMEM((rows, LANES), jnp.float32),
            pltpu.VMEM((win, LANES), jnp.float32),
            pltpu.VMEM((win, LANES), jnp.float32),
            *[pltpu.VMEM((ATTN_GROUP, 2 * Q_CHUNK, K_CHUNK), jnp.float32)] * 3,
            *[pltpu.VMEM((ATTN_GROUP, 2 * Q_CHUNK, LANES), jnp.float32)] * 2,
            pltpu.VMEM((len(DILATIONS), rows, LANES), jnp.float32),
            pltpu.VMEM((len(DILATIONS), rows, LANES), jnp.float32),
            pltpu.VMEM((len(DILATIONS), rows, LANES), jnp.float32),
            pltpu.VMEM((4, 2 * Q_CHUNK, K_CHUNK), jnp.float32),
        ],
        compiler_params=pltpu.CompilerParams(
            dimension_semantics=("arbitrary", "arbitrary", "arbitrary"), vmem_limit_bytes=VMEM_LIMIT),
        name="attn",
    )(q4, k4, v4)


def _mix_kernel(x_ref, p_ref, pprev_ref, pnext_ref, a_ref, pw_ref, ps_ref, gp_ref, ga_ref,
                wo_ref, gpost_ref, o_ref, pbuf, mixed, *, seq):
    rows = MIX_ROWS
    f32 = jnp.float32
    i = pl.program_id(0)
    pos0 = (i * rows) % seq
    zero_rows = jnp.zeros((HALO_ROWS, POOL_WIDTH), f32)
    pbuf[pl.ds(0, HALO_ROWS), :] = jnp.where(pos0 > 0, pprev_ref[...], zero_rows)
    pbuf[pl.ds(HALO_ROWS, rows), :] = p_ref[...]
    pbuf[pl.ds(HALO_ROWS + rows, HALO_ROWS), :] = jnp.where(pos0 + rows < seq, pnext_ref[...], zero_rows)

    pos = pos0 + lax.broadcasted_iota(jnp.int32, (rows, LANES), 0)
    lane = lax.broadcasted_iota(jnp.int32, (rows, LANES), 1)
    means = []
    for half in range(2):
        w_small, w_big = POOL_WINDOWS[2 * half], POOL_WINDOWS[2 * half + 1]
        lanes = pl.ds(half * LANES, LANES)

        def wsum(lo, hi):
            acc = pbuf[pl.ds(HALO_ROWS + lo, rows), lanes]
            for off in range(lo + 1, hi):
                acc = acc + pbuf[pl.ds(HALO_ROWS + off, rows), lanes]
            return acc

        small = wsum(-(w_small // 2), w_small // 2)
        big = small + wsum(-(w_big // 2), -(w_small // 2)) + wsum(w_small // 2, w_big // 2)

        def count(w):
            return (jnp.minimum(pos + w // 2, seq) - jnp.maximum(pos - w // 2, 0)).astype(f32)

        first = lane < POOL_GROUP_DIM
        total = jnp.where(first, small, big)
        cnt = jnp.where(first, count(w_small), count(w_big))
        means.append(total / cnt - pbuf[pl.ds(HALO_ROWS, rows), lanes])
    pooled = jnp.concatenate(means, axis=-1).astype(jnp.bfloat16)
    pool_out = jnp.dot(pooled, pw_ref[...], preferred_element_type=f32) * ps_ref[...]
    mixed[:, pl.ds(0, POOL_WIDTH)] = _rms(pool_out, gp_ref[...]).astype(jnp.bfloat16)

    ssq = jnp.zeros((rows, 1), f32)
    for g in range(N_LANE_GROUPS):
        a = a_ref[g]
        ssq = ssq + jnp.sum(a * a, axis=-1, keepdims=True)
    inv = lax.rsqrt(ssq / ATTN_WIDTH + EPS)
    for g in range(N_LANE_GROUPS):
        ga = ga_ref[:, pl.ds(g * LANES, LANES)]
        mixed[:, pl.ds(POOL_WIDTH + g * LANES, LANES)] = (a_ref[g] * inv * ga).astype(jnp.bfloat16)

    y = jnp.dot(mixed[...], wo_ref[...], preferred_element_type=f32)
    o_ref[...] = x_ref[...] + _rms(y, gpost_ref[...])


def _mix_call(x, p_in, attn, pool_bd, pool_scale, g_pool, g_attn, w_out_bf16, g_post, seq):
    n = x.shape[0]
    rows = MIX_ROWS
    hb = rows // HALO_ROWS
    n_halo_blocks = n // HALO_ROWS
    const = lambda i: (0, 0)
    return pl.pallas_call(
        functools.partial(_mix_kernel, seq=seq),
        grid=(n // rows,),
        in_specs=[
            pl.BlockSpec((rows, D_MODEL), lambda i: (i, 0)),
            pl.BlockSpec((rows, POOL_WIDTH), lambda i: (i, 0)),
            pl.BlockSpec((HALO_ROWS, POOL_WIDTH), lambda i: (jnp.maximum(i * hb - 1, 0), 0)),
            pl.BlockSpec((HALO_ROWS, POOL_WIDTH),
                         lambda i: (jnp.minimum((i + 1) * hb, n_halo_blocks - 1), 0)),
            pl.BlockSpec((N_LANE_GROUPS, rows, LANES), lambda i: (0, i, 0)),
            pl.BlockSpec((POOL_WIDTH, POOL_WIDTH), const),
            pl.BlockSpec((1, POOL_WIDTH), const),
            pl.BlockSpec((1, POOL_WIDTH), const),
            pl.BlockSpec((1, ATTN_WIDTH), const),
            pl.BlockSpec((D_MODEL, D_MODEL), const),
            pl.BlockSpec((1, D_MODEL), const),
        ],
        out_specs=pl.BlockSpec((rows, D_MODEL), lambda i: (i, 0)),
        out_shape=jax.ShapeDtypeStruct((n, D_MODEL), jnp.float32),
        scratch_shapes=[
            pltpu.VMEM((rows + 2 * HALO_ROWS, POOL_WIDTH), jnp.float32),
            pltpu.VMEM((rows, D_MODEL), jnp.bfloat16),
        ],
        compiler_params=pltpu.CompilerParams(
            dimension_semantics=("arbitrary",), vmem_limit_bytes=VMEM_LIMIT),
        name="mix",
    )(x, p_in, p_in, p_in, attn, pool_bd, pool_scale, g_pool, g_attn, w_out_bf16, g_post)


def _ffn_kernel(x_ref, xprev_ref, xnext_ref, gpre_ref, wup_ref, cw_ref, cb_ref, wdn_ref, gpost_ref,
                o_ref, hbuf, *bufs, seq):
    n_chunks = D_FF // FFN_CHUNK
    ubufs, abufs = bufs[:n_chunks], bufs[n_chunks:]
    rows = FFN_ROWS
    f32 = jnp.float32
    i = pl.program_id(0)
    pos0 = (i * rows) % seq
    g = gpre_ref[...]
    hprev = jnp.where(pos0 > 0, _rms(xprev_ref[...], g), 0.0)
    hnext = jnp.where(pos0 + rows < seq, _rms(xnext_ref[...], g), 0.0)
    hbuf[pl.ds(0, HALO_ROWS), :] = hprev
    hbuf[pl.ds(HALO_ROWS, rows), :] = _rms(x_ref[...], g)
    hbuf[pl.ds(HALO_ROWS + rows, HALO_ROWS), :] = hnext
    h = hbuf[...].astype(jnp.bfloat16)

    def conv(c, half):
        col = half * D_FF + c * FFN_CHUNK
        w = cw_ref[:, pl.ds(col, FFN_CHUNK)]
        acc = cb_ref[:, pl.ds(col, FFN_CHUNK)]
        for j in range(3):
            u = ubufs[c][pl.ds(HALO_ROWS - 1 + j, rows), pl.ds(half * FFN_CHUNK, FFN_CHUNK)]
            acc = acc + u * w[j:j + 1]
        return acc

    def up(c):
        for half in range(2):
            col = half * D_FF + c * FFN_CHUNK
            ubufs[c][:, pl.ds(half * FFN_CHUNK, FFN_CHUNK)] = jnp.dot(
                h, wup_ref[:, pl.ds(col, FFN_CHUNK)], preferred_element_type=f32)

    def down(g, n_in_group):
        width = n_in_group * FFN_CHUNK
        w = wdn_ref[pl.ds(g * FFN_DOWN_GROUP * FFN_CHUNK, width), :]
        return jnp.dot(abufs[g][:, pl.ds(0, width)], w, preferred_element_type=f32)

    for c in range(min(FFN_LOOKAHEAD, n_chunks)):
        up(c)
    y = None
    for c in range(n_chunks):
        if c + FFN_LOOKAHEAD < n_chunks:
            up(c + FFN_LOOKAHEAD)
        gate = conv(c, 0)
        val = conv(c, 1)
        gelu = 0.5 * gate * (1.0 + lax.erf(gate * (2.0 ** -0.5)))
        g, j = divmod(c, FFN_DOWN_GROUP)
        abufs[g][:, pl.ds(j * FFN_CHUNK, FFN_CHUNK)] = (gelu * val).astype(jnp.bfloat16)
        if j + 1 == FFN_DOWN_GROUP or c + 1 == n_chunks:
            part = down(g, j + 1)
            y = part if y is None else y + part
    o_ref[...] = x_ref[...] + _rms(y, gpost_ref[...])


def _ffn_call(x1, g_pre, w_up_bf16, conv_w, conv_b, w_down_bf16, g_post, seq):
    n = x1.shape[0]
    rows = FFN_ROWS
    hb = rows // HALO_ROWS
    n_halo_blocks = n // HALO_ROWS
    const = lambda i: (0, 0)
    return pl.pallas_call(
        functools.partial(_ffn_kernel, seq=seq),
        grid=(n // rows,),
        in_specs=[
            pl.BlockSpec((rows, D_MODEL), lambda i: (i, 0)),
            pl.BlockSpec((HALO_ROWS, D_MODEL), lambda i: (jnp.maximum(i * hb - 1, 0), 0)),
            pl.BlockSpec((HALO_ROWS, D_MODEL),
                         lambda i: (jnp.minimum((i + 1) * hb, n_halo_blocks - 1), 0)),
            pl.BlockSpec((1, D_MODEL), const),
            pl.BlockSpec((D_MODEL, 2 * D_FF), const, pipeline_mode=pl.Buffered(1)),
            pl.BlockSpec((3, 2 * D_FF), const),
            pl.BlockSpec((1, 2 * D_FF), const),
            pl.BlockSpec((D_FF, D_MODEL), const, pipeline_mode=pl.Buffered(1)),
            pl.BlockSpec((1, D_MODEL), const),
        ],
        out_specs=pl.BlockSpec((rows, D_MODEL), lambda i: (i, 0)),
        out_shape=jax.ShapeDtypeStruct((n, D_MODEL), jnp.float32),
        scratch_shapes=[
            pltpu.VMEM((rows + 2 * HALO_ROWS, D_MODEL), jnp.float32),
            *[pltpu.VMEM((rows + 2 * HALO_ROWS, 2 * FFN_CHUNK), jnp.float32)] * (D_FF // FFN_CHUNK),
            *[pltpu.VMEM((rows, FFN_DOWN_GROUP * FFN_CHUNK), jnp.bfloat16)]
            * pl.cdiv(D_FF // FFN_CHUNK, FFN_DOWN_GROUP),
        ],
        compiler_params=pltpu.CompilerParams(
            dimension_semantics=("arbitrary",), vmem_limit_bytes=VMEM_LIMIT),
        name="ffn",
    )(x1, x1, x1, g_pre, w_up_bf16, conv_w, conv_b, w_down_bf16, g_post)


def _block_diag(pool_w):
    groups = pool_w.shape[0]
    bd = jnp.zeros((POOL_WIDTH, POOL_WIDTH), pool_w.dtype)
    for g in range(groups):
        lo = g * POOL_GROUP_DIM
        bd = bd.at[lo:lo + POOL_GROUP_DIM, lo:lo + POOL_GROUP_DIM].set(pool_w[g])
    return bd


def _layer(x, params):
    (g_mix_pre, g_mix_post, w_in, pool_bd, pool_scale, g_pool_out, g_attn_out, w_out,
     g_ffn_pre, g_ffn_post, w_up, conv_w, conv_b, w_down) = params
    batch, seq, _ = x.shape
    assert seq % ATTN_ROWS == 0 and seq % PROJ_ROWS == 0 and seq % MIX_ROWS == 0 and seq % FFN_ROWS == 0
    xf = x.reshape(batch * seq, D_MODEL)
    p_in, q, k, v = _proj_call(xf, g_mix_pre, w_in, _rope_tables(seq), seq)
    attn = _attn_call(q, k, v, batch, seq).reshape(N_LANE_GROUPS, batch * seq, LANES)
    x1 = _mix_call(xf, p_in, attn, pool_bd, pool_scale, g_pool_out, g_attn_out, w_out, g_mix_post, seq)
    y = _ffn_call(x1, g_ffn_pre, w_up, conv_w, conv_b, w_down, g_ffn_post, seq)
    return y.reshape(batch, seq, D_MODEL)


def kernel(x_prompt, x_sample, g_mix_pre, g_mix_post, w_in, pool_w, pool_scale, g_pool_out, g_attn_out,
           w_out, g_ffn_pre, g_ffn_post, w_up, conv_w, conv_b, w_down):
    depth = w_in.shape[0]
    bf16 = jnp.bfloat16

    def layer_params(l):
        return (g_mix_pre[l][None], g_mix_post[l][None], w_in[l].astype(bf16),
                _block_diag(pool_w[l]).astype(bf16), pool_scale[l][None], g_pool_out[l][None],
                g_attn_out[l][None], w_out[l].astype(bf16), g_ffn_pre[l][None], g_ffn_post[l][None],
                w_up[l].astype(bf16), conv_w[l], conv_b[l][None], w_down[l].astype(bf16))

    params = [layer_params(l) for l in range(depth)]

    def run(x):
        for p in params:
            x = _layer(x, p)
        return x

    return (run(x_prompt), run(x_sample))
```

```python
import functools
import math

import jax
import jax.numpy as jnp
from jax import lax
from jax.experimental import pallas as pl
from jax.experimental.pallas import tpu as pltpu

D_MODEL = 1024
POOL_WIDTH = 256
POOL_WINDOWS = (2, 4, 8, 16)
POOL_GROUP_DIM = 64
ATTN_WIDTH = 768
HEAD_DIM = 64
DILATIONS = (1, 4, 16)
BAND_RADIUS = 64
ROPE_THETA = 500000.0
ROPE_DIM = 16
D_FF = 2816
EPS = 1e-6
IN_WIDTH = POOL_WIDTH + 3 * ATTN_WIDTH

LANES = 128
N_LANE_GROUPS = ATTN_WIDTH // LANES
NEG_BIG = -1e30

PROJ_ROWS = 512
ATTN_ROWS = 2048
Q_CHUNK = 128
K_CHUNK = Q_CHUNK + 2 * BAND_RADIUS
MIX_ROWS = 512
FFN_ROWS = 512
FFN_CHUNK = 256
FFN_LOOKAHEAD = 2
FFN_DOWN_GROUP = 4
HALO_ROWS = 8

VMEM_LIMIT = 56 * 1024 * 1024


def _rms(x, g):
    ms = jnp.mean(x * x, axis=-1, keepdims=True)
    return x * lax.rsqrt(ms + EPS) * g


def _proj_kernel(x_ref, g_ref, w_ref, tab_ref, p_ref, q_ref, k_ref, v_ref):
    h = _rms(x_ref[...], g_ref[...]).astype(jnp.bfloat16)
    proj = jnp.dot(h, w_ref[...], preferred_element_type=jnp.float32)
    p_ref[...] = proj[:, :POOL_WIDTH]
    qc, qs1, qs2 = tab_ref[0], tab_ref[1], tab_ref[2]
    kc, ks1, ks2 = tab_ref[3], tab_ref[4], tab_ref[5]

    def rope(t, c, s1, s2):
        return t * c + pltpu.roll(t, LANES - 8, 1) * s1 + pltpu.roll(t, 8, 1) * s2

    for g in range(N_LANE_GROUPS):
        lo = POOL_WIDTH + g * LANES
        q_ref[g] = rope(proj[:, lo:lo + LANES], qc, qs1, qs2).astype(jnp.bfloat16)
        lo += ATTN_WIDTH
        k_ref[g] = rope(proj[:, lo:lo + LANES], kc, ks1, ks2).astype(jnp.bfloat16)
        lo += ATTN_WIDTH
        v_ref[g] = proj[:, lo:lo + LANES].astype(jnp.bfloat16)


def _rope_tables(seq):
    pos = jnp.arange(seq, dtype=jnp.float32)
    inv_freq = ROPE_THETA ** (-jnp.arange(0, ROPE_DIM, 2, dtype=jnp.float32) / ROPE_DIM)
    ang = pos[:, None] * inv_freq[None, :]
    cos, sin = jnp.cos(ang), jnp.sin(ang)
    half = ROPE_DIM // 2
    ones = jnp.ones((seq, HEAD_DIM - ROPE_DIM), jnp.float32)
    zeros = jnp.zeros((seq, HEAD_DIM - ROPE_DIM), jnp.float32)
    zh = jnp.zeros((seq, half), jnp.float32)
    c = jnp.concatenate([cos, cos, ones], axis=-1)
    s1 = jnp.concatenate([-sin, zh, zeros], axis=-1)
    s2 = jnp.concatenate([zh, sin, zeros], axis=-1)
    tabs = jnp.stack([jnp.tile(t, (1, LANES // HEAD_DIM)) for t in (c, s1, s2)])
    scale = HEAD_DIM ** -0.5 * math.log2(math.e)
    return jnp.concatenate([tabs * scale, tabs], axis=0)


def _proj_call(x, g, w_bf16, tabs, seq):
    n = x.shape[0]
    t = PROJ_ROWS
    blocks_per_seq = seq // t
    qkv_shape = jax.ShapeDtypeStruct((N_LANE_GROUPS, n, LANES), jnp.bfloat16)
    qkv_spec = pl.BlockSpec((N_LANE_GROUPS, t, LANES), lambda i: (0, i, 0))
    return pl.pallas_call(
        _proj_kernel,
        grid=(n // t,),
        in_specs=[
            pl.BlockSpec((t, D_MODEL), lambda i: (i, 0)),
            pl.BlockSpec((1, D_MODEL), lambda i: (0, 0)),
            pl.BlockSpec((D_MODEL, IN_WIDTH), lambda i: (0, 0)),
            pl.BlockSpec((6, t, LANES), lambda i: (0, i % blocks_per_seq, 0)),
        ],
        out_specs=[pl.BlockSpec((t, POOL_WIDTH), lambda i: (i, 0)), qkv_spec, qkv_spec, qkv_spec],
        out_shape=[jax.ShapeDtypeStruct((n, POOL_WIDTH), jnp.float32), qkv_shape, qkv_shape, qkv_shape],
        compiler_params=pltpu.CompilerParams(
            dimension_semantics=("arbitrary",), vmem_limit_bytes=VMEM_LIMIT),
        name="proj",
    )(x, g, w_bf16, tabs)


def _attn_kernel(*refs, n_blocks):
    n_pat = len(DILATIONS)
    q_refs = refs[:n_pat]
    k_refs = [refs[n_pat + 3 * i:n_pat + 3 * i + 3] for i in range(n_pat)]
    v_refs = [refs[4 * n_pat + 3 * i:4 * n_pat + 3 * i + 3] for i in range(n_pat)]
    o_ref, tbuf, mpat, mall, opat, lpat, bias_ref = refs[7 * n_pat:]
    t = pl.program_id(2)
    rows = ATTN_ROWS
    f32 = jnp.float32
    bf16 = jnp.bfloat16

    qi = lax.broadcasted_iota(jnp.int32, (2 * Q_CHUNK, K_CHUNK), 0) % Q_CHUNK
    kj = lax.broadcasted_iota(jnp.int32, (2 * Q_CHUNK, K_CHUNK), 1)
    band = (kj >= qi) & (kj <= qi + 2 * BAND_RADIUS)
    for variant in range(4):
        ok = band
        if variant & 1:
            ok = ok & (kj >= BAND_RADIUS)
        if variant & 2:
            ok = ok & (kj < BAND_RADIUS + Q_CHUNK)
        bias_ref[variant] = jnp.where(ok, 0.0, NEG_BIG).astype(f32)

    lane = lax.broadcasted_iota(jnp.int32, (Q_CHUNK, LANES), 1)
    first_head = lane < HEAD_DIM

    def pick_head(x):
        return jnp.where(first_head, x[:Q_CHUNK], x[Q_CHUNK:])

    tiles = []
    for pi, d in enumerate(DILATIONS):
        n_chunks = rows // (d * Q_CHUNK)
        for r in range(d):
            for c in range(n_chunks):
                q_lo = r + c * (d * Q_CHUNK)
                tok = pl.ds(q_lo, Q_CHUNK) if d == 1 else pl.ds(q_lo, Q_CHUNK, stride=d)
                tiles.append((pi, r, c, n_chunks, tok))

    def key_rows(trio, r, c, n_chunks):
        prev_ref, main_ref, next_ref = trio
        lanes = pl.ds(r * LANES, LANES)
        lo, hi = c * Q_CHUNK - BAND_RADIUS, (c + 1) * Q_CHUNK + BAND_RADIUS
        parts = []
        if lo < 0:
            parts.append(prev_ref[:, lanes])
            lo = 0
        main_hi = min(hi, n_chunks * Q_CHUNK)
        parts.append(main_ref[pl.ds(lo, main_hi - lo), lanes])
        if hi > main_hi:
            parts.append(next_ref[:, lanes])
        return parts[0] if len(parts) == 1 else jnp.concatenate(parts, axis=0)

    def bias_variant(c, n_chunks):
        at_start, at_end = c == 0, c == n_chunks - 1
        if n_blocks == 1:
            return int(at_start) + 2 * int(at_end)
        first = (t == 0).astype(jnp.int32) if at_start else 0
        last = (t == n_blocks - 1).astype(jnp.int32) if at_end else 0
        return first + 2 * last

    dyn_zero = jnp.minimum(t, 0)

    for n, (pi, r, c, n_chunks, tok) in enumerate(tiles):
        qc = q_refs[pi][pl.ds(c * Q_CHUNK, Q_CHUNK), pl.ds(r * LANES, LANES)]
        zero = jnp.zeros_like(qc)
        q2 = jnp.concatenate([jnp.where(first_head, qc, zero), jnp.where(first_head, zero, qc)], axis=0)
        kc = key_rows(k_refs[pi], r, c, n_chunks)
        s = lax.dot_general(q2, kc, (((1,), (1,)), ((), ())), preferred_element_type=f32)
        s = s + bias_ref[bias_variant(c, n_chunks)]
        tbuf[n + dyn_zero] = s
        m = jnp.broadcast_to(jnp.max(s, axis=-1, keepdims=True), (2 * Q_CHUNK, LANES))
        for h in range(2):
            mpat[pi, h, tok, :] = m[h * Q_CHUNK:(h + 1) * Q_CHUNK]

    pass_rows = 64
    for h in range(2):
        for i in range(rows // pass_rows):
            sl = pl.ds(i * pass_rows, pass_rows)
            mall[h, sl, :] = jnp.maximum(jnp.maximum(mpat[0, h, sl, :], mpat[1, h, sl, :]), mpat[2, h, sl, :])

    ones = jnp.ones((K_CHUNK, LANES), bf16)
    for n, (pi, r, c, n_chunks, tok) in enumerate(tiles):
        mb = jnp.concatenate([mall[0, tok, :], mall[1, tok, :]], axis=0)
        p = jnp.exp2(tbuf[n + dyn_zero] - jnp.concatenate([mb, mb], axis=1)).astype(bf16)
        vc = jnp.concatenate([key_rows(v_refs[pi], r, c, n_chunks), ones], axis=1)
        ol = jnp.dot(p, vc, preferred_element_type=f32)
        opat[pi, tok, :] = pick_head(ol[:, :LANES])
        lpat[pi, tok, :] = pick_head(ol[:, LANES:])

    for i in range(rows // pass_rows):
        sl = pl.ds(i * pass_rows, pass_rows)
        num = opat[0, sl, :] + opat[1, sl, :] + opat[2, sl, :]
        den = lpat[0, sl, :] + lpat[1, sl, :] + lpat[2, sl, :]
        o_ref[sl, :] = num / den


def _attn_call(q, k, v, batch, seq):
    rows = ATTN_ROWS
    n_blocks = seq // rows
    radius = BAND_RADIUS

    def view(a, d):
        return a.reshape(N_LANE_GROUPS, batch, seq // d, d * LANES)

    operands, in_specs = [], []
    for d in DILATIONS:
        operands.append(view(q, d))
        in_specs.append(pl.BlockSpec((None, None, rows // d, d * LANES), lambda b, g, t: (g, b, t, 0)))
    for a in (k, v):
        for d in DILATIONS:
            halo_per_block = rows // d // radius
            n_halo = seq // d // radius
            operands += [view(a, d)] * 3
            in_specs += [
                pl.BlockSpec((None, None, radius, d * LANES),
                             lambda b, g, t, hb=halo_per_block: (g, b, jnp.maximum(t * hb - 1, 0), 0)),
                pl.BlockSpec((None, None, rows // d, d * LANES), lambda b, g, t: (g, b, t, 0)),
                pl.BlockSpec((None, None, radius, d * LANES),
                             lambda b, g, t, hb=halo_per_block, nh=n_halo:
                             (g, b, jnp.minimum((t + 1) * hb, nh - 1), 0)),
            ]
    n_tiles = len(DILATIONS) * rows // Q_CHUNK
    return pl.pallas_call(
        functools.partial(_attn_kernel, n_blocks=n_blocks),
        grid=(batch, N_LANE_GROUPS, n_blocks),
        in_specs=in_specs,
        out_specs=pl.BlockSpec((None, None, rows, LANES), lambda b, g, t: (g, b, t, 0)),
        out_shape=jax.ShapeDtypeStruct((N_LANE_GROUPS, batch, seq, LANES), jnp.float32),
        scratch_shapes=[
            pltpu.VMEM((n_tiles, 2 * Q_CHUNK, K_CHUNK), jnp.float32),
            pltpu.VMEM((len(DILATIONS), 2, rows, LANES), jnp.float32),
            pltpu.VMEM((2, rows, LANES), jnp.float32),
            pltpu.VMEM((len(DILATIONS), rows, LANES), jnp.float32),
            pltpu.VMEM((len(DILATIONS), rows, LANES), jnp.float32),
            pltpu.VMEM((4, 2 * Q_CHUNK, K_CHUNK), jnp.float32),
        ],
        compiler_params=pltpu.CompilerParams(
            dimension_semantics=("arbitrary", "arbitrary", "arbitrary"), vmem_limit_bytes=VMEM_LIMIT),
        name="attn",
    )(*operands)


def _mix_kernel(x_ref, p_ref, pprev_ref, pnext_ref, a_ref, pw_ref, ps_ref, gp_ref, ga_ref,
                wo_ref, gpost_ref, o_ref, pbuf, mixed, *, seq):
    rows = MIX_ROWS
    f32 = jnp.float32
    i = pl.program_id(0)
    pos0 = (i * rows) % seq
    zero_rows = jnp.zeros((HALO_ROWS, POOL_WIDTH), f32)
    pbuf[pl.ds(0, HALO_ROWS), :] = jnp.where(pos0 > 0, pprev_ref[...], zero_rows)
    pbuf[pl.ds(HALO_ROWS, rows), :] = p_ref[...]
    pbuf[pl.ds(HALO_ROWS + rows, HALO_ROWS), :] = jnp.where(pos0 + rows < seq, pnext_ref[...], zero_rows)

    pos = pos0 + lax.broadcasted_iota(jnp.int32, (rows, LANES), 0)
    lane = lax.broadcasted_iota(jnp.int32, (rows, LANES), 1)
    means = []
    for half in range(2):
        w_small, w_big = POOL_WINDOWS[2 * half], POOL_WINDOWS[2 * half + 1]
        lanes = pl.ds(half * LANES, LANES)

        def wsum(lo, hi):
            acc = pbuf[pl.ds(HALO_ROWS + lo, rows), lanes]
            for off in range(lo + 1, hi):
                acc = acc + pbuf[pl.ds(HALO_ROWS + off, rows), lanes]
            return acc

        small = wsum(-(w_small // 2), w_small // 2)
        big = small + wsum(-(w_big // 2), -(w_small // 2)) + wsum(w_small // 2, w_big // 2)

        def count(w):
            return (jnp.minimum(pos + w // 2, seq) - jnp.maximum(pos - w // 2, 0)).astype(f32)

        first = lane < POOL_GROUP_DIM
        total = jnp.where(first, small, big)
        cnt = jnp.where(first, count(w_small), count(w_big))
        means.append(total / cnt - pbuf[pl.ds(HALO_ROWS, rows), lanes])
    pooled = jnp.concatenate(means, axis=-1).astype(jnp.bfloat16)
    pool_out = jnp.dot(pooled, pw_ref[...], preferred_element_type=f32) * ps_ref[...]
    mixed[:, pl.ds(0, POOL_WIDTH)] = _rms(pool_out, gp_ref[...]).astype(jnp.bfloat16)

    ssq = jnp.zeros((rows, 1), f32)
    for g in range(N_LANE_GROUPS):
        a = a_ref[g]
        ssq = ssq + jnp.sum(a * a, axis=-1, keepdims=True)
    inv = lax.rsqrt(ssq / ATTN_WIDTH + EPS)
    for g in range(N_LANE_GROUPS):
        ga = ga_ref[:, pl.ds(g * LANES, LANES)]
        mixed[:, pl.ds(POOL_WIDTH + g * LANES, LANES)] = (a_ref[g] * inv * ga).astype(jnp.bfloat16)

    y = jnp.dot(mixed[...], wo_ref[...], preferred_element_type=f32)
    o_ref[...] = x_ref[...] + _rms(y, gpost_ref[...])


def _mix_call(x, p_in, attn, pool_bd, pool_scale, g_pool, g_attn, w_out_bf16, g_post, seq):
    n = x.shape[0]
    rows = MIX_ROWS
    hb = rows // HALO_ROWS
    n_halo_blocks = n // HALO_ROWS
    const = lambda i: (0, 0)
    return pl.pallas_call(
        functools.partial(_mix_kernel, seq=seq),
        grid=(n // rows,),
        in_specs=[
            pl.BlockSpec((rows, D_MODEL), lambda i: (i, 0)),
            pl.BlockSpec((rows, POOL_WIDTH), lambda i: (i, 0)),
            pl.BlockSpec((HALO_ROWS, POOL_WIDTH), lambda i: (jnp.maximum(i * hb - 1, 0), 0)),
            pl.BlockSpec((HALO_ROWS, POOL_WIDTH),
                         lambda i: (jnp.minimum((i + 1) * hb, n_halo_blocks - 1), 0)),
            pl.BlockSpec((N_LANE_GROUPS, rows, LANES), lambda i: (0, i, 0)),
            pl.BlockSpec((POOL_WIDTH, POOL_WIDTH), const),
            pl.BlockSpec((1, POOL_WIDTH), const),
            pl.BlockSpec((1, POOL_WIDTH), const),
            pl.BlockSpec((1, ATTN_WIDTH), const),
            pl.BlockSpec((D_MODEL, D_MODEL), const),
            pl.BlockSpec((1, D_MODEL), const),
        ],
        out_specs=pl.BlockSpec((rows, D_MODEL), lambda i: (i, 0)),
        out_shape=jax.ShapeDtypeStruct((n, D_MODEL), jnp.float32),
        scratch_shapes=[
            pltpu.VMEM((rows + 2 * HALO_ROWS, POOL_WIDTH), jnp.float32),
            pltpu.VMEM((rows, D_MODEL), jnp.bfloat16),
        ],
        compiler_params=pltpu.CompilerParams(
            dimension_semantics=("arbitrary",), vmem_limit_bytes=VMEM_LIMIT),
        name="mix",
    )(x, p_in, p_in, p_in, attn, pool_bd, pool_scale, g_pool, g_attn, w_out_bf16, g_post)


def _ffn_kernel(x_ref, xprev_ref, xnext_ref, gpre_ref, wup_ref, cw_ref, cb_ref, wdn_ref, gpost_ref,
                o_ref, hbuf, *bufs, seq):
    n_chunks = D_FF // FFN_CHUNK
    ubufs, abufs = bufs[:n_chunks], bufs[n_chunks:]
    rows = FFN_ROWS
    f32 = jnp.float32
    i = pl.program_id(0)
    pos0 = (i * rows) % seq
    g = gpre_ref[...]
    hprev = jnp.where(pos0 > 0, _rms(xprev_ref[...], g), 0.0)
    hnext = jnp.where(pos0 + rows < seq, _rms(xnext_ref[...], g), 0.0)
    hbuf[pl.ds(0, HALO_ROWS), :] = hprev
    hbuf[pl.ds(HALO_ROWS, rows), :] = _rms(x_ref[...], g)
    hbuf[pl.ds(HALO_ROWS + rows, HALO_ROWS), :] = hnext
    h = hbuf[...].astype(jnp.bfloat16)

    def conv(c, half):
        col = half * D_FF + c * FFN_CHUNK
        w = cw_ref[:, pl.ds(col, FFN_CHUNK)]
        acc = cb_ref[:, pl.ds(col, FFN_CHUNK)]
        for j in range(3):
            u = ubufs[c][pl.ds(HALO_ROWS - 1 + j, rows), pl.ds(half * FFN_CHUNK, FFN_CHUNK)]
            acc = acc + u * w[j:j + 1]
        return acc

    def up(c):
        for half in range(2):
            col = half * D_FF + c * FFN_CHUNK
            ubufs[c][:, pl.ds(half * FFN_CHUNK, FFN_CHUNK)] = jnp.dot(
                h, wup_ref[:, pl.ds(col, FFN_CHUNK)], preferred_element_type=f32)

    def down(g, n_in_group):
        width = n_in_group * FFN_CHUNK
        w = wdn_ref[pl.ds(g * FFN_DOWN_GROUP * FFN_CHUNK, width), :]
        return jnp.dot(abufs[g][:, pl.ds(0, width)], w, preferred_element_type=f32)

    for c in range(min(FFN_LOOKAHEAD, n_chunks)):
        up(c)
    y = None
    for c in range(n_chunks):
        if c + FFN_LOOKAHEAD < n_chunks:
            up(c + FFN_LOOKAHEAD)
        gate = conv(c, 0)
        val = conv(c, 1)
        gelu = 0.5 * gate * (1.0 + lax.erf(gate * (2.0 ** -0.5)))
        g, j = divmod(c, FFN_DOWN_GROUP)
        abufs[g][:, pl.ds(j * FFN_CHUNK, FFN_CHUNK)] = (gelu * val).astype(jnp.bfloat16)
        if j + 1 == FFN_DOWN_GROUP or c + 1 == n_chunks:
            part = down(g, j + 1)
            y = part if y is None else y + part
    o_ref[...] = x_ref[...] + _rms(y, gpost_ref[...])


def _ffn_call(x1, g_pre, w_up_bf16, conv_w, conv_b, w_down_bf16, g_post, seq):
    n = x1.shape[0]
    rows = FFN_ROWS
    hb = rows // HALO_ROWS
    n_halo_blocks = n // HALO_ROWS
    const = lambda i: (0, 0)
    return pl.pallas_call(
        functools.partial(_ffn_kernel, seq=seq),
        grid=(n // rows,),
        in_specs=[
            pl.BlockSpec((rows, D_MODEL), lambda i: (i, 0)),
            pl.BlockSpec((HALO_ROWS, D_MODEL), lambda i: (jnp.maximum(i * hb - 1, 0), 0)),
            pl.BlockSpec((HALO_ROWS, D_MODEL),
                         lambda i: (jnp.minimum((i + 1) * hb, n_halo_blocks - 1), 0)),
            pl.BlockSpec((1, D_MODEL), const),
            pl.BlockSpec((D_MODEL, 2 * D_FF), const, pipeline_mode=pl.Buffered(1)),
            pl.BlockSpec((3, 2 * D_FF), const),
            pl.BlockSpec((1, 2 * D_FF), const),
            pl.BlockSpec((D_FF, D_MODEL), const, pipeline_mode=pl.Buffered(1)),
            pl.BlockSpec((1, D_MODEL), const),
        ],
        out_specs=pl.BlockSpec((rows, D_MODEL), lambda i: (i, 0)),
        out_shape=jax.ShapeDtypeStruct((n, D_MODEL), jnp.float32),
        scratch_shapes=[
            pltpu.VMEM((rows + 2 * HALO_ROWS, D_MODEL), jnp.float32),
            *[pltpu.VMEM((rows + 2 * HALO_ROWS, 2 * FFN_CHUNK), jnp.float32)] * (D_FF // FFN_CHUNK),
            *[pltpu.VMEM((rows, FFN_DOWN_GROUP * FFN_CHUNK), jnp.bfloat16)]
            * pl.cdiv(D_FF // FFN_CHUNK, FFN_DOWN_GROUP),
        ],
        compiler_params=pltpu.CompilerParams(
            dimension_semantics=("arbitrary",), vmem_limit_bytes=VMEM_LIMIT),
        name="ffn",
    )(x1, x1, x1, g_pre, w_up_bf16, conv_w, conv_b, w_down_bf16, g_post)


def _block_diag(pool_w):
    groups = pool_w.shape[0]
    bd = jnp.zeros((POOL_WIDTH, POOL_WIDTH), pool_w.dtype)
    for g in range(groups):
        lo = g * POOL_GROUP_DIM
        bd = bd.at[lo:lo + POOL_GROUP_DIM, lo:lo + POOL_GROUP_DIM].set(pool_w[g])
    return bd


def _layer(x, params):
    (g_mix_pre, g_mix_post, w_in, pool_bd, pool_scale, g_pool_out, g_attn_out, w_out,
     g_ffn_pre, g_ffn_post, w_up, conv_w, conv_b, w_down) = params
    batch, seq, _ = x.shape
    assert seq % ATTN_ROWS == 0 and seq % PROJ_ROWS == 0 and seq % MIX_ROWS == 0 and seq % FFN_ROWS == 0
    xf = x.reshape(batch * seq, D_MODEL)
    p_in, q, k, v = _proj_call(xf, g_mix_pre, w_in, _rope_tables(seq), seq)
    attn = _attn_call(q, k, v, batch, seq).reshape(N_LANE_GROUPS, batch * seq, LANES)
    x1 = _mix_call(xf, p_in, attn, pool_bd, pool_scale, g_pool_out, g_attn_out, w_out, g_mix_post, seq)
    y = _ffn_call(x1, g_ffn_pre, w_up, conv_w, conv_b, w_down, g_ffn_post, seq)
    return y.reshape(batch, seq, D_MODEL)


def kernel(x_prompt, x_sample, g_mix_pre, g_mix_post, w_in, pool_w, pool_scale, g_pool_out, g_attn_out,
           w_out, g_ffn_pre, g_ffn_post, w_up, conv_w, conv_b, w_down):
    depth = w_in.shape[0]
    bf16 = jnp.bfloat16

    def layer_params(l):
        return (g_mix_pre[l][None], g_mix_post[l][None], w_in[l].astype(bf16),
                _block_diag(pool_w[l]).astype(bf16), pool_scale[l][None], g_pool_out[l][None],
                g_attn_out[l][None], w_out[l].astype(bf16), g_ffn_pre[l][None], g_ffn_post[l][None],
                w_up[l].astype(bf16), conv_w[l], conv_b[l][None], w_down[l].astype(bf16))

    params = [layer_params(l) for l in range(depth)]

    def run(x):
        for p in params:
            x = _layer(x, p)
        return x

    return (run(x_prompt), run(x_sample))
```

```python
import functools
import math

import jax
import jax.numpy as jnp
from jax import lax
from jax.experimental import pallas as pl
from jax.experimental.pallas import tpu as pltpu

D_MODEL = 1024
POOL_WIDTH = 256
POOL_WINDOWS = (2, 4, 8, 16)
POOL_GROUP_DIM = 64
ATTN_WIDTH = 768
HEAD_DIM = 64
DILATIONS = (1, 4, 16)
BAND_RADIUS = 64
ROPE_THETA = 500000.0
ROPE_DIM = 16
D_FF = 2816
EPS = 1e-6
IN_WIDTH = POOL_WIDTH + 3 * ATTN_WIDTH

LANES = 128
N_LANE_GROUPS = ATTN_WIDTH // LANES
NEG_BIG = -1e30

PROJ_ROWS = 512
ATTN_ROWS = 2048
Q_CHUNK = 128
K_CHUNK = Q_CHUNK + 2 * BAND_RADIUS
MIX_ROWS = 512
FFN_ROWS = 512
FFN_CHUNK = 256
FFN_LOOKAHEAD = 2
FFN_DOWN_GROUP = 4
HALO_ROWS = 8

VMEM_LIMIT = 56 * 1024 * 1024


def _rms(x, g):
    ms = jnp.mean(x * x, axis=-1, keepdims=True)
    return x * lax.rsqrt(ms + EPS) * g


def _proj_kernel(x_ref, g_ref, w_ref, tab_ref, p_ref, *refs):
    n_out = 3 * len(DILATIONS)
    outs = [refs[len(DILATIONS) * i:len(DILATIONS) * (i + 1)] for i in range(3)]
    stage = refs[n_out:]
    rows = PROJ_ROWS
    bf16 = jnp.bfloat16
    h = _rms(x_ref[...], g_ref[...]).astype(bf16)
    proj = jnp.dot(h, w_ref[...], preferred_element_type=jnp.float32)
    p_ref[...] = proj[:, :POOL_WIDTH]
    qc, qs1, qs2 = tab_ref[0], tab_ref[1], tab_ref[2]
    kc, ks1, ks2 = tab_ref[3], tab_ref[4], tab_ref[5]

    def rope(t, c, s1, s2):
        return t * c + pltpu.roll(t, LANES - 8, 1) * s1 + pltpu.roll(t, 8, 1) * s2

    def emit(val, out_refs, g, slab_a, slab_b):
        o1, o4, o16 = out_refs
        o1[g] = val.astype(bf16)
        slab_a[...] = val
        for r_lo in range(4):
            plane = slab_a[pl.ds(r_lo, rows // 4, stride=4), :]
            o4[g, :, pl.ds(r_lo * LANES, LANES)] = plane.astype(bf16)
            slab_b[pl.ds(r_lo * (rows // 4), rows // 4), :] = plane
        for r_lo in range(4):
            for r_hi in range(4):
                piece = slab_b[pl.ds(r_lo * (rows // 4) + r_hi, rows // 16, stride=4), :]
                o16[g, :, pl.ds((4 * r_hi + r_lo) * LANES, LANES)] = piece.astype(bf16)

    for g in range(N_LANE_GROUPS):
        lo = POOL_WIDTH + g * LANES
        emit(rope(proj[:, lo:lo + LANES], qc, qs1, qs2), outs[0], g, stage[6 * g], stage[6 * g + 1])
        lo += ATTN_WIDTH
        emit(rope(proj[:, lo:lo + LANES], kc, ks1, ks2), outs[1], g, stage[6 * g + 2], stage[6 * g + 3])
        lo += ATTN_WIDTH
        emit(proj[:, lo:lo + LANES], outs[2], g, stage[6 * g + 4], stage[6 * g + 5])


def _rope_tables(seq):
    pos = jnp.arange(seq, dtype=jnp.float32)
    inv_freq = ROPE_THETA ** (-jnp.arange(0, ROPE_DIM, 2, dtype=jnp.float32) / ROPE_DIM)
    ang = pos[:, None] * inv_freq[None, :]
    cos, sin = jnp.cos(ang), jnp.sin(ang)
    half = ROPE_DIM // 2
    ones = jnp.ones((seq, HEAD_DIM - ROPE_DIM), jnp.float32)
    zeros = jnp.zeros((seq, HEAD_DIM - ROPE_DIM), jnp.float32)
    zh = jnp.zeros((seq, half), jnp.float32)
    c = jnp.concatenate([cos, cos, ones], axis=-1)
    s1 = jnp.concatenate([-sin, zh, zeros], axis=-1)
    s2 = jnp.concatenate([zh, sin, zeros], axis=-1)
    tabs = jnp.stack([jnp.tile(t, (1, LANES // HEAD_DIM)) for t in (c, s1, s2)])
    scale = HEAD_DIM ** -0.5 * math.log2(math.e)
    return jnp.concatenate([tabs * scale, tabs], axis=0)


def _proj_call(x, g, w_bf16, tabs, seq):
    n = x.shape[0]
    t = PROJ_ROWS
    blocks_per_seq = seq // t
    qkv_shapes = [jax.ShapeDtypeStruct((N_LANE_GROUPS, n // d, d * LANES), jnp.bfloat16)
                  for d in DILATIONS] * 3
    qkv_specs = [pl.BlockSpec((N_LANE_GROUPS, t // d, d * LANES), lambda i: (0, i, 0))
                 for d in DILATIONS] * 3
    outs = pl.pallas_call(
        _proj_kernel,
        grid=(n // t,),
        in_specs=[
            pl.BlockSpec((t, D_MODEL), lambda i: (i, 0)),
            pl.BlockSpec((1, D_MODEL), lambda i: (0, 0)),
            pl.BlockSpec((D_MODEL, IN_WIDTH), lambda i: (0, 0)),
            pl.BlockSpec((6, t, LANES), lambda i: (0, i % blocks_per_seq, 0)),
        ],
        out_specs=[pl.BlockSpec((t, POOL_WIDTH), lambda i: (i, 0)), *qkv_specs],
        out_shape=[jax.ShapeDtypeStruct((n, POOL_WIDTH), jnp.float32), *qkv_shapes],
        scratch_shapes=[pltpu.VMEM((t, LANES), jnp.float32)] * (2 * 3 * N_LANE_GROUPS),
        compiler_params=pltpu.CompilerParams(
            dimension_semantics=("arbitrary",), vmem_limit_bytes=VMEM_LIMIT),
        name="proj",
    )(x, g, w_bf16, tabs)
    n_pat = len(DILATIONS)
    return outs[0], outs[1:1 + n_pat], outs[1 + n_pat:1 + 2 * n_pat], outs[1 + 2 * n_pat:]


def _attn_kernel(*refs, n_blocks):
    n_pat = len(DILATIONS)
    q_refs = refs[:n_pat]
    k_refs = [refs[n_pat + 3 * i:n_pat + 3 * i + 3] for i in range(n_pat)]
    v_refs = [refs[4 * n_pat + 3 * i:4 * n_pat + 3 * i + 3] for i in range(n_pat)]
    o_ref, tbuf, mpat, mall, opat, lpat, bias_ref = refs[7 * n_pat:]
    t = pl.program_id(2)
    rows = ATTN_ROWS
    f32 = jnp.float32
    bf16 = jnp.bfloat16

    qi = lax.broadcasted_iota(jnp.int32, (2 * Q_CHUNK, K_CHUNK), 0) % Q_CHUNK
    kj = lax.broadcasted_iota(jnp.int32, (2 * Q_CHUNK, K_CHUNK), 1)
    band = (kj >= qi) & (kj <= qi + 2 * BAND_RADIUS)
    for variant in range(4):
        ok = band
        if variant & 1:
            ok = ok & (kj >= BAND_RADIUS)
        if variant & 2:
            ok = ok & (kj < BAND_RADIUS + Q_CHUNK)
        bias_ref[variant] = jnp.where(ok, 0.0, NEG_BIG).astype(f32)

    lane = lax.broadcasted_iota(jnp.int32, (Q_CHUNK, LANES), 1)
    first_head = lane < HEAD_DIM

    def pick_head(x):
        return jnp.where(first_head, x[:Q_CHUNK], x[Q_CHUNK:])

    tiles = []
    for pi, d in enumerate(DILATIONS):
        n_chunks = rows // (d * Q_CHUNK)
        for r in range(d):
            for c in range(n_chunks):
                q_lo = r + c * (d * Q_CHUNK)
                tok = pl.ds(q_lo, Q_CHUNK) if d == 1 else pl.ds(q_lo, Q_CHUNK, stride=d)
                tiles.append((pi, r, c, n_chunks, tok))

    def key_rows(trio, r, c, n_chunks):
        prev_ref, main_ref, next_ref = trio
        lanes = pl.ds(r * LANES, LANES)
        lo, hi = c * Q_CHUNK - BAND_RADIUS, (c + 1) * Q_CHUNK + BAND_RADIUS
        parts = []
        if lo < 0:
            parts.append(prev_ref[:, lanes])
            lo = 0
        main_hi = min(hi, n_chunks * Q_CHUNK)
        parts.append(main_ref[pl.ds(lo, main_hi - lo), lanes])
        if hi > main_hi:
            parts.append(next_ref[:, lanes])
        return parts[0] if len(parts) == 1 else jnp.concatenate(parts, axis=0)

    def bias_variant(c, n_chunks):
        at_start, at_end = c == 0, c == n_chunks - 1
        if n_blocks == 1:
            return int(at_start) + 2 * int(at_end)
        first = (t == 0).astype(jnp.int32) if at_start else 0
        last = (t == n_blocks - 1).astype(jnp.int32) if at_end else 0
        return first + 2 * last

    dyn_zero = jnp.minimum(t, 0)

    for n, (pi, r, c, n_chunks, tok) in enumerate(tiles):
        qc = q_refs[pi][pl.ds(c * Q_CHUNK, Q_CHUNK), pl.ds(r * LANES, LANES)]
        zero = jnp.zeros_like(qc)
        q2 = jnp.concatenate([jnp.where(first_head, qc, zero), jnp.where(first_head, zero, qc)], axis=0)
        kc = key_rows(k_refs[pi], r, c, n_chunks)
        s = lax.dot_general(q2, kc, (((1,), (1,)), ((), ())), preferred_element_type=f32)
        s = s + bias_ref[bias_variant(c, n_chunks)]
        tbuf[n + dyn_zero] = s
        m = jnp.broadcast_to(jnp.max(s, axis=-1, keepdims=True), (2 * Q_CHUNK, LANES))
        for h in range(2):
            mpat[pi, h, tok, :] = m[h * Q_CHUNK:(h + 1) * Q_CHUNK]

    pass_rows = 64
    for h in range(2):
        for i in range(rows // pass_rows):
            sl = pl.ds(i * pass_rows, pass_rows)
            mall[h, sl, :] = jnp.maximum(jnp.maximum(mpat[0, h, sl, :], mpat[1, h, sl, :]), mpat[2, h, sl, :])

    ones = jnp.ones((K_CHUNK, LANES), bf16)
    for n, (pi, r, c, n_chunks, tok) in enumerate(tiles):
        mb = jnp.concatenate([mall[0, tok, :], mall[1, tok, :]], axis=0)
        p = jnp.exp2(tbuf[n + dyn_zero] - jnp.concatenate([mb, mb], axis=1)).astype(bf16)
        vc = jnp.concatenate([key_rows(v_refs[pi], r, c, n_chunks), ones], axis=1)
        ol = jnp.dot(p, vc, preferred_element_type=f32)
        opat[pi, tok, :] = pick_head(ol[:, :LANES])
        lpat[pi, tok, :] = pick_head(ol[:, LANES:])

    for i in range(rows // pass_rows):
        sl = pl.ds(i * pass_rows, pass_rows)
        num = opat[0, sl, :] + opat[1, sl, :] + opat[2, sl, :]
        den = lpat[0, sl, :] + lpat[1, sl, :] + lpat[2, sl, :]
        o_ref[sl, :] = num / den


def _attn_call(q, k, v, batch, seq):
    rows = ATTN_ROWS
    n_blocks = seq // rows
    radius = BAND_RADIUS

    def view(a, d):
        return a.reshape(N_LANE_GROUPS, batch, seq // d, d * LANES)

    operands, in_specs = [], []
    for d, a in zip(DILATIONS, q):
        operands.append(view(a, d))
        in_specs.append(pl.BlockSpec((None, None, rows // d, d * LANES), lambda b, g, t: (g, b, t, 0)))
    for arrs in (k, v):
        for d, a in zip(DILATIONS, arrs):
            halo_per_block = rows // d // radius
            n_halo = seq // d // radius
            operands += [view(a, d)] * 3
            in_specs += [
                pl.BlockSpec((None, None, radius, d * LANES),
                             lambda b, g, t, hb=halo_per_block: (g, b, jnp.maximum(t * hb - 1, 0), 0)),
                pl.BlockSpec((None, None, rows // d, d * LANES), lambda b, g, t: (g, b, t, 0)),
                pl.BlockSpec((None, None, radius, d * LANES),
                             lambda b, g, t, hb=halo_per_block, nh=n_halo:
                             (g, b, jnp.minimum((t + 1) * hb, nh - 1), 0)),
            ]
    n_tiles = len(DILATIONS) * rows // Q_CHUNK
    return pl.pallas_call(
        functools.partial(_attn_kernel, n_blocks=n_blocks),
        grid=(batch, N_LANE_GROUPS, n_blocks),
        in_specs=in_specs,
        out_specs=pl.BlockSpec((None, None, rows, LANES), lambda b, g, t: (g, b, t, 0)),
        out_shape=jax.ShapeDtypeStruct((N_LANE_GROUPS, batch, seq, LANES), jnp.float32),
        scratch_shapes=[
            pltpu.VMEM((n_tiles, 2 * Q_CHUNK, K_CHUNK), jnp.float32),
            pltpu.VMEM((len(DILATIONS), 2, rows, LANES), jnp.float32),
            pltpu.VMEM((2, rows, LANES), jnp.float32),
            pltpu.VMEM((len(DILATIONS), rows, LANES), jnp.float32),
            pltpu.VMEM((len(DILATIONS), rows, LANES), jnp.float32),
            pltpu.VMEM((4, 2 * Q_CHUNK, K_CHUNK), jnp.float32),
        ],
        compiler_params=pltpu.CompilerParams(
            dimension_semantics=("arbitrary", "arbitrary", "arbitrary"), vmem_limit_bytes=VMEM_LIMIT),
        name="attn",
    )(*operands)


def _mix_kernel(x_ref, p_ref, pprev_ref, pnext_ref, a_ref, pw_ref, ps_ref, gp_ref, ga_ref,
                wo_ref, gpost_ref, o_ref, pbuf, mixed, *, seq):
    rows = MIX_ROWS
    f32 = jnp.float32
    i = pl.program_id(0)
    pos0 = (i * rows) % seq
    zero_rows = jnp.zeros((HALO_ROWS, POOL_WIDTH), f32)
    pbuf[pl.ds(0, HALO_ROWS), :] = jnp.where(pos0 > 0, pprev_ref[...], zero_rows)
    pbuf[pl.ds(HALO_ROWS, rows), :] = p_ref[...]
    pbuf[pl.ds(HALO_ROWS + rows, HALO_ROWS), :] = jnp.where(pos0 + rows < seq, pnext_ref[...], zero_rows)

    pos = pos0 + lax.broadcasted_iota(jnp.int32, (rows, LANES), 0)
    lane = lax.broadcasted_iota(jnp.int32, (rows, LANES), 1)
    means = []
    for half in range(2):
        w_small, w_big = POOL_WINDOWS[2 * half], POOL_WINDOWS[2 * half + 1]
        lanes = pl.ds(half * LANES, LANES)

        def wsum(lo, hi):
            acc = pbuf[pl.ds(HALO_ROWS + lo, rows), lanes]
            for off in range(lo + 1, hi):
                acc = acc + pbuf[pl.ds(HALO_ROWS + off, rows), lanes]
            return acc

        small = wsum(-(w_small // 2), w_small // 2)
        big = small + wsum(-(w_big // 2), -(w_small // 2)) + wsum(w_small // 2, w_big // 2)

        def count(w):
            return (jnp.minimum(pos + w // 2, seq) - jnp.maximum(pos - w // 2, 0)).astype(f32)

        first = lane < POOL_GROUP_DIM
        total = jnp.where(first, small, big)
        cnt = jnp.where(first, count(w_small), count(w_big))
        means.append(total / cnt - pbuf[pl.ds(HALO_ROWS, rows), lanes])
    pooled = jnp.concatenate(means, axis=-1).astype(jnp.bfloat16)
    pool_out = jnp.dot(pooled, pw_ref[...], preferred_element_type=f32) * ps_ref[...]
    mixed[:, pl.ds(0, POOL_WIDTH)] = _rms(pool_out, gp_ref[...]).astype(jnp.bfloat16)

    ssq = jnp.zeros((rows, 1), f32)
    for g in range(N_LANE_GROUPS):
        a = a_ref[g]
        ssq = ssq + jnp.sum(a * a, axis=-1, keepdims=True)
    inv = lax.rsqrt(ssq / ATTN_WIDTH + EPS)
    for g in range(N_LANE_GROUPS):
        ga = ga_ref[:, pl.ds(g * LANES, LANES)]
        mixed[:, pl.ds(POOL_WIDTH + g * LANES, LANES)] = (a_ref[g] * inv * ga).astype(jnp.bfloat16)

    y = jnp.dot(mixed[...], wo_ref[...], preferred_element_type=f32)
    o_ref[...] = x_ref[...] + _rms(y, gpost_ref[...])


def _mix_call(x, p_in, attn, pool_bd, pool_scale, g_pool, g_attn, w_out_bf16, g_post, seq):
    n = x.shape[0]
    rows = MIX_ROWS
    hb = rows // HALO_ROWS
    n_halo_blocks = n // HALO_ROWS
    const = lambda i: (0, 0)
    return pl.pallas_call(
        functools.partial(_mix_kernel, seq=seq),
        grid=(n // rows,),
        in_specs=[
            pl.BlockSpec((rows, D_MODEL), lambda i: (i, 0)),
            pl.BlockSpec((rows, POOL_WIDTH), lambda i: (i, 0)),
            pl.BlockSpec((HALO_ROWS, POOL_WIDTH), lambda i: (jnp.maximum(i * hb - 1, 0), 0)),
            pl.BlockSpec((HALO_ROWS, POOL_WIDTH),
                         lambda i: (jnp.minimum((i + 1) * hb, n_halo_blocks - 1), 0)),
            pl.BlockSpec((N_LANE_GROUPS, rows, LANES), lambda i: (0, i, 0)),
            pl.BlockSpec((POOL_WIDTH, POOL_WIDTH), const),
            pl.BlockSpec((1, POOL_WIDTH), const),
            pl.BlockSpec((1, POOL_WIDTH), const),
            pl.BlockSpec((1, ATTN_WIDTH), const),
            pl.BlockSpec((D_MODEL, D_MODEL), const),
            pl.BlockSpec((1, D_MODEL), const),
        ],
        out_specs=pl.BlockSpec((rows, D_MODEL), lambda i: (i, 0)),
        out_shape=jax.ShapeDtypeStruct((n, D_MODEL), jnp.float32),
        scratch_shapes=[
            pltpu.VMEM((rows + 2 * HALO_ROWS, POOL_WIDTH), jnp.float32),
            pltpu.VMEM((rows, D_MODEL), jnp.bfloat16),
        ],
        compiler_params=pltpu.CompilerParams(
            dimension_semantics=("arbitrary",), vmem_limit_bytes=VMEM_LIMIT),
        name="mix",
    )(x, p_in, p_in, p_in, attn, pool_bd, pool_scale, g_pool, g_attn, w_out_bf16, g_post)


def _ffn_kernel(x_ref, xprev_ref, xnext_ref, gpre_ref, wup_ref, cw_ref, cb_ref, wdn_ref, gpost_ref,
                o_ref, hbuf, *bufs, seq):
    n_chunks = D_FF // FFN_CHUNK
    ubufs, abufs = bufs[:n_chunks], bufs[n_chunks:]
    rows = FFN_ROWS
    f32 = jnp.float32
    i = pl.program_id(0)
    pos0 = (i * rows) % seq
    g = gpre_ref[...]
    hprev = jnp.where(pos0 > 0, _rms(xprev_ref[...], g), 0.0)
    hnext = jnp.where(pos0 + rows < seq, _rms(xnext_ref[...], g), 0.0)
    hbuf[pl.ds(0, HALO_ROWS), :] = hprev
    hbuf[pl.ds(HALO_ROWS, rows), :] = _rms(x_ref[...], g)
    hbuf[pl.ds(HALO_ROWS + rows, HALO_ROWS), :] = hnext
    h = hbuf[...].astype(jnp.bfloat16)

    def conv(c, half):
        col = half * D_FF + c * FFN_CHUNK
        w = cw_ref[:, pl.ds(col, FFN_CHUNK)]
        acc = cb_ref[:, pl.ds(col, FFN_CHUNK)]
        for j in range(3):
            u = ubufs[c][pl.ds(HALO_ROWS - 1 + j, rows), pl.ds(half * FFN_CHUNK, FFN_CHUNK)]
            acc = acc + u * w[j:j + 1]
        return acc

    def up(c):
        for half in range(2):
            col = half * D_FF + c * FFN_CHUNK
            ubufs[c][:, pl.ds(half * FFN_CHUNK, FFN_CHUNK)] = jnp.dot(
                h, wup_ref[:, pl.ds(col, FFN_CHUNK)], preferred_element_type=f32)

    def down(g, n_in_group):
        width = n_in_group * FFN_CHUNK
        w = wdn_ref[pl.ds(g * FFN_DOWN_GROUP * FFN_CHUNK, width), :]
        return jnp.dot(abufs[g][:, pl.ds(0, width)], w, preferred_element_type=f32)

    for c in range(min(FFN_LOOKAHEAD, n_chunks)):
        up(c)
    y = None
    for c in range(n_chunks):
        if c + FFN_LOOKAHEAD < n_chunks:
            up(c + FFN_LOOKAHEAD)
        gate = conv(c, 0)
        val = conv(c, 1)
        gelu = 0.5 * gate * (1.0 + lax.erf(gate * (2.0 ** -0.5)))
        g, j = divmod(c, FFN_DOWN_GROUP)
        abufs[g][:, pl.ds(j * FFN_CHUNK, FFN_CHUNK)] = (gelu * val).astype(jnp.bfloat16)
        if j + 1 == FFN_DOWN_GROUP or c + 1 == n_chunks:
            part = down(g, j + 1)
            y = part if y is None else y + part
    o_ref[...] = x_ref[...] + _rms(y, gpost_ref[...])


def _ffn_call(x1, g_pre, w_up_bf16, conv_w, conv_b, w_down_bf16, g_post, seq):
    n = x1.shape[0]
    rows = FFN_ROWS
    hb = rows // HALO_ROWS
    n_halo_blocks = n // HALO_ROWS
    const = lambda i: (0, 0)
    return pl.pallas_call(
        functools.partial(_ffn_kernel, seq=seq),
        grid=(n // rows,),
        in_specs=[
            pl.BlockSpec((rows, D_MODEL), lambda i: (i, 0)),
            pl.BlockSpec((HALO_ROWS, D_MODEL), lambda i: (jnp.maximum(i * hb - 1, 0), 0)),
            pl.BlockSpec((HALO_ROWS, D_MODEL),
                         lambda i: (jnp.minimum((i + 1) * hb, n_halo_blocks - 1), 0)),
            pl.BlockSpec((1, D_MODEL), const),
            pl.BlockSpec((D_MODEL, 2 * D_FF), const, pipeline_mode=pl.Buffered(1)),
            pl.BlockSpec((3, 2 * D_FF), const),
            pl.BlockSpec((1, 2 * D_FF), const),
            pl.BlockSpec((D_FF, D_MODEL), const, pipeline_mode=pl.Buffered(1)),
            pl.BlockSpec((1, D_MODEL), const),
        ],
        out_specs=pl.BlockSpec((rows, D_MODEL), lambda i: (i, 0)),
        out_shape=jax.ShapeDtypeStruct((n, D_MODEL), jnp.float32),
        scratch_shapes=[
            pltpu.VMEM((rows + 2 * HALO_ROWS, D_MODEL), jnp.float32),
            *[pltpu.VMEM((rows + 2 * HALO_ROWS, 2 * FFN_CHUNK), jnp.float32)] * (D_FF // FFN_CHUNK),
            *[pltpu.VMEM((rows, FFN_DOWN_GROUP * FFN_CHUNK), jnp.bfloat16)]
            * pl.cdiv(D_FF // FFN_CHUNK, FFN_DOWN_GROUP),
        ],
        compiler_params=pltpu.CompilerParams(
            dimension_semantics=("arbitrary",), vmem_limit_bytes=VMEM_LIMIT),
        name="ffn",
    )(x1, x1, x1, g_pre, w_up_bf16, conv_w, conv_b, w_down_bf16, g_post)


def _block_diag(pool_w):
    groups = pool_w.shape[0]
    bd = jnp.zeros((POOL_WIDTH, POOL_WIDTH), pool_w.dtype)
    for g in range(groups):
        lo = g * POOL_GROUP_DIM
        bd = bd.at[lo:lo + POOL_GROUP_DIM, lo:lo + POOL_GROUP_DIM].set(pool_w[g])
    return bd


def _layer(x, params):
    (g_mix_pre, g_mix_post, w_in, pool_bd, pool_scale, g_pool_out, g_attn_out, w_out,
     g_ffn_pre, g_ffn_post, w_up, conv_w, conv_b, w_down) = params
    batch, seq, _ = x.shape
    assert seq % ATTN_ROWS == 0 and seq % PROJ_ROWS == 0 and seq % MIX_ROWS == 0 and seq % FFN_ROWS == 0
    xf = x.reshape(batch * seq, D_MODEL)
    p_in, q, k, v = _proj_call(xf, g_mix_pre, w_in, _rope_tables(seq), seq)
    attn = _attn_call(q, k, v, batch, seq).reshape(N_LANE_GROUPS, batch * seq, LANES)
    x1 = _mix_call(xf, p_in, attn, pool_bd, pool_scale, g_pool_out, g_attn_out, w_out, g_mix_post, seq)
    y = _ffn_call(x1, g_ffn_pre, w_up, conv_w, conv_b, w_down, g_ffn_post, seq)
    return y.reshape(batch, seq, D_MODEL)


def kernel(x_prompt, x_sample, g_mix_pre, g_mix_post, w_in, pool_w, pool_scale, g_pool_out, g_attn_out,
           w_out, g_ffn_pre, g_ffn_post, w_up, conv_w, conv_b, w_down):
    depth = w_in.shape[0]
    bf16 = jnp.bfloat16

    def layer_params(l):
        return (g_mix_pre[l][None], g_mix_post[l][None], w_in[l].astype(bf16),
                _block_diag(pool_w[l]).astype(bf16), pool_scale[l][None], g_pool_out[l][None],
                g_attn_out[l][None], w_out[l].astype(bf16), g_ffn_pre[l][None], g_ffn_post[l][None],
                w_up[l].astype(bf16), conv_w[l], conv_b[l][None], w_down[l].astype(bf16))

    params = [layer_params(l) for l in range(depth)]

    def run(x):
        for p in params:
            x = _layer(x, p)
        return x

    return (run(x_prompt), run(x_sample))
```

```python
import functools
import math

import jax
import jax.numpy as jnp
from jax import lax
from jax.experimental import pallas as pl
from jax.experimental.pallas import tpu as pltpu

D_MODEL = 1024
POOL_WIDTH = 256
POOL_WINDOWS = (2, 4, 8, 16)
POOL_GROUP_DIM = 64
ATTN_WIDTH = 768
HEAD_DIM = 64
DILATIONS = (1, 4, 16)
BAND_RADIUS = 64
ROPE_THETA = 500000.0
ROPE_DIM = 16
D_FF = 2816
EPS = 1e-6
IN_WIDTH = POOL_WIDTH + 3 * ATTN_WIDTH

LANES = 128
N_LANE_GROUPS = ATTN_WIDTH // LANES
NEG_BIG = -1e30

PROJ_ROWS = 512
ATTN_ROWS = 2048
Q_CHUNK = 128
K_CHUNK = Q_CHUNK + 2 * BAND_RADIUS
MIX_ROWS = 512
FFN_ROWS = 512
FFN_CHUNK = 256
FFN_LOOKAHEAD = 2
FFN_U_RING = FFN_LOOKAHEAD + 1
FFN_DOWN_GROUP = 4
HALO_ROWS = 8

VMEM_LIMIT = 56 * 1024 * 1024


def _rms(x, g):
    ms = jnp.mean(x * x, axis=-1, keepdims=True)
    return x * lax.rsqrt(ms + EPS) * g


def _proj_kernel(x_ref, g_ref, w_ref, tab_ref, p_ref, *refs):
    n_out = 3 * len(DILATIONS)
    outs = [refs[len(DILATIONS) * i:len(DILATIONS) * (i + 1)] for i in range(3)]
    stage = refs[n_out:]
    rows = PROJ_ROWS
    bf16 = jnp.bfloat16
    h = _rms(x_ref[...], g_ref[...]).astype(bf16)
    proj = jnp.dot(h, w_ref[...], preferred_element_type=jnp.float32)
    p_ref[...] = proj[:, :POOL_WIDTH]
    qc, qs1, qs2 = tab_ref[0], tab_ref[1], tab_ref[2]
    kc, ks1, ks2 = tab_ref[3], tab_ref[4], tab_ref[5]

    def rope(t, c, s1, s2):
        return t * c + pltpu.roll(t, LANES - 8, 1) * s1 + pltpu.roll(t, 8, 1) * s2

    def emit(val, out_refs, g, slab_a, slab_b):
        o1, o4, o16 = out_refs
        o1[g] = val.astype(bf16)
        slab_a[...] = val
        for r_lo in range(4):
            plane = slab_a[pl.ds(r_lo, rows // 4, stride=4), :]
            o4[g, :, pl.ds(r_lo * LANES, LANES)] = plane.astype(bf16)
            slab_b[pl.ds(r_lo * (rows // 4), rows // 4), :] = plane
        for r_lo in range(4):
            for r_hi in range(4):
                piece = slab_b[pl.ds(r_lo * (rows // 4) + r_hi, rows // 16, stride=4), :]
                o16[g, :, pl.ds((4 * r_hi + r_lo) * LANES, LANES)] = piece.astype(bf16)

    for g in range(N_LANE_GROUPS):
        lo = POOL_WIDTH + g * LANES
        emit(rope(proj[:, lo:lo + LANES], qc, qs1, qs2), outs[0], g, stage[6 * g], stage[6 * g + 1])
        lo += ATTN_WIDTH
        emit(rope(proj[:, lo:lo + LANES], kc, ks1, ks2), outs[1], g, stage[6 * g + 2], stage[6 * g + 3])
        lo += ATTN_WIDTH
        emit(proj[:, lo:lo + LANES], outs[2], g, stage[6 * g + 4], stage[6 * g + 5])


def _rope_tables(seq):
    pos = jnp.arange(seq, dtype=jnp.float32)
    inv_freq = ROPE_THETA ** (-jnp.arange(0, ROPE_DIM, 2, dtype=jnp.float32) / ROPE_DIM)
    half = ROPE_DIM // 2
    dim = jnp.arange(LANES) % HEAD_DIM
    ang = pos[:, None] * inv_freq[dim % half][None, :]
    cos, sin = jnp.cos(ang), jnp.sin(ang)
    first_half, rotated = (dim < half)[None, :], (dim < ROPE_DIM)[None, :]
    c = jnp.where(rotated, cos, 1.0)
    s1 = jnp.where(first_half, -sin, 0.0)
    s2 = jnp.where(rotated & ~first_half, sin, 0.0)
    tabs = jnp.stack([c, s1, s2])
    scale = HEAD_DIM ** -0.5 * math.log2(math.e)
    return jnp.concatenate([tabs * scale, tabs], axis=0)


def _proj_call(x, g, w_bf16, tabs, seq):
    n = x.shape[0]
    t = PROJ_ROWS
    blocks_per_seq = seq // t
    qkv_shapes = [jax.ShapeDtypeStruct((N_LANE_GROUPS, n // d, d * LANES), jnp.bfloat16)
                  for d in DILATIONS] * 3
    qkv_specs = [pl.BlockSpec((N_LANE_GROUPS, t // d, d * LANES), lambda i: (0, i, 0))
                 for d in DILATIONS] * 3
    outs = pl.pallas_call(
        _proj_kernel,
        grid=(n // t,),
        in_specs=[
            pl.BlockSpec((t, D_MODEL), lambda i: (i, 0)),
            pl.BlockSpec((1, D_MODEL), lambda i: (0, 0)),
            pl.BlockSpec((D_MODEL, IN_WIDTH), lambda i: (0, 0)),
            pl.BlockSpec((6, t, LANES), lambda i: (0, i % blocks_per_seq, 0)),
        ],
        out_specs=[pl.BlockSpec((t, POOL_WIDTH), lambda i: (i, 0)), *qkv_specs],
        out_shape=[jax.ShapeDtypeStruct((n, POOL_WIDTH), jnp.float32), *qkv_shapes],
        scratch_shapes=[pltpu.VMEM((t, LANES), jnp.float32)] * (2 * 3 * N_LANE_GROUPS),
        compiler_params=pltpu.CompilerParams(
            dimension_semantics=("arbitrary",), vmem_limit_bytes=VMEM_LIMIT),
        name="proj",
    )(x, g, w_bf16, tabs)
    n_pat = len(DILATIONS)
    return outs[0], outs[1:1 + n_pat], outs[1 + n_pat:1 + 2 * n_pat], outs[1 + 2 * n_pat:]


def _attn_kernel(*refs, n_blocks):
    n_pat = len(DILATIONS)
    q_refs = refs[:n_pat]
    k_refs = [refs[n_pat + 3 * i:n_pat + 3 * i + 3] for i in range(n_pat)]
    v_refs = [refs[4 * n_pat + 3 * i:4 * n_pat + 3 * i + 3] for i in range(n_pat)]
    o_ref, tbuf, mpat, mall, opat, lpat, bias_ref = refs[7 * n_pat:]
    t = pl.program_id(2)
    rows = ATTN_ROWS
    f32 = jnp.float32
    bf16 = jnp.bfloat16

    qi = lax.broadcasted_iota(jnp.int32, (2 * Q_CHUNK, K_CHUNK), 0) % Q_CHUNK
    kj = lax.broadcasted_iota(jnp.int32, (2 * Q_CHUNK, K_CHUNK), 1)
    band = (kj >= qi) & (kj <= qi + 2 * BAND_RADIUS)
    for variant in range(4):
        ok = band
        if variant & 1:
            ok = ok & (kj >= BAND_RADIUS)
        if variant & 2:
            ok = ok & (kj < BAND_RADIUS + Q_CHUNK)
        bias_ref[variant] = jnp.where(ok, 0.0, NEG_BIG).astype(f32)

    lane = lax.broadcasted_iota(jnp.int32, (Q_CHUNK, LANES), 1)
    first_head = lane < HEAD_DIM

    def pick_head(x):
        return jnp.where(first_head, x[:Q_CHUNK], x[Q_CHUNK:])

    tiles = []
    for pi, d in enumerate(DILATIONS):
        n_chunks = rows // (d * Q_CHUNK)
        for r in range(d):
            for c in range(n_chunks):
                q_lo = r + c * (d * Q_CHUNK)
                tok = pl.ds(q_lo, Q_CHUNK) if d == 1 else pl.ds(q_lo, Q_CHUNK, stride=d)
                tiles.append((pi, r, c, n_chunks, tok))

    def key_rows(trio, r, c, n_chunks):
        prev_ref, main_ref, next_ref = trio
        lanes = pl.ds(r * LANES, LANES)
        lo, hi = c * Q_CHUNK - BAND_RADIUS, (c + 1) * Q_CHUNK + BAND_RADIUS
        parts = []
        if lo < 0:
            parts.append(prev_ref[:, lanes])
            lo = 0
        main_hi = min(hi, n_chunks * Q_CHUNK)
        parts.append(main_ref[pl.ds(lo, main_hi - lo), lanes])
        if hi > main_hi:
            parts.append(next_ref[:, lanes])
        return parts[0] if len(parts) == 1 else jnp.concatenate(parts, axis=0)

    def bias_variant(c, n_chunks):
        at_start, at_end = c == 0, c == n_chunks - 1
        if n_blocks == 1:
            return int(at_start) + 2 * int(at_end)
        first = (t == 0).astype(jnp.int32) if at_start else 0
        last = (t == n_blocks - 1).astype(jnp.int32) if at_end else 0
        return first + 2 * last

    dyn_zero = jnp.minimum(t, 0)

    for n, (pi, r, c, n_chunks, tok) in enumerate(tiles):
        qc = q_refs[pi][pl.ds(c * Q_CHUNK, Q_CHUNK), pl.ds(r * LANES, LANES)]
        zero = jnp.zeros_like(qc)
        q2 = jnp.concatenate([jnp.where(first_head, qc, zero), jnp.where(first_head, zero, qc)], axis=0)
        kc = key_rows(k_refs[pi], r, c, n_chunks)
        s = lax.dot_general(q2, kc, (((1,), (1,)), ((), ())), preferred_element_type=f32)
        s = s + bias_ref[bias_variant(c, n_chunks)]
        tbuf[n + dyn_zero] = s
        m = jnp.broadcast_to(jnp.max(s, axis=-1, keepdims=True), (2 * Q_CHUNK, LANES))
        for h in range(2):
            mpat[pi, h, tok, :] = m[h * Q_CHUNK:(h + 1) * Q_CHUNK]

    pass_rows = 64
    for h in range(2):
        for i in range(rows // pass_rows):
            sl = pl.ds(i * pass_rows, pass_rows)
            mall[h, sl, :] = jnp.maximum(jnp.maximum(mpat[0, h, sl, :], mpat[1, h, sl, :]), mpat[2, h, sl, :])

    ones = jnp.ones((K_CHUNK, LANES), bf16)
    for n, (pi, r, c, n_chunks, tok) in enumerate(tiles):
        mb = jnp.concatenate([mall[0, tok, :], mall[1, tok, :]], axis=0)
        p = jnp.exp2(tbuf[n + dyn_zero] - jnp.concatenate([mb, mb], axis=1)).astype(bf16)
        vc = jnp.concatenate([key_rows(v_refs[pi], r, c, n_chunks), ones], axis=1)
        ol = jnp.dot(p, vc, preferred_element_type=f32)
        opat[pi, tok, :] = pick_head(ol[:, :LANES])
        lpat[pi, tok, :] = pick_head(ol[:, LANES:])

    for i in range(rows // pass_rows):
        sl = pl.ds(i * pass_rows, pass_rows)
        num = opat[0, sl, :] + opat[1, sl, :] + opat[2, sl, :]
        den = lpat[0, sl, :] + lpat[1, sl, :] + lpat[2, sl, :]
        o_ref[sl, :] = num / den


def _attn_call(q, k, v, batch, seq):
    rows = ATTN_ROWS
    n_blocks = seq // rows
    radius = BAND_RADIUS

    def view(a, d):
        return a.reshape(N_LANE_GROUPS, batch, seq // d, d * LANES)

    operands, in_specs = [], []
    for d, a in zip(DILATIONS, q):
        operands.append(view(a, d))
        in_specs.append(pl.BlockSpec((None, None, rows // d, d * LANES), lambda b, g, t: (g, b, t, 0)))
    for arrs in (k, v):
        for d, a in zip(DILATIONS, arrs):
            halo_per_block = rows // d // radius
            n_halo = seq // d // radius
            operands += [view(a, d)] * 3
            in_specs += [
                pl.BlockSpec((None, None, radius, d * LANES),
                             lambda b, g, t, hb=halo_per_block: (g, b, jnp.maximum(t * hb - 1, 0), 0)),
                pl.BlockSpec((None, None, rows // d, d * LANES), lambda b, g, t: (g, b, t, 0)),
                pl.BlockSpec((None, None, radius, d * LANES),
                             lambda b, g, t, hb=halo_per_block, nh=n_halo:
                             (g, b, jnp.minimum((t + 1) * hb, nh - 1), 0)),
            ]
    n_tiles = len(DILATIONS) * rows // Q_CHUNK
    return pl.pallas_call(
        functools.partial(_attn_kernel, n_blocks=n_blocks),
        grid=(batch, N_LANE_GROUPS, n_blocks),
        in_specs=in_specs,
        out_specs=pl.BlockSpec((None, None, rows, LANES), lambda b, g, t: (g, b, t, 0)),
        out_shape=jax.ShapeDtypeStruct((N_LANE_GROUPS, batch, seq, LANES), jnp.float32),
        scratch_shapes=[
            pltpu.VMEM((n_tiles, 2 * Q_CHUNK, K_CHUNK), jnp.float32),
            pltpu.VMEM((len(DILATIONS), 2, rows, LANES), jnp.float32),
            pltpu.VMEM((2, rows, LANES), jnp.float32),
            pltpu.VMEM((len(DILATIONS), rows, LANES), jnp.float32),
            pltpu.VMEM((len(DILATIONS), rows, LANES), jnp.float32),
            pltpu.VMEM((4, 2 * Q_CHUNK, K_CHUNK), jnp.float32),
        ],
        compiler_params=pltpu.CompilerParams(
            dimension_semantics=("arbitrary", "arbitrary", "arbitrary"), vmem_limit_bytes=VMEM_LIMIT),
        name="attn",
    )(*operands)


def _mix_kernel(x_ref, p_ref, pprev_ref, pnext_ref, a_ref, pw_ref, ps_ref, gp_ref, ga_ref,
                wo_ref, gpost_ref, o_ref, pbuf, mixed, *, seq):
    rows = MIX_ROWS
    f32 = jnp.float32
    i = pl.program_id(0)
    pos0 = (i * rows) % seq
    zero_rows = jnp.zeros((HALO_ROWS, POOL_WIDTH), f32)
    pbuf[pl.ds(0, HALO_ROWS), :] = jnp.where(pos0 > 0, pprev_ref[...], zero_rows)
    pbuf[pl.ds(HALO_ROWS, rows), :] = p_ref[...]
    pbuf[pl.ds(HALO_ROWS + rows, HALO_ROWS), :] = jnp.where(pos0 + rows < seq, pnext_ref[...], zero_rows)

    pos = pos0 + lax.broadcasted_iota(jnp.int32, (rows, LANES), 0)
    lane = lax.broadcasted_iota(jnp.int32, (rows, LANES), 1)
    means = []
    for half in range(2):
        w_small, w_big = POOL_WINDOWS[2 * half], POOL_WINDOWS[2 * half + 1]
        lanes = pl.ds(half * LANES, LANES)

        def wsum(lo, hi):
            acc = pbuf[pl.ds(HALO_ROWS + lo, rows), lanes]
            for off in range(lo + 1, hi):
                acc = acc + pbuf[pl.ds(HALO_ROWS + off, rows), lanes]
            return acc

        small = wsum(-(w_small // 2), w_small // 2)
        big = small + wsum(-(w_big // 2), -(w_small // 2)) + wsum(w_small // 2, w_big // 2)

        def count(w):
            return (jnp.minimum(pos + w // 2, seq) - jnp.maximum(pos - w // 2, 0)).astype(f32)

        first = lane < POOL_GROUP_DIM
        total = jnp.where(first, small, big)
        cnt = jnp.where(first, count(w_small), count(w_big))
        means.append(total / cnt - pbuf[pl.ds(HALO_ROWS, rows), lanes])
    pooled = jnp.concatenate(means, axis=-1).astype(jnp.bfloat16)
    pool_out = jnp.dot(pooled, pw_ref[...], preferred_element_type=f32) * ps_ref[...]
    mixed[:, pl.ds(0, POOL_WIDTH)] = _rms(pool_out, gp_ref[...]).astype(jnp.bfloat16)

    ssq = jnp.zeros((rows, 1), f32)
    for g in range(N_LANE_GROUPS):
        a = a_ref[g]
        ssq = ssq + jnp.sum(a * a, axis=-1, keepdims=True)
    inv = lax.rsqrt(ssq / ATTN_WIDTH + EPS)
    for g in range(N_LANE_GROUPS):
        ga = ga_ref[:, pl.ds(g * LANES, LANES)]
        mixed[:, pl.ds(POOL_WIDTH + g * LANES, LANES)] = (a_ref[g] * inv * ga).astype(jnp.bfloat16)

    y = jnp.dot(mixed[...], wo_ref[...], preferred_element_type=f32)
    o_ref[...] = x_ref[...] + _rms(y, gpost_ref[...])


def _mix_call(x, p_in, attn, pool_bd, pool_scale, g_pool, g_attn, w_out_bf16, g_post, seq):
    n = x.shape[0]
    rows = MIX_ROWS
    hb = rows // HALO_ROWS
    n_halo_blocks = n // HALO_ROWS
    const = lambda i: (0, 0)
    return pl.pallas_call(
        functools.partial(_mix_kernel, seq=seq),
        grid=(n // rows,),
        in_specs=[
            pl.BlockSpec((rows, D_MODEL), lambda i: (i, 0)),
            pl.BlockSpec((rows, POOL_WIDTH), lambda i: (i, 0)),
            pl.BlockSpec((HALO_ROWS, POOL_WIDTH), lambda i: (jnp.maximum(i * hb - 1, 0), 0)),
            pl.BlockSpec((HALO_ROWS, POOL_WIDTH),
                         lambda i: (jnp.minimum((i + 1) * hb, n_halo_blocks - 1), 0)),
            pl.BlockSpec((N_LANE_GROUPS, rows, LANES), lambda i: (0, i, 0)),
            pl.BlockSpec((POOL_WIDTH, POOL_WIDTH), const),
            pl.BlockSpec((1, POOL_WIDTH), const),
            pl.BlockSpec((1, POOL_WIDTH), const),
            pl.BlockSpec((1, ATTN_WIDTH), const),
            pl.BlockSpec((D_MODEL, D_MODEL), const),
            pl.BlockSpec((1, D_MODEL), const),
        ],
        out_specs=pl.BlockSpec((rows, D_MODEL), lambda i: (i, 0)),
        out_shape=jax.ShapeDtypeStruct((n, D_MODEL), jnp.float32),
        scratch_shapes=[
            pltpu.VMEM((rows + 2 * HALO_ROWS, POOL_WIDTH), jnp.float32),
            pltpu.VMEM((rows, D_MODEL), jnp.bfloat16),
        ],
        compiler_params=pltpu.CompilerParams(
            dimension_semantics=("arbitrary",), vmem_limit_bytes=VMEM_LIMIT),
        name="mix",
    )(x, p_in, p_in, p_in, attn, pool_bd, pool_scale, g_pool, g_attn, w_out_bf16, g_post)


def _ffn_kernel(x_ref, xprev_ref, xnext_ref, gpre_ref, wup_ref, cw_ref, cb_ref, wdn_ref, gpost_ref,
                o_ref, xs, hbuf, *bufs, seq):
    n_chunks = D_FF // FFN_CHUNK
    ubufs, abufs = bufs[:FFN_U_RING], bufs[FFN_U_RING:]
    rows = FFN_ROWS
    run = rows // HALO_ROWS
    n_slabs = D_MODEL // LANES
    f32 = jnp.float32
    i = pl.program_id(0)
    pos0 = (i * rows) % seq
    g = gpre_ref[...]

    for l in range(n_slabs):
        for s in range(HALO_ROWS):
            xs[l, pl.ds(s, run, stride=HALO_ROWS), :] = x_ref[pl.ds(s * run, run), pl.ds(l * LANES, LANES)]
    ssq = jnp.zeros((rows, 1), f32)
    for l in range(n_slabs):
        xl = xs[l]
        ssq = ssq + jnp.sum(xl * xl, axis=-1, keepdims=True)
    inv = lax.rsqrt(ssq / D_MODEL + EPS)
    for l in range(n_slabs):
        lanes = pl.ds(l * LANES, LANES)
        hbuf[pl.ds(HALO_ROWS, rows), lanes] = xs[l] * inv * gpre_ref[:, lanes]
    hprev = jnp.where(pos0 > 0, _rms(xprev_ref[...], g), 0.0)
    hnext = jnp.where(pos0 + rows < seq, _rms(xnext_ref[...], g), 0.0)
    hbuf[pl.ds(0, HALO_ROWS), :] = jnp.broadcast_to(hprev[HALO_ROWS - 1:], (HALO_ROWS, D_MODEL))
    hbuf[pl.ds(HALO_ROWS + rows, HALO_ROWS), :] = jnp.broadcast_to(hnext[:1], (HALO_ROWS, D_MODEL))
    h = hbuf[...].astype(jnp.bfloat16)

    sub = lax.broadcasted_iota(jnp.int32, (HALO_ROWS, FFN_CHUNK), 0)

    def conv(c, half):
        col = half * D_FF + c * FFN_CHUNK
        w = cw_ref[:, pl.ds(col, FFN_CHUNK)]
        acc = cb_ref[:, pl.ds(col, FFN_CHUNK)]
        for j in range(3):
            u = ubufs[c % FFN_U_RING][pl.ds(HALO_ROWS * j, rows), pl.ds(half * FFN_CHUNK, FFN_CHUNK)]
            acc = acc + u * w[j:j + 1]
        return acc

    def up(c):
        buf = ubufs[c % FFN_U_RING]
        for half in range(2):
            col = half * D_FF + c * FFN_CHUNK
            cols = pl.ds(half * FFN_CHUNK, FFN_CHUNK)
            buf[:, cols] = jnp.dot(h, wup_ref[:, pl.ds(col, FFN_CHUNK)], preferred_element_type=f32)
            halo = jnp.broadcast_to(buf[pl.ds(0, 1), cols], (HALO_ROWS, FFN_CHUNK))
            last = pltpu.roll(buf[pl.ds(rows, HALO_ROWS), cols], 1, 0)
            buf[pl.ds(0, HALO_ROWS), cols] = jnp.where(sub == 0, halo, last)
            halo = jnp.broadcast_to(buf[pl.ds(rows + HALO_ROWS, 1), cols], (HALO_ROWS, FFN_CHUNK))
            first = pltpu.roll(buf[pl.ds(HALO_ROWS, HALO_ROWS), cols], HALO_ROWS - 1, 0)
            buf[pl.ds(rows + HALO_ROWS, HALO_ROWS), cols] = jnp.where(sub == HALO_ROWS - 1, halo, first)

    def down(g, n_in_group):
        width = n_in_group * FFN_CHUNK
        w = wdn_ref[pl.ds(g * FFN_DOWN_GROUP * FFN_CHUNK, width), :]
        return jnp.dot(abufs[g][:, pl.ds(0, width)], w, preferred_element_type=f32)

    for c in range(min(FFN_LOOKAHEAD, n_chunks)):
        up(c)
    y = None
    for c in range(n_chunks):
        if c + FFN_LOOKAHEAD < n_chunks:
            up(c + FFN_LOOKAHEAD)
        gate = conv(c, 0)
        val = conv(c, 1)
        gelu = 0.5 * gate * (1.0 + lax.erf(gate * (2.0 ** -0.5)))
        g, j = divmod(c, FFN_DOWN_GROUP)
        abufs[g][:, pl.ds(j * FFN_CHUNK, FFN_CHUNK)] = (gelu * val).astype(jnp.bfloat16)
        if j + 1 == FFN_DOWN_GROUP or c + 1 == n_chunks:
            part = down(g, j + 1)
            y = part if y is None else y + part
    res = _rms(y, gpost_ref[...])
    for l in range(n_slabs):
        xs[l] = xs[l] + res[:, l * LANES:(l + 1) * LANES]
    for l in range(n_slabs):
        for s in range(HALO_ROWS):
            o_ref[pl.ds(s * run, run), pl.ds(l * LANES, LANES)] = xs[l, pl.ds(s, run, stride=HALO_ROWS), :]


def _ffn_call(x1, g_pre, w_up_bf16, conv_w, conv_b, w_down_bf16, g_post, seq):
    n = x1.shape[0]
    rows = FFN_ROWS
    hb = rows // HALO_ROWS
    n_halo_blocks = n // HALO_ROWS
    const = lambda i: (0, 0)
    return pl.pallas_call(
        functools.partial(_ffn_kernel, seq=seq),
        grid=(n // rows,),
        in_specs=[
            pl.BlockSpec((rows, D_MODEL), lambda i: (i, 0)),
            pl.BlockSpec((HALO_ROWS, D_MODEL), lambda i: (jnp.maximum(i * hb - 1, 0), 0)),
            pl.BlockSpec((HALO_ROWS, D_MODEL),
                         lambda i: (jnp.minimum((i + 1) * hb, n_halo_blocks - 1), 0)),
            pl.BlockSpec((1, D_MODEL), const),
            pl.BlockSpec((D_MODEL, 2 * D_FF), const, pipeline_mode=pl.Buffered(1)),
            pl.BlockSpec((3, 2 * D_FF), const),
            pl.BlockSpec((1, 2 * D_FF), const),
            pl.BlockSpec((D_FF, D_MODEL), const, pipeline_mode=pl.Buffered(1)),
            pl.BlockSpec((1, D_MODEL), const),
        ],
        out_specs=pl.BlockSpec((rows, D_MODEL), lambda i: (i, 0)),
        out_shape=jax.ShapeDtypeStruct((n, D_MODEL), jnp.float32),
        scratch_shapes=[
            pltpu.VMEM((D_MODEL // LANES, rows, LANES), jnp.float32),
            pltpu.VMEM((rows + 2 * HALO_ROWS, D_MODEL), jnp.float32),
            *[pltpu.VMEM((rows + 2 * HALO_ROWS, 2 * FFN_CHUNK), jnp.float32)] * FFN_U_RING,
            *[pltpu.VMEM((rows, FFN_DOWN_GROUP * FFN_CHUNK), jnp.bfloat16)]
            * pl.cdiv(D_FF // FFN_CHUNK, FFN_DOWN_GROUP),
        ],
        compiler_params=pltpu.CompilerParams(
            dimension_semantics=("arbitrary",), vmem_limit_bytes=VMEM_LIMIT),
        name="ffn",
    )(x1, x1, x1, g_pre, w_up_bf16, conv_w, conv_b, w_down_bf16, g_post)


def _block_diag(pool_w):
    groups = pool_w.shape[0]
    bd = jnp.zeros((POOL_WIDTH, POOL_WIDTH), pool_w.dtype)
    for g in range(groups):
        lo = g * POOL_GROUP_DIM
        bd = bd.at[lo:lo + POOL_GROUP_DIM, lo:lo + POOL_GROUP_DIM].set(pool_w[g])
    return bd


def _layer(x, params):
    (g_mix_pre, g_mix_post, w_in, pool_bd, pool_scale, g_pool_out, g_attn_out, w_out,
     g_ffn_pre, g_ffn_post, w_up, conv_w, conv_b, w_down) = params
    batch, seq, _ = x.shape
    assert seq % ATTN_ROWS == 0 and seq % PROJ_ROWS == 0 and seq % MIX_ROWS == 0 and seq % FFN_ROWS == 0
    xf = x.reshape(batch * seq, D_MODEL)
    p_in, q, k, v = _proj_call(xf, g_mix_pre, w_in, _rope_tables(seq), seq)
    attn = _attn_call(q, k, v, batch, seq).reshape(N_LANE_GROUPS, batch * seq, LANES)
    x1 = _mix_call(xf, p_in, attn, pool_bd, pool_scale, g_pool_out, g_attn_out, w_out, g_mix_post, seq)
    y = _ffn_call(x1, g_ffn_pre, w_up, conv_w, conv_b, w_down, g_ffn_post, seq)
    return y.reshape(batch, seq, D_MODEL)


def kernel(x_prompt, x_sample, g_mix_pre, g_mix_post, w_in, pool_w, pool_scale, g_pool_out, g_attn_out,
           w_out, g_ffn_pre, g_ffn_post, w_up, conv_w, conv_b, w_down):
    depth = w_in.shape[0]
    bf16 = jnp.bfloat16

    def layer_params(l):
        return (g_mix_pre[l][None], g_mix_post[l][None], w_in[l].astype(bf16),
                _block_diag(pool_w[l]).astype(bf16), pool_scale[l][None], g_pool_out[l][None],
                g_attn_out[l][None], w_out[l].astype(bf16), g_ffn_pre[l][None], g_ffn_post[l][None],
                w_up[l].astype(bf16), conv_w[l], conv_b[l][None], w_down[l].astype(bf16))

    params = [layer_params(l) for l in range(depth)]

    def run(x):
        for p in params:
            x = _layer(x, p)
        return x

    return (run(x_prompt), run(x_sample))
```

```python
import functools
import math

import jax
import jax.numpy as jnp
from jax import lax
from jax.experimental import pallas as pl
from jax.experimental.pallas import tpu as pltpu

D_MODEL = 1024
POOL_WIDTH = 256
POOL_WINDOWS = (2, 4, 8, 16)
POOL_GROUP_DIM = 64
ATTN_WIDTH = 768
HEAD_DIM = 64
DILATIONS = (1, 4, 16)
QKV_LAYOUTS = (4, 16)
BAND_RADIUS = 64
ROPE_THETA = 500000.0
ROPE_DIM = 16
D_FF = 2816
EPS = 1e-6
IN_WIDTH = POOL_WIDTH + 3 * ATTN_WIDTH

LANES = 128
N_LANE_GROUPS = ATTN_WIDTH // LANES
NEG_BIG = -1e30

PROJ_ROWS = 512
ATTN_ROWS = 2048
Q_CHUNK = 128
K_CHUNK = Q_CHUNK + 2 * BAND_RADIUS
MIX_ROWS = 512
FFN_ROWS = 512
FFN_CHUNK = 256
FFN_LOOKAHEAD = 2
FFN_U_RING = FFN_LOOKAHEAD + 1
FFN_DOWN_GROUP = 4
HALO_ROWS = 8

VMEM_LIMIT = 56 * 1024 * 1024


def _rms(x, g):
    ms = jnp.mean(x * x, axis=-1, keepdims=True)
    return x * lax.rsqrt(ms + EPS) * g


def _proj_kernel(x_ref, g_ref, w_ref, tab_ref, p_ref, *refs):
    n_lay = len(QKV_LAYOUTS)
    outs = [refs[n_lay * i:n_lay * (i + 1)] for i in range(3)]
    stage = refs[3 * n_lay:]
    rows = PROJ_ROWS
    bf16 = jnp.bfloat16
    h = _rms(x_ref[...], g_ref[...]).astype(bf16)
    proj = jnp.dot(h, w_ref[...], preferred_element_type=jnp.float32)
    p_ref[...] = proj[:, :POOL_WIDTH]
    qc, qs1, qs2 = tab_ref[0], tab_ref[1], tab_ref[2]
    kc, ks1, ks2 = tab_ref[3], tab_ref[4], tab_ref[5]

    def rope(t, c, s1, s2):
        return t * c + pltpu.roll(t, LANES - 8, 1) * s1 + pltpu.roll(t, 8, 1) * s2

    def emit(val, out_refs, g, slab_a, slab_b):
        o4, o16 = out_refs
        slab_a[...] = val
        for r_lo in range(4):
            plane = slab_a[pl.ds(r_lo, rows // 4, stride=4), :]
            o4[g, :, pl.ds(r_lo * LANES, LANES)] = plane.astype(bf16)
            slab_b[pl.ds(r_lo * (rows // 4), rows // 4), :] = plane
        for r_lo in range(4):
            for r_hi in range(4):
                piece = slab_b[pl.ds(r_lo * (rows // 4) + r_hi, rows // 16, stride=4), :]
                o16[g, :, pl.ds((4 * r_hi + r_lo) * LANES, LANES)] = piece.astype(bf16)

    for g in range(N_LANE_GROUPS):
        lo = POOL_WIDTH + g * LANES
        emit(rope(proj[:, lo:lo + LANES], qc, qs1, qs2), outs[0], g, stage[6 * g], stage[6 * g + 1])
        lo += ATTN_WIDTH
        emit(rope(proj[:, lo:lo + LANES], kc, ks1, ks2), outs[1], g, stage[6 * g + 2], stage[6 * g + 3])
        lo += ATTN_WIDTH
        emit(proj[:, lo:lo + LANES], outs[2], g, stage[6 * g + 4], stage[6 * g + 5])


def _rope_tables(seq):
    pos = jnp.arange(seq, dtype=jnp.float32)
    inv_freq = ROPE_THETA ** (-jnp.arange(0, ROPE_DIM, 2, dtype=jnp.float32) / ROPE_DIM)
    half = ROPE_DIM // 2
    dim = jnp.arange(LANES) % HEAD_DIM
    ang = pos[:, None] * inv_freq[dim % half][None, :]
    cos, sin = jnp.cos(ang), jnp.sin(ang)
    first_half, rotated = (dim < half)[None, :], (dim < ROPE_DIM)[None, :]
    c = jnp.where(rotated, cos, 1.0)
    s1 = jnp.where(first_half, -sin, 0.0)
    s2 = jnp.where(rotated & ~first_half, sin, 0.0)
    tabs = jnp.stack([c, s1, s2])
    scale = HEAD_DIM ** -0.5 * math.log2(math.e)
    return jnp.concatenate([tabs * scale, tabs], axis=0)


def _proj_call(x, g, w_bf16, tabs, seq):
    n = x.shape[0]
    t = PROJ_ROWS
    blocks_per_seq = seq // t
    qkv_shapes = [jax.ShapeDtypeStruct((N_LANE_GROUPS, n // d, d * LANES), jnp.bfloat16)
                  for d in QKV_LAYOUTS] * 3
    qkv_specs = [pl.BlockSpec((N_LANE_GROUPS, t // d, d * LANES), lambda i: (0, i, 0))
                 for d in QKV_LAYOUTS] * 3
    outs = pl.pallas_call(
        _proj_kernel,
        grid=(n // t,),
        in_specs=[
            pl.BlockSpec((t, D_MODEL), lambda i: (i, 0)),
            pl.BlockSpec((1, D_MODEL), lambda i: (0, 0)),
            pl.BlockSpec((D_MODEL, IN_WIDTH), lambda i: (0, 0)),
            pl.BlockSpec((6, t, LANES), lambda i: (0, i % blocks_per_seq, 0)),
        ],
        out_specs=[pl.BlockSpec((t, POOL_WIDTH), lambda i: (i, 0)), *qkv_specs],
        out_shape=[jax.ShapeDtypeStruct((n, POOL_WIDTH), jnp.float32), *qkv_shapes],
        scratch_shapes=[pltpu.VMEM((t, LANES), jnp.float32)] * (2 * 3 * N_LANE_GROUPS),
        compiler_params=pltpu.CompilerParams(
            dimension_semantics=("arbitrary",), vmem_limit_bytes=VMEM_LIMIT),
        name="proj",
    )(x, g, w_bf16, tabs)
    n_lay = len(QKV_LAYOUTS)
    return outs[0], outs[1:1 + n_lay], outs[1 + n_lay:1 + 2 * n_lay], outs[1 + 2 * n_lay:]


def _attn_kernel(*refs, n_blocks):
    n_lay = len(QKV_LAYOUTS)
    q_refs = dict(zip(QKV_LAYOUTS, refs[:n_lay]))
    k_refs = {d: refs[n_lay + 3 * i:n_lay + 3 * i + 3] for i, d in enumerate(QKV_LAYOUTS)}
    v_refs = {d: refs[4 * n_lay + 3 * i:4 * n_lay + 3 * i + 3] for i, d in enumerate(QKV_LAYOUTS)}
    o_ref, tbuf, mpat, mall, opat, lpat, bias_ref = refs[7 * n_lay:]
    t = pl.program_id(2)
    rows = ATTN_ROWS
    plane = rows // 4
    f32 = jnp.float32
    bf16 = jnp.bfloat16

    @pl.when((pl.program_id(0) == 0) & (pl.program_id(1) == 0) & (t == 0))
    def _():
        row = lax.broadcasted_iota(jnp.int32, (2 * Q_CHUNK, K_CHUNK), 0) % Q_CHUNK
        col = lax.broadcasted_iota(jnp.int32, (2 * Q_CHUNK, K_CHUNK), 1)
        q_rows, k_rows = Q_CHUNK // 4, K_CHUNK // 4
        delta = 4 * (col % k_rows - row % q_rows) + (col // k_rows - row // q_rows)
        families = (
            ((col >= row) & (col <= row + 2 * BAND_RADIUS), col, BAND_RADIUS, BAND_RADIUS + Q_CHUNK),
            ((delta >= 0) & (delta <= 2 * BAND_RADIUS), col % k_rows, BAND_RADIUS // 4,
             (BAND_RADIUS + Q_CHUNK) // 4),
        )
        for fam, (band, key_pos, lo, hi) in enumerate(families):
            for variant in range(4):
                ok = band
                if variant & 1:
                    ok = ok & (key_pos >= lo)
                if variant & 2:
                    ok = ok & (key_pos < hi)
                bias_ref[4 * fam + variant] = jnp.where(ok, 0.0, NEG_BIG).astype(f32)

    lane = lax.broadcasted_iota(jnp.int32, (Q_CHUNK, LANES), 1)
    first_head = lane < HEAD_DIM

    def pick_head(x):
        return jnp.where(first_head, x[:Q_CHUNK], x[Q_CHUNK:])

    def ext_rows(trio, lo, hi, block):
        prev_ref, main_ref, next_ref = trio
        lanes = pl.ds(block * LANES, LANES)
        n_main, n_halo = main_ref.shape[0], prev_ref.shape[0]
        parts = []
        if lo < 0:
            parts.append(prev_ref[pl.ds(n_halo + lo, -lo), lanes])
        parts.append(main_ref[pl.ds(max(lo, 0), min(hi, n_main) - max(lo, 0)), lanes])
        if hi > n_main:
            parts.append(next_ref[pl.ds(0, hi - n_main), lanes])
        return parts

    def rows_of(parts):
        return parts[0] if len(parts) == 1 else jnp.concatenate(parts, axis=0)

    tiles = []
    q_rows, k_rows = Q_CHUNK // 4, K_CHUNK // 4
    for c in range(rows // Q_CHUNK):
        lo = c * q_rows - BAND_RADIUS // 4
        q_parts = [q_refs[4][pl.ds(c * q_rows, q_rows), pl.ds(j * LANES, LANES)] for j in range(4)]
        keys = lambda trios, lo=lo: rows_of(
            [p for j in range(4) for p in ext_rows(trios[4], lo, lo + k_rows, j)])
        acc = [(j * q_rows, pl.ds(j * plane + c * q_rows, q_rows)) for j in range(4)]
        tiles.append((0, q_parts, keys, 1, c == 0, c == rows // Q_CHUNK - 1, acc))
    for r in range(4):
        for c in range(plane // Q_CHUNK):
            lo = c * Q_CHUNK - BAND_RADIUS
            q_parts = [q_refs[4][pl.ds(c * Q_CHUNK, Q_CHUNK), pl.ds(r * LANES, LANES)]]
            keys = lambda trios, lo=lo, r=r: rows_of(ext_rows(trios[4], lo, lo + K_CHUNK, r))
            acc = [(0, pl.ds(r * plane + c * Q_CHUNK, Q_CHUNK))]
            tiles.append((1, q_parts, keys, 0, c == 0, c == plane // Q_CHUNK - 1, acc))
    for r in range(16):
        q_parts = [q_refs[16][:, pl.ds(r * LANES, LANES)]]
        keys = lambda trios, r=r: rows_of(ext_rows(trios[16], -BAND_RADIUS, Q_CHUNK + BAND_RADIUS, r))
        acc = [(0, pl.ds((r % 4) * plane + r // 4, Q_CHUNK, stride=4))]
        tiles.append((2, q_parts, keys, 0, True, True, acc))

    def bias_index(family, at_start, at_end):
        if n_blocks == 1:
            return 4 * family + int(at_start) + 2 * int(at_end)
        first = (t == 0).astype(jnp.int32) if at_start else 0
        last = (t == n_blocks - 1).astype(jnp.int32) if at_end else 0
        return 4 * family + first + 2 * last

    def store_rows(ref, lead, acc, x):
        for start, idx in acc:
            ref[(*lead, idx, slice(None))] = x[start:start + idx.size]

    def load_rows(ref, lead, acc):
        return rows_of([ref[(*lead, idx, slice(None))] for _, idx in acc])

    dyn_zero = jnp.minimum(t, 0)

    for n, (pi, q_parts, keys, family, at_start, at_end, acc) in enumerate(tiles):
        qc = rows_of(q_parts)
        zero = jnp.zeros_like(qc)
        q2 = jnp.concatenate([jnp.where(first_head, qc, zero), jnp.where(first_head, zero, qc)], axis=0)
        s = lax.dot_general(q2, keys(k_refs), (((1,), (1,)), ((), ())), preferred_element_type=f32)
        s = s + bias_ref[bias_index(family, at_start, at_end)]
        tbuf[n + dyn_zero] = s
        m = jnp.broadcast_to(jnp.max(s, axis=-1, keepdims=True), (2 * Q_CHUNK, LANES))
        for h in range(2):
            store_rows(mpat, (pi, h), acc, m[h * Q_CHUNK:(h + 1) * Q_CHUNK])

    pass_rows = 64
    for h in range(2):
        for i in range(rows // pass_rows):
            sl = pl.ds(i * pass_rows, pass_rows)
            mall[h, sl, :] = jnp.maximum(jnp.maximum(mpat[0, h, sl, :], mpat[1, h, sl, :]), mpat[2, h, sl, :])

    ones = jnp.ones((K_CHUNK, LANES), bf16)
    for n, (pi, q_parts, keys, family, at_start, at_end, acc) in enumerate(tiles):
        mb = jnp.concatenate([load_rows(mall, (0,), acc), load_rows(mall, (1,), acc)], axis=0)
        p = jnp.exp2(tbuf[n + dyn_zero] - jnp.concatenate([mb, mb], axis=1)).astype(bf16)
        vc = jnp.concatenate([keys(v_refs), ones], axis=1)
        ol = jnp.dot(p, vc, preferred_element_type=f32)
        store_rows(opat, (pi,), acc, pick_head(ol[:, :LANES]))
        store_rows(lpat, (pi,), acc, pick_head(ol[:, LANES:]))

    for r in range(4):
        for i in range(plane // pass_rows):
            sl = pl.ds(r * plane + i * pass_rows, pass_rows)
            num = opat[0, sl, :] + opat[1, sl, :] + opat[2, sl, :]
            den = lpat[0, sl, :] + lpat[1, sl, :] + lpat[2, sl, :]
            o_ref[pl.ds(r + 4 * i * pass_rows, pass_rows, stride=4), :] = num / den


def _attn_call(q, k, v, batch, seq):
    rows = ATTN_ROWS
    n_blocks = seq // rows
    radius = BAND_RADIUS

    def view(a, d):
        return a.reshape(N_LANE_GROUPS, batch, seq // d, d * LANES)

    operands, in_specs = [], []
    for d, a in zip(QKV_LAYOUTS, q):
        operands.append(view(a, d))
        in_specs.append(pl.BlockSpec((None, None, rows // d, d * LANES), lambda b, g, t: (g, b, t, 0)))
    for arrs in (k, v):
        for d, a in zip(QKV_LAYOUTS, arrs):
            halo_per_block = rows // d // radius
            n_halo = seq // d // radius
            operands += [view(a, d)] * 3
            in_specs += [
                pl.BlockSpec((None, None, radius, d * LANES),
                             lambda b, g, t, hb=halo_per_block: (g, b, jnp.maximum(t * hb - 1, 0), 0)),
                pl.BlockSpec((None, None, rows // d, d * LANES), lambda b, g, t: (g, b, t, 0)),
                pl.BlockSpec((None, None, radius, d * LANES),
                             lambda b, g, t, hb=halo_per_block, nh=n_halo:
                             (g, b, jnp.minimum((t + 1) * hb, nh - 1), 0)),
            ]
    n_tiles = len(DILATIONS) * rows // Q_CHUNK
    return pl.pallas_call(
        functools.partial(_attn_kernel, n_blocks=n_blocks),
        grid=(batch, N_LANE_GROUPS, n_blocks),
        in_specs=in_specs,
        out_specs=pl.BlockSpec((None, None, rows, LANES), lambda b, g, t: (g, b, t, 0)),
        out_shape=jax.ShapeDtypeStruct((N_LANE_GROUPS, batch, seq, LANES), jnp.float32),
        scratch_shapes=[
            pltpu.VMEM((n_tiles, 2 * Q_CHUNK, K_CHUNK), jnp.float32),
            pltpu.VMEM((len(DILATIONS), 2, rows, LANES), jnp.float32),
            pltpu.VMEM((2, rows, LANES), jnp.float32),
            pltpu.VMEM((len(DILATIONS), rows, LANES), jnp.float32),
            pltpu.VMEM((len(DILATIONS), rows, LANES), jnp.float32),
            pltpu.VMEM((8, 2 * Q_CHUNK, K_CHUNK), jnp.float32),
        ],
        compiler_params=pltpu.CompilerParams(
            dimension_semantics=("arbitrary", "arbitrary", "arbitrary"), vmem_limit_bytes=VMEM_LIMIT),
        name="attn",
    )(*operands)


def _mix_kernel(x_ref, p_ref, pprev_ref, pnext_ref, a_ref, pw_ref, ps_ref, gp_ref, ga_ref,
                wo_ref, gpost_ref, o_ref, pbuf, mixed, *, seq):
    rows = MIX_ROWS
    f32 = jnp.float32
    i = pl.program_id(0)
    pos0 = (i * rows) % seq
    zero_rows = jnp.zeros((HALO_ROWS, POOL_WIDTH), f32)
    pbuf[pl.ds(0, HALO_ROWS), :] = jnp.where(pos0 > 0, pprev_ref[...], zero_rows)
    pbuf[pl.ds(HALO_ROWS, rows), :] = p_ref[...]
    pbuf[pl.ds(HALO_ROWS + rows, HALO_ROWS), :] = jnp.where(pos0 + rows < seq, pnext_ref[...], zero_rows)

    pos = pos0 + lax.broadcasted_iota(jnp.int32, (rows, LANES), 0)
    lane = lax.broadcasted_iota(jnp.int32, (rows, LANES), 1)
    means = []
    for half in range(2):
        w_small, w_big = POOL_WINDOWS[2 * half], POOL_WINDOWS[2 * half + 1]
        lanes = pl.ds(half * LANES, LANES)

        def wsum(lo, hi):
            acc = pbuf[pl.ds(HALO_ROWS + lo, rows), lanes]
            for off in range(lo + 1, hi):
                acc = acc + pbuf[pl.ds(HALO_ROWS + off, rows), lanes]
            return acc

        small = wsum(-(w_small // 2), w_small // 2)
        big = small + wsum(-(w_big // 2), -(w_small // 2)) + wsum(w_small // 2, w_big // 2)

        def count(w):
            return (jnp.minimum(pos + w // 2, seq) - jnp.maximum(pos - w // 2, 0)).astype(f32)

        first = lane < POOL_GROUP_DIM
        total = jnp.where(first, small, big)
        cnt = jnp.where(first, count(w_small), count(w_big))
        means.append(total / cnt - pbuf[pl.ds(HALO_ROWS, rows), lanes])
    pooled = jnp.concatenate(means, axis=-1).astype(jnp.bfloat16)
    pool_out = jnp.dot(pooled, pw_ref[...], preferred_element_type=f32) * ps_ref[...]
    mixed[:, pl.ds(0, POOL_WIDTH)] = _rms(pool_out, gp_ref[...]).astype(jnp.bfloat16)

    ssq = jnp.zeros((rows, 1), f32)
    for g in range(N_LANE_GROUPS):
        a = a_ref[g]
        ssq = ssq + jnp.sum(a * a, axis=-1, keepdims=True)
    inv = lax.rsqrt(ssq / ATTN_WIDTH + EPS)
    for g in range(N_LANE_GROUPS):
        ga = ga_ref[:, pl.ds(g * LANES, LANES)]
        mixed[:, pl.ds(POOL_WIDTH + g * LANES, LANES)] = (a_ref[g] * inv * ga).astype(jnp.bfloat16)

    y = jnp.dot(mixed[...], wo_ref[...], preferred_element_type=f32)
    o_ref[...] = x_ref[...] + _rms(y, gpost_ref[...])


def _mix_call(x, p_in, attn, pool_bd, pool_scale, g_pool, g_attn, w_out_bf16, g_post, seq):
    n = x.shape[0]
    rows = MIX_ROWS
    hb = rows // HALO_ROWS
    n_halo_blocks = n // HALO_ROWS
    const = lambda i: (0, 0)
    return pl.pallas_call(
        functools.partial(_mix_kernel, seq=seq),
        grid=(n // rows,),
        in_specs=[
            pl.BlockSpec((rows, D_MODEL), lambda i: (i, 0)),
            pl.BlockSpec((rows, POOL_WIDTH), lambda i: (i, 0)),
            pl.BlockSpec((HALO_ROWS, POOL_WIDTH), lambda i: (jnp.maximum(i * hb - 1, 0), 0)),
            pl.BlockSpec((HALO_ROWS, POOL_WIDTH),
                         lambda i: (jnp.minimum((i + 1) * hb, n_halo_blocks - 1), 0)),
            pl.BlockSpec((N_LANE_GROUPS, rows, LANES), lambda i: (0, i, 0)),
            pl.BlockSpec((POOL_WIDTH, POOL_WIDTH), const),
            pl.BlockSpec((1, POOL_WIDTH), const),
            pl.BlockSpec((1, POOL_WIDTH), const),
            pl.BlockSpec((1, ATTN_WIDTH), const),
            pl.BlockSpec((D_MODEL, D_MODEL), const),
            pl.BlockSpec((1, D_MODEL), const),
        ],
        out_specs=pl.BlockSpec((rows, D_MODEL), lambda i: (i, 0)),
        out_shape=jax.ShapeDtypeStruct((n, D_MODEL), jnp.float32),
        scratch_shapes=[
            pltpu.VMEM((rows + 2 * HALO_ROWS, POOL_WIDTH), jnp.float32),
            pltpu.VMEM((rows, D_MODEL), jnp.bfloat16),
        ],
        compiler_params=pltpu.CompilerParams(
            dimension_semantics=("arbitrary",), vmem_limit_bytes=VMEM_LIMIT),
        name="mix",
    )(x, p_in, p_in, p_in, attn, pool_bd, pool_scale, g_pool, g_attn, w_out_bf16, g_post)


def _ffn_kernel(x_ref, xprev_ref, xnext_ref, gpre_ref, wup_ref, cw_ref, cb_ref, wdn_ref, gpost_ref,
                o_ref, xs, hbuf, *bufs, seq):
    n_chunks = D_FF // FFN_CHUNK
    ubufs, abufs = bufs[:FFN_U_RING], bufs[FFN_U_RING:]
    rows = FFN_ROWS
    run = rows // HALO_ROWS
    n_slabs = D_MODEL // LANES
    f32 = jnp.float32
    i = pl.program_id(0)
    pos0 = (i * rows) % seq
    g = gpre_ref[...]

    for l in range(n_slabs):
        for s in range(HALO_ROWS):
            xs[l, pl.ds(s, run, stride=HALO_ROWS), :] = x_ref[pl.ds(s * run, run), pl.ds(l * LANES, LANES)]
    ssq = jnp.zeros((rows, 1), f32)
    for l in range(n_slabs):
        xl = xs[l]
        ssq = ssq + jnp.sum(xl * xl, axis=-1, keepdims=True)
    inv = lax.rsqrt(ssq / D_MODEL + EPS)
    for l in range(n_slabs):
        lanes = pl.ds(l * LANES, LANES)
        hbuf[pl.ds(HALO_ROWS, rows), lanes] = xs[l] * inv * gpre_ref[:, lanes]
    hprev = jnp.where(pos0 > 0, _rms(xprev_ref[...], g), 0.0)
    hnext = jnp.where(pos0 + rows < seq, _rms(xnext_ref[...], g), 0.0)
    hbuf[pl.ds(0, HALO_ROWS), :] = jnp.broadcast_to(hprev[HALO_ROWS - 1:], (HALO_ROWS, D_MODEL))
    hbuf[pl.ds(HALO_ROWS + rows, HALO_ROWS), :] = jnp.broadcast_to(hnext[:1], (HALO_ROWS, D_MODEL))
    h = hbuf[...].astype(jnp.bfloat16)

    sub = lax.broadcasted_iota(jnp.int32, (HALO_ROWS, FFN_CHUNK), 0)

    def conv(c, half):
        col = half * D_FF + c * FFN_CHUNK
        w = cw_ref[:, pl.ds(col, FFN_CHUNK)]
        acc = cb_ref[:, pl.ds(col, FFN_CHUNK)]
        for j in range(3):
            u = ubufs[c % FFN_U_RING][pl.ds(HALO_ROWS * j, rows), pl.ds(half * FFN_CHUNK, FFN_CHUNK)]
            acc = acc + u * w[j:j + 1]
        return acc

    def up(c):
        buf = ubufs[c % FFN_U_RING]
        for half in range(2):
            col = half * D_FF + c * FFN_CHUNK
            cols = pl.ds(half * FFN_CHUNK, FFN_CHUNK)
            buf[:, cols] = jnp.dot(h, wup_ref[:, pl.ds(col, FFN_CHUNK)], preferred_element_type=f32)
            halo = jnp.broadcast_to(buf[pl.ds(0, 1), cols], (HALO_ROWS, FFN_CHUNK))
            last = pltpu.roll(buf[pl.ds(rows, HALO_ROWS), cols], 1, 0)
            buf[pl.ds(0, HALO_ROWS), cols] = jnp.where(sub == 0, halo, last)
            halo = jnp.broadcast_to(buf[pl.ds(rows + HALO_ROWS, 1), cols], (HALO_ROWS, FFN_CHUNK))
            first = pltpu.roll(buf[pl.ds(HALO_ROWS, HALO_ROWS), cols], HALO_ROWS - 1, 0)
            buf[pl.ds(rows + HALO_ROWS, HALO_ROWS), cols] = jnp.where(sub == HALO_ROWS - 1, halo, first)

    def down(g, n_in_group):
        width = n_in_group * FFN_CHUNK
        w = wdn_ref[pl.ds(g * FFN_DOWN_GROUP * FFN_CHUNK, width), :]
        return jnp.dot(abufs[g][:, pl.ds(0, width)], w, preferred_element_type=f32)

    for c in range(min(FFN_LOOKAHEAD, n_chunks)):
        up(c)
    y = None
    for c in range(n_chunks):
        if c + FFN_LOOKAHEAD < n_chunks:
            up(c + FFN_LOOKAHEAD)
        gate = conv(c, 0)
        val = conv(c, 1)
        gelu = 0.5 * gate * (1.0 + lax.erf(gate * (2.0 ** -0.5)))
        g, j = divmod(c, FFN_DOWN_GROUP)
        abufs[g][:, pl.ds(j * FFN_CHUNK, FFN_CHUNK)] = (gelu * val).astype(jnp.bfloat16)
        if j + 1 == FFN_DOWN_GROUP or c + 1 == n_chunks:
            part = down(g, j + 1)
            y = part if y is None else y + part
    res = _rms(y, gpost_ref[...])
    for l in range(n_slabs):
        xs[l] = xs[l] + res[:, l * LANES:(l + 1) * LANES]
    for l in range(n_slabs):
        for s in range(HALO_ROWS):
            o_ref[pl.ds(s * run, run), pl.ds(l * LANES, LANES)] = xs[l, pl.ds(s, run, stride=HALO_ROWS), :]


def _ffn_call(x1, g_pre, w_up_bf16, conv_w, conv_b, w_down_bf16, g_post, seq):
    n = x1.shape[0]
    rows = FFN_ROWS
    hb = rows // HALO_ROWS
    n_halo_blocks = n // HALO_ROWS
    const = lambda i: (0, 0)
    return pl.pallas_call(
        functools.partial(_ffn_kernel, seq=seq),
        grid=(n // rows,),
        in_specs=[
            pl.BlockSpec((rows, D_MODEL), lambda i: (i, 0)),
            pl.BlockSpec((HALO_ROWS, D_MODEL), lambda i: (jnp.maximum(i * hb - 1, 0), 0)),
            pl.BlockSpec((HALO_ROWS, D_MODEL),
                         lambda i: (jnp.minimum((i + 1) * hb, n_halo_blocks - 1), 0)),
            pl.BlockSpec((1, D_MODEL), const),
            pl.BlockSpec((D_MODEL, 2 * D_FF), const, pipeline_mode=pl.Buffered(1)),
            pl.BlockSpec((3, 2 * D_FF), const),
            pl.BlockSpec((1, 2 * D_FF), const),
            pl.BlockSpec((D_FF, D_MODEL), const, pipeline_mode=pl.Buffered(1)),
            pl.BlockSpec((1, D_MODEL), const),
        ],
        out_specs=pl.BlockSpec((rows, D_MODEL), lambda i: (i, 0)),
        out_shape=jax.ShapeDtypeStruct((n, D_MODEL), jnp.float32),
        scratch_shapes=[
            pltpu.VMEM((D_MODEL // LANES, rows, LANES), jnp.float32),
            pltpu.VMEM((rows + 2 * HALO_ROWS, D_MODEL), jnp.float32),
            *[pltpu.VMEM((rows + 2 * HALO_ROWS, 2 * FFN_CHUNK), jnp.float32)] * FFN_U_RING,
            *[pltpu.VMEM((rows, FFN_DOWN_GROUP * FFN_CHUNK), jnp.bfloat16)]
            * pl.cdiv(D_FF // FFN_CHUNK, FFN_DOWN_GROUP),
        ],
        compiler_params=pltpu.CompilerParams(
            dimension_semantics=("arbitrary",), vmem_limit_bytes=VMEM_LIMIT),
        name="ffn",
    )(x1, x1, x1, g_pre, w_up_bf16, conv_w, conv_b, w_down_bf16, g_post)


def _block_diag(pool_w):
    groups = pool_w.shape[0]
    bd = jnp.zeros((POOL_WIDTH, POOL_WIDTH), pool_w.dtype)
    for g in range(groups):
        lo = g * POOL_GROUP_DIM
        bd = bd.at[lo:lo + POOL_GROUP_DIM, lo:lo + POOL_GROUP_DIM].set(pool_w[g])
    return bd


def _layer(x, params):
    (g_mix_pre, g_mix_post, w_in, pool_bd, pool_scale, g_pool_out, g_attn_out, w_out,
     g_ffn_pre, g_ffn_post, w_up, conv_w, conv_b, w_down) = params
    batch, seq, _ = x.shape
    assert seq % ATTN_ROWS == 0 and seq % PROJ_ROWS == 0 and seq % MIX_ROWS == 0 and seq % FFN_ROWS == 0
    xf = x.reshape(batch * seq, D_MODEL)
    p_in, q, k, v = _proj_call(xf, g_mix_pre, w_in, _rope_tables(seq), seq)
    attn = _attn_call(q, k, v, batch, seq).reshape(N_LANE_GROUPS, batch * seq, LANES)
    x1 = _mix_call(xf, p_in, attn, pool_bd, pool_scale, g_pool_out, g_attn_out, w_out, g_mix_post, seq)
    y = _ffn_call(x1, g_ffn_pre, w_up, conv_w, conv_b, w_down, g_ffn_post, seq)
    return y.reshape(batch, seq, D_MODEL)


def kernel(x_prompt, x_sample, g_mix_pre, g_mix_post, w_in, pool_w, pool_scale, g_pool_out, g_attn_out,
           w_out, g_ffn_pre, g_ffn_post, w_up, conv_w, conv_b, w_down):
    depth = w_in.shape[0]
    bf16 = jnp.bfloat16

    def layer_params(l):
        return (g_mix_pre[l][None], g_mix_post[l][None], w_in[l].astype(bf16),
                _block_diag(pool_w[l]).astype(bf16), pool_scale[l][None], g_pool_out[l][None],
                g_attn_out[l][None], w_out[l].astype(bf16), g_ffn_pre[l][None], g_ffn_post[l][None],
                w_up[l].astype(bf16), conv_w[l], conv_b[l][None], w_down[l].astype(bf16))

    params = [layer_params(l) for l in range(depth)]

    def run(x):
        for p in params:
            x = _layer(x, p)
        return x

    return (run(x_prompt), run(x_sample))
```

```python
import functools
import math

import jax
import jax.numpy as jnp
from jax import lax
from jax.experimental import pallas as pl
from jax.experimental.pallas import tpu as pltpu

D_MODEL = 1024
POOL_WIDTH = 256
POOL_WINDOWS = (2, 4, 8, 16)
POOL_GROUP_DIM = 64
ATTN_WIDTH = 768
HEAD_DIM = 64
DILATIONS = (1, 4, 16)
QKV_LAYOUTS = (4, 16)
BAND_RADIUS = 64
ROPE_THETA = 500000.0
ROPE_DIM = 16
D_FF = 2816
EPS = 1e-6
IN_WIDTH = POOL_WIDTH + 3 * ATTN_WIDTH

LANES = 128
N_LANE_GROUPS = ATTN_WIDTH // LANES
NEG_BIG = -1e30
Q_SCALE = HEAD_DIM ** -0.5 * math.log2(math.e)

PROJ_ROWS = 512
ATTN_ROWS = 2048
Q_CHUNK = 128
K_CHUNK = Q_CHUNK + 2 * BAND_RADIUS
MIX_ROWS = 512
FFN_ROWS = 512
FFN_CHUNK = 256
FFN_LOOKAHEAD = 2
FFN_U_RING = FFN_LOOKAHEAD + 1
FFN_DOWN_GROUP = 4
FFN_TAIL_BLOCKS = 4
HALO_ROWS = 8

VMEM_LIMIT = 56 * 1024 * 1024


def _rms(x, g):
    ms = jnp.mean(x * x, axis=-1, keepdims=True)
    return x * lax.rsqrt(ms + EPS) * g


def _proj_kernel(x_ref, g_ref, w_ref, tab_ref, p_ref, *refs):
    n_lay = len(QKV_LAYOUTS)
    outs = [refs[n_lay * i:n_lay * (i + 1)] for i in range(3)]
    stage = refs[3 * n_lay:]
    rows = PROJ_ROWS
    bf16 = jnp.bfloat16
    h = _rms(x_ref[...], g_ref[...]).astype(bf16)
    proj = jnp.dot(h, w_ref[...], preferred_element_type=jnp.float32)
    p_ref[...] = proj[:, :POOL_WIDTH]
    cos, sin_lo, sin_hi = tab_ref[0], tab_ref[1], tab_ref[2]

    def rope(t):
        return t * cos + pltpu.roll(t, LANES - 8, 1) * sin_lo + pltpu.roll(t, 8, 1) * sin_hi

    def emit(val, out_refs, g, slab_a, slab_b):
        o4, o16 = out_refs
        slab_a[...] = val
        for r_lo in range(4):
            plane = slab_a[pl.ds(r_lo, rows // 4, stride=4), :]
            o4[g, :, pl.ds(r_lo * LANES, LANES)] = plane.astype(bf16)
            slab_b[pl.ds(r_lo * (rows // 4), rows // 4), :] = plane
        for r_lo in range(4):
            for r_hi in range(4):
                piece = slab_b[pl.ds(r_lo * (rows // 4) + r_hi, rows // 16, stride=4), :]
                o16[g, :, pl.ds((4 * r_hi + r_lo) * LANES, LANES)] = piece.astype(bf16)

    for g in range(N_LANE_GROUPS):
        lo = POOL_WIDTH + g * LANES
        emit(rope(proj[:, lo:lo + LANES]) * Q_SCALE, outs[0], g, stage[6 * g], stage[6 * g + 1])
        lo += ATTN_WIDTH
        emit(rope(proj[:, lo:lo + LANES]), outs[1], g, stage[6 * g + 2], stage[6 * g + 3])
        lo += ATTN_WIDTH
        emit(proj[:, lo:lo + LANES], outs[2], g, stage[6 * g + 4], stage[6 * g + 5])


def _rope_tables(seq):
    pos = jnp.arange(seq, dtype=jnp.float32)
    inv_freq = ROPE_THETA ** (-jnp.arange(0, ROPE_DIM, 2, dtype=jnp.float32) / ROPE_DIM)
    half = ROPE_DIM // 2
    dim = jnp.arange(LANES) % HEAD_DIM
    ang = pos[:, None] * inv_freq[dim % half][None, :]
    cos, sin = jnp.cos(ang), jnp.sin(ang)
    first_half, rotated = (dim < half)[None, :], (dim < ROPE_DIM)[None, :]
    c = jnp.where(rotated, cos, 1.0)
    s1 = jnp.where(first_half, -sin, 0.0)
    s2 = jnp.where(rotated & ~first_half, sin, 0.0)
    return jnp.stack([c, s1, s2])


def _proj_call(x, g, w_bf16, tabs, seq):
    n = x.shape[0]
    t = PROJ_ROWS
    blocks_per_seq = seq // t
    qkv_shapes = [jax.ShapeDtypeStruct((N_LANE_GROUPS, n // d, d * LANES), jnp.bfloat16)
                  for d in QKV_LAYOUTS] * 3
    qkv_specs = [pl.BlockSpec((N_LANE_GROUPS, t // d, d * LANES), lambda i: (0, i, 0))
                 for d in QKV_LAYOUTS] * 3
    outs = pl.pallas_call(
        _proj_kernel,
        grid=(n // t,),
        in_specs=[
            pl.BlockSpec((t, D_MODEL), lambda i: (i, 0)),
            pl.BlockSpec((1, D_MODEL), lambda i: (0, 0)),
            pl.BlockSpec((D_MODEL, IN_WIDTH), lambda i: (0, 0)),
            pl.BlockSpec((3, t, LANES), lambda i: (0, i % blocks_per_seq, 0)),
        ],
        out_specs=[pl.BlockSpec((t, POOL_WIDTH), lambda i: (i, 0)), *qkv_specs],
        out_shape=[jax.ShapeDtypeStruct((n, POOL_WIDTH), jnp.float32), *qkv_shapes],
        scratch_shapes=[pltpu.VMEM((t, LANES), jnp.float32)] * (2 * 3 * N_LANE_GROUPS),
        compiler_params=pltpu.CompilerParams(
            dimension_semantics=("arbitrary",), vmem_limit_bytes=VMEM_LIMIT),
        name="proj",
    )(x, g, w_bf16, tabs)
    n_lay = len(QKV_LAYOUTS)
    return outs[0], outs[1:1 + n_lay], outs[1 + n_lay:1 + 2 * n_lay], outs[1 + 2 * n_lay:]


def _attn_kernel(*refs, n_blocks):
    n_lay = len(QKV_LAYOUTS)
    q_refs = dict(zip(QKV_LAYOUTS, refs[:n_lay]))
    k_refs = {d: refs[n_lay + 3 * i:n_lay + 3 * i + 3] for i, d in enumerate(QKV_LAYOUTS)}
    v_refs = {d: refs[4 * n_lay + 3 * i:4 * n_lay + 3 * i + 3] for i, d in enumerate(QKV_LAYOUTS)}
    o_ref, tbuf, mpat, mall, opat, lpat, bias_ref = refs[7 * n_lay:]
    t = pl.program_id(2)
    rows = ATTN_ROWS
    plane = rows // 4
    f32 = jnp.float32
    bf16 = jnp.bfloat16

    @pl.when((pl.program_id(0) == 0) & (pl.program_id(1) == 0) & (t == 0))
    def _():
        row = lax.broadcasted_iota(jnp.int32, (2 * Q_CHUNK, K_CHUNK), 0) % Q_CHUNK
        col = lax.broadcasted_iota(jnp.int32, (2 * Q_CHUNK, K_CHUNK), 1)
        q_rows, k_rows = Q_CHUNK // 4, K_CHUNK // 4
        delta = 4 * (col % k_rows - row % q_rows) + (col // k_rows - row // q_rows)
        families = (
            ((col >= row) & (col <= row + 2 * BAND_RADIUS), col, BAND_RADIUS, BAND_RADIUS + Q_CHUNK),
            ((delta >= 0) & (delta <= 2 * BAND_RADIUS), col % k_rows, BAND_RADIUS // 4,
             (BAND_RADIUS + Q_CHUNK) // 4),
        )
        for fam, (band, key_pos, lo, hi) in enumerate(families):
            for variant in range(4):
                ok = band
                if variant & 1:
                    ok = ok & (key_pos >= lo)
                if variant & 2:
                    ok = ok & (key_pos < hi)
                bias_ref[4 * fam + variant] = jnp.where(ok, 0.0, NEG_BIG).astype(f32)

    lane = lax.broadcasted_iota(jnp.int32, (Q_CHUNK, LANES), 1)
    first_head = lane < HEAD_DIM

    def pick_head(x):
        return jnp.where(first_head, x[:Q_CHUNK], x[Q_CHUNK:])

    def ext_rows(trio, lo, hi, block):
        prev_ref, main_ref, next_ref = trio
        lanes = pl.ds(block * LANES, LANES)
        n_main, n_halo = main_ref.shape[0], prev_ref.shape[0]
        parts = []
        if lo < 0:
            parts.append(prev_ref[pl.ds(n_halo + lo, -lo), lanes])
        parts.append(main_ref[pl.ds(max(lo, 0), min(hi, n_main) - max(lo, 0)), lanes])
        if hi > n_main:
            parts.append(next_ref[pl.ds(0, hi - n_main), lanes])
        return parts

    def rows_of(parts):
        return parts[0] if len(parts) == 1 else jnp.concatenate(parts, axis=0)

    tiles = []
    q_rows, k_rows = Q_CHUNK // 4, K_CHUNK // 4
    for c in range(rows // Q_CHUNK):
        lo = c * q_rows - BAND_RADIUS // 4
        q_parts = [q_refs[4][pl.ds(c * q_rows, q_rows), pl.ds(j * LANES, LANES)] for j in range(4)]
        keys = lambda trios, lo=lo: rows_of(
            [p for j in range(4) for p in ext_rows(trios[4], lo, lo + k_rows, j)])
        acc = [(j * q_rows, pl.ds(j * plane + c * q_rows, q_rows)) for j in range(4)]
        tiles.append((0, q_parts, keys, 1, c == 0, c == rows // Q_CHUNK - 1, acc))
    for r in range(4):
        for c in range(plane // Q_CHUNK):
            lo = c * Q_CHUNK - BAND_RADIUS
            q_parts = [q_refs[4][pl.ds(c * Q_CHUNK, Q_CHUNK), pl.ds(r * LANES, LANES)]]
            keys = lambda trios, lo=lo, r=r: rows_of(ext_rows(trios[4], lo, lo + K_CHUNK, r))
            acc = [(0, pl.ds(r * plane + c * Q_CHUNK, Q_CHUNK))]
            tiles.append((1, q_parts, keys, 0, c == 0, c == plane // Q_CHUNK - 1, acc))
    for r in range(16):
        q_parts = [q_refs[16][:, pl.ds(r * LANES, LANES)]]
        keys = lambda trios, r=r: rows_of(ext_rows(trios[16], -BAND_RADIUS, Q_CHUNK + BAND_RADIUS, r))
        acc = [(0, pl.ds((r % 4) * plane + r // 4, Q_CHUNK, stride=4))]
        tiles.append((2, q_parts, keys, 0, True, True, acc))

    def bias_index(family, at_start, at_end):
        if n_blocks == 1:
            return 4 * family + int(at_start) + 2 * int(at_end)
        first = (t == 0).astype(jnp.int32) if at_start else 0
        last = (t == n_blocks - 1).astype(jnp.int32) if at_end else 0
        return 4 * family + first + 2 * last

    def store_rows(ref, lead, acc, x):
        for start, idx in acc:
            ref[(*lead, idx, slice(None))] = x[start:start + idx.size]

    def load_rows(ref, lead, acc):
        return rows_of([ref[(*lead, idx, slice(None))] for _, idx in acc])

    dyn_zero = jnp.minimum(t, 0)

    for n, (pi, q_parts, keys, family, at_start, at_end, acc) in enumerate(tiles):
        qc = rows_of(q_parts)
        zero = jnp.zeros_like(qc)
        q2 = jnp.concatenate([jnp.where(first_head, qc, zero), jnp.where(first_head, zero, qc)], axis=0)
        s = lax.dot_general(q2, keys(k_refs), (((1,), (1,)), ((), ())), preferred_element_type=f32)
        s = s + bias_ref[bias_index(family, at_start, at_end)]
        tbuf[n + dyn_zero] = s
        m = jnp.broadcast_to(jnp.max(s, axis=-1, keepdims=True), (2 * Q_CHUNK, LANES))
        for h in range(2):
            store_rows(mpat, (pi, h), acc, m[h * Q_CHUNK:(h + 1) * Q_CHUNK])

    pass_rows = 64
    for h in range(2):
        for i in range(rows // pass_rows):
            sl = pl.ds(i * pass_rows, pass_rows)
            mall[h, sl, :] = jnp.maximum(jnp.maximum(mpat[0, h, sl, :], mpat[1, h, sl, :]), mpat[2, h, sl, :])

    ones = jnp.ones((K_CHUNK, LANES), bf16)
    for n, (pi, q_parts, keys, family, at_start, at_end, acc) in enumerate(tiles):
        mb = jnp.concatenate([load_rows(mall, (0,), acc), load_rows(mall, (1,), acc)], axis=0)
        p = jnp.exp2(tbuf[n + dyn_zero] - jnp.concatenate([mb, mb], axis=1)).astype(bf16)
        vc = jnp.concatenate([keys(v_refs), ones], axis=1)
        ol = jnp.dot(p, vc, preferred_element_type=f32)
        store_rows(opat, (pi,), acc, pick_head(ol[:, :LANES]))
        store_rows(lpat, (pi,), acc, pick_head(ol[:, LANES:]))

    for r in range(4):
        for i in range(plane // pass_rows):
            sl = pl.ds(r * plane + i * pass_rows, pass_rows)
            num = opat[0, sl, :] + opat[1, sl, :] + opat[2, sl, :]
            den = lpat[0, sl, :] + lpat[1, sl, :] + lpat[2, sl, :]
            o_ref[pl.ds(r + 4 * i * pass_rows, pass_rows, stride=4), :] = num / den


def _attn_call(q, k, v, batch, seq):
    rows = ATTN_ROWS
    n_blocks = seq // rows
    radius = BAND_RADIUS

    def view(a, d):
        return a.reshape(N_LANE_GROUPS, batch, seq // d, d * LANES)

    operands, in_specs = [], []
    for d, a in zip(QKV_LAYOUTS, q):
        operands.append(view(a, d))
        in_specs.append(pl.BlockSpec((None, None, rows // d, d * LANES), lambda b, g, t: (g, b, t, 0)))
    for arrs in (k, v):
        for d, a in zip(QKV_LAYOUTS, arrs):
            halo_per_block = rows // d // radius
            n_halo = seq // d // radius
            operands += [view(a, d)] * 3
            in_specs += [
                pl.BlockSpec((None, None, radius, d * LANES),
                             lambda b, g, t, hb=halo_per_block: (g, b, jnp.maximum(t * hb - 1, 0), 0)),
                pl.BlockSpec((None, None, rows // d, d * LANES), lambda b, g, t: (g, b, t, 0)),
                pl.BlockSpec((None, None, radius, d * LANES),
                             lambda b, g, t, hb=halo_per_block, nh=n_halo:
                             (g, b, jnp.minimum((t + 1) * hb, nh - 1), 0)),
            ]
    n_tiles = len(DILATIONS) * rows // Q_CHUNK
    return pl.pallas_call(
        functools.partial(_attn_kernel, n_blocks=n_blocks),
        grid=(batch, N_LANE_GROUPS, n_blocks),
        in_specs=in_specs,
        out_specs=pl.BlockSpec((None, None, rows, LANES), lambda b, g, t: (g, b, t, 0)),
        out_shape=jax.ShapeDtypeStruct((N_LANE_GROUPS, batch, seq, LANES), jnp.float32),
        scratch_shapes=[
            pltpu.VMEM((n_tiles, 2 * Q_CHUNK, K_CHUNK), jnp.float32),
            pltpu.VMEM((len(DILATIONS), 2, rows, LANES), jnp.float32),
            pltpu.VMEM((2, rows, LANES), jnp.float32),
            pltpu.VMEM((len(DILATIONS), rows, LANES), jnp.float32),
            pltpu.VMEM((len(DILATIONS), rows, LANES), jnp.float32),
            pltpu.VMEM((8, 2 * Q_CHUNK, K_CHUNK), jnp.float32),
        ],
        compiler_params=pltpu.CompilerParams(
            dimension_semantics=("arbitrary", "arbitrary", "arbitrary"), vmem_limit_bytes=VMEM_LIMIT),
        name="attn",
    )(*operands)


def _mix_kernel(x_ref, p_ref, pprev_ref, pnext_ref, a_ref, pw_ref, ps_ref, gp_ref, ga_ref,
                wo_ref, gpost_ref, o_ref, pbuf, mixed, *, seq):
    rows = MIX_ROWS
    f32 = jnp.float32
    i = pl.program_id(0)
    pos0 = (i * rows) % seq
    zero_rows = jnp.zeros((HALO_ROWS, POOL_WIDTH), f32)
    pbuf[pl.ds(0, HALO_ROWS), :] = jnp.where(pos0 > 0, pprev_ref[...], zero_rows)
    pbuf[pl.ds(HALO_ROWS, rows), :] = p_ref[...]
    pbuf[pl.ds(HALO_ROWS + rows, HALO_ROWS), :] = jnp.where(pos0 + rows < seq, pnext_ref[...], zero_rows)

    pos = pos0 + lax.broadcasted_iota(jnp.int32, (rows, LANES), 0)
    lane = lax.broadcasted_iota(jnp.int32, (rows, LANES), 1)
    means = []
    for half in range(2):
        w_small, w_big = POOL_WINDOWS[2 * half], POOL_WINDOWS[2 * half + 1]
        lanes = pl.ds(half * LANES, LANES)

        def wsum(lo, hi):
            acc = pbuf[pl.ds(HALO_ROWS + lo, rows), lanes]
            for off in range(lo + 1, hi):
                acc = acc + pbuf[pl.ds(HALO_ROWS + off, rows), lanes]
            return acc

        small = wsum(-(w_small // 2), w_small // 2)
        big = small + wsum(-(w_big // 2), -(w_small // 2)) + wsum(w_small // 2, w_big // 2)

        def count(w):
            return (jnp.minimum(pos + w // 2, seq) - jnp.maximum(pos - w // 2, 0)).astype(f32)

        first = lane < POOL_GROUP_DIM
        total = jnp.where(first, small, big)
        cnt = jnp.where(first, count(w_small), count(w_big))
        means.append(total / cnt - pbuf[pl.ds(HALO_ROWS, rows), lanes])
    pooled = jnp.concatenate(means, axis=-1).astype(jnp.bfloat16)
    pool_out = jnp.dot(pooled, pw_ref[...], preferred_element_type=f32) * ps_ref[...]
    mixed[:, pl.ds(0, POOL_WIDTH)] = _rms(pool_out, gp_ref[...]).astype(jnp.bfloat16)

    sq = None
    for g in range(N_LANE_GROUPS):
        a = a_ref[g]
        sq = a * a if sq is None else sq + a * a
    inv = lax.rsqrt(jnp.sum(sq, axis=-1, keepdims=True) / ATTN_WIDTH + EPS)
    for g in range(N_LANE_GROUPS):
        ga = ga_ref[:, pl.ds(g * LANES, LANES)]
        mixed[:, pl.ds(POOL_WIDTH + g * LANES, LANES)] = (a_ref[g] * inv * ga).astype(jnp.bfloat16)

    y = jnp.dot(mixed[...], wo_ref[...], preferred_element_type=f32)
    o_ref[...] = x_ref[...] + _rms(y, gpost_ref[...])


def _mix_call(x, p_in, attn, pool_bd, pool_scale, g_pool, g_attn, w_out_bf16, g_post, seq):
    n = x.shape[0]
    rows = MIX_ROWS
    hb = rows // HALO_ROWS
    n_halo_blocks = n // HALO_ROWS
    const = lambda i: (0, 0)
    return pl.pallas_call(
        functools.partial(_mix_kernel, seq=seq),
        grid=(n // rows,),
        in_specs=[
            pl.BlockSpec((rows, D_MODEL), lambda i: (i, 0)),
            pl.BlockSpec((rows, POOL_WIDTH), lambda i: (i, 0)),
            pl.BlockSpec((HALO_ROWS, POOL_WIDTH), lambda i: (jnp.maximum(i * hb - 1, 0), 0)),
            pl.BlockSpec((HALO_ROWS, POOL_WIDTH),
                         lambda i: (jnp.minimum((i + 1) * hb, n_halo_blocks - 1), 0)),
            pl.BlockSpec((N_LANE_GROUPS, rows, LANES), lambda i: (0, i, 0)),
            pl.BlockSpec((POOL_WIDTH, POOL_WIDTH), const),
            pl.BlockSpec((1, POOL_WIDTH), const),
            pl.BlockSpec((1, POOL_WIDTH), const),
            pl.BlockSpec((1, ATTN_WIDTH), const),
            pl.BlockSpec((D_MODEL, D_MODEL), const),
            pl.BlockSpec((1, D_MODEL), const),
        ],
        out_specs=pl.BlockSpec((rows, D_MODEL), lambda i: (i, 0)),
        out_shape=jax.ShapeDtypeStruct((n, D_MODEL), jnp.float32),
        scratch_shapes=[
            pltpu.VMEM((rows + 2 * HALO_ROWS, POOL_WIDTH), jnp.float32),
            pltpu.VMEM((rows, D_MODEL), jnp.bfloat16),
        ],
        compiler_params=pltpu.CompilerParams(
            dimension_semantics=("arbitrary",), vmem_limit_bytes=VMEM_LIMIT),
        name="mix",
    )(x, p_in, p_in, p_in, attn, pool_bd, pool_scale, g_pool, g_attn, w_out_bf16, g_post)


def _ffn_kernel(x_ref, xprev_ref, xnext_ref, gpre_ref, wup_ref, cw_ref, cb_ref, wdn_ref, gpost_ref,
                o_ref, xs, hbuf, *bufs, seq):
    n_chunks = D_FF // FFN_CHUNK
    ubufs, abufs = bufs[:FFN_U_RING], bufs[FFN_U_RING:]
    rows = FFN_ROWS
    run = rows // HALO_ROWS
    n_slabs = D_MODEL // LANES
    f32 = jnp.float32
    i = pl.program_id(0)
    pos0 = (i * rows) % seq
    g = gpre_ref[...]

    for l in range(n_slabs):
        for s in range(HALO_ROWS):
            xs[l, pl.ds(s, run, stride=HALO_ROWS), :] = x_ref[pl.ds(s * run, run), pl.ds(l * LANES, LANES)]
    sq = None
    for l in range(n_slabs):
        xl = xs[l]
        sq = xl * xl if sq is None else sq + xl * xl
    inv = lax.rsqrt(jnp.sum(sq, axis=-1, keepdims=True) / D_MODEL + EPS)
    for l in range(n_slabs):
        lanes = pl.ds(l * LANES, LANES)
        hbuf[pl.ds(HALO_ROWS, rows), lanes] = xs[l] * inv * gpre_ref[:, lanes]
    hprev = jnp.where(pos0 > 0, _rms(xprev_ref[...], g), 0.0)
    hnext = jnp.where(pos0 + rows < seq, _rms(xnext_ref[...], g), 0.0)
    hbuf[pl.ds(0, HALO_ROWS), :] = jnp.broadcast_to(hprev[HALO_ROWS - 1:], (HALO_ROWS, D_MODEL))
    hbuf[pl.ds(HALO_ROWS + rows, HALO_ROWS), :] = jnp.broadcast_to(hnext[:1], (HALO_ROWS, D_MODEL))
    h = hbuf[...].astype(jnp.bfloat16)

    sub = lax.broadcasted_iota(jnp.int32, (HALO_ROWS, FFN_CHUNK), 0)

    def conv(c, half):
        col = half * D_FF + c * FFN_CHUNK
        w = cw_ref[:, pl.ds(col, FFN_CHUNK)]
        acc = cb_ref[:, pl.ds(col, FFN_CHUNK)]
        for j in range(3):
            u = ubufs[c % FFN_U_RING][pl.ds(HALO_ROWS * j, rows), pl.ds(half * FFN_CHUNK, FFN_CHUNK)]
            acc = acc + u * w[j:j + 1]
        return acc

    def up(c):
        buf = ubufs[c % FFN_U_RING]
        for half in range(2):
            col = half * D_FF + c * FFN_CHUNK
            cols = pl.ds(half * FFN_CHUNK, FFN_CHUNK)
            buf[:, cols] = jnp.dot(h, wup_ref[:, pl.ds(col, FFN_CHUNK)], preferred_element_type=f32)
            halo = jnp.broadcast_to(buf[pl.ds(0, 1), cols], (HALO_ROWS, FFN_CHUNK))
            last = pltpu.roll(buf[pl.ds(rows, HALO_ROWS), cols], 1, 0)
            buf[pl.ds(0, HALO_ROWS), cols] = jnp.where(sub == 0, halo, last)
            halo = jnp.broadcast_to(buf[pl.ds(rows + HALO_ROWS, 1), cols], (HALO_ROWS, FFN_CHUNK))
            first = pltpu.roll(buf[pl.ds(HALO_ROWS, HALO_ROWS), cols], HALO_ROWS - 1, 0)
            buf[pl.ds(rows + HALO_ROWS, HALO_ROWS), cols] = jnp.where(sub == HALO_ROWS - 1, halo, first)

    def down(g, n_in_group):
        width = n_in_group * FFN_CHUNK
        w = wdn_ref[pl.ds(g * FFN_DOWN_GROUP * FFN_CHUNK, width), :]
        return jnp.dot(abufs[g][:, pl.ds(0, width)], w, preferred_element_type=f32)

    for c in range(min(FFN_LOOKAHEAD, n_chunks)):
        up(c)
    y = None
    for c in range(n_chunks):
        if c + FFN_LOOKAHEAD < n_chunks:
            up(c + FFN_LOOKAHEAD)
        gate = conv(c, 0)
        val = conv(c, 1)
        gelu = 0.5 * gate * (1.0 + lax.erf(gate * (2.0 ** -0.5)))
        g, j = divmod(c, FFN_DOWN_GROUP)
        abufs[g][:, pl.ds(j * FFN_CHUNK, FFN_CHUNK)] = (gelu * val).astype(jnp.bfloat16)
        if j + 1 == FFN_DOWN_GROUP and c + 1 < n_chunks:
            part = down(g, j + 1)
            y = part if y is None else y + part

    g, n_last = divmod(n_chunks - 1, FFN_DOWN_GROUP)
    width = (n_last + 1) * FFN_CHUNK
    w_last = wdn_ref[pl.ds(g * FFN_DOWN_GROUP * FFN_CHUNK, width), :]
    blk = rows // FFN_TAIL_BLOCKS
    blk_run = run // FFN_TAIL_BLOCKS
    for b in range(FFN_TAIL_BLOCKS):
        rs = pl.ds(b * blk, blk)
        yb = jnp.dot(abufs[g][rs, pl.ds(0, width)], w_last, preferred_element_type=f32)
        if y is not None:
            yb = yb + y[b * blk:(b + 1) * blk]
        res = _rms(yb, gpost_ref[...])
        for l in range(n_slabs):
            xs[l, rs, :] = xs[l, rs, :] + res[:, l * LANES:(l + 1) * LANES]
        for l in range(n_slabs):
            for s in range(HALO_ROWS):
                o_ref[pl.ds(s * run + b * blk_run, blk_run), pl.ds(l * LANES, LANES)] = (
                    xs[l, pl.ds(b * blk + s, blk_run, stride=HALO_ROWS), :])


def _ffn_call(x1, g_pre, w_up_bf16, conv_w, conv_b, w_down_bf16, g_post, seq):
    n = x1.shape[0]
    rows = FFN_ROWS
    hb = rows // HALO_ROWS
    n_halo_blocks = n // HALO_ROWS
    const = lambda i: (0, 0)
    return pl.pallas_call(
        functools.partial(_ffn_kernel, seq=seq),
        grid=(n // rows,),
        in_specs=[
            pl.BlockSpec((rows, D_MODEL), lambda i: (i, 0)),
            pl.BlockSpec((HALO_ROWS, D_MODEL), lambda i: (jnp.maximum(i * hb - 1, 0), 0)),
            pl.BlockSpec((HALO_ROWS, D_MODEL),
                         lambda i: (jnp.minimum((i + 1) * hb, n_halo_blocks - 1), 0)),
            pl.BlockSpec((1, D_MODEL), const),
            pl.BlockSpec((D_MODEL, 2 * D_FF), const, pipeline_mode=pl.Buffered(1)),
            pl.BlockSpec((3, 2 * D_FF), const),
            pl.BlockSpec((1, 2 * D_FF), const),
            pl.BlockSpec((D_FF, D_MODEL), const, pipeline_mode=pl.Buffered(1)),
            pl.BlockSpec((1, D_MODEL), const),
        ],
        out_specs=pl.BlockSpec((rows, D_MODEL), lambda i: (i, 0)),
        out_shape=jax.ShapeDtypeStruct((n, D_MODEL), jnp.float32),
        scratch_shapes=[
            pltpu.VMEM((D_MODEL // LANES, rows, LANES), jnp.float32),
            pltpu.VMEM((rows + 2 * HALO_ROWS, D_MODEL), jnp.float32),
            *[pltpu.VMEM((rows + 2 * HALO_ROWS, 2 * FFN_CHUNK), jnp.float32)] * FFN_U_RING,
            *[pltpu.VMEM((rows, FFN_DOWN_GROUP * FFN_CHUNK), jnp.bfloat16)]
            * pl.cdiv(D_FF // FFN_CHUNK, FFN_DOWN_GROUP),
        ],
        compiler_params=pltpu.CompilerParams(
            dimension_semantics=("arbitrary",), vmem_limit_bytes=VMEM_LIMIT),
        name="ffn",
    )(x1, x1, x1, g_pre, w_up_bf16, conv_w, conv_b, w_down_bf16, g_post)


def _block_diag(pool_w):
    groups = pool_w.shape[0]
    bd = jnp.zeros((POOL_WIDTH, POOL_WIDTH), pool_w.dtype)
    for g in range(groups):
        lo = g * POOL_GROUP_DIM
        bd = bd.at[lo:lo + POOL_GROUP_DIM, lo:lo + POOL_GROUP_DIM].set(pool_w[g])
    return bd


def _layer(x, params):
    (g_mix_pre, g_mix_post, w_in, pool_bd, pool_scale, g_pool_out, g_attn_out, w_out,
     g_ffn_pre, g_ffn_post, w_up, conv_w, conv_b, w_down) = params
    batch, seq, _ = x.shape
    assert seq % ATTN_ROWS == 0 and seq % PROJ_ROWS == 0 and seq % MIX_ROWS == 0 and seq % FFN_ROWS == 0
    xf = x.reshape(batch * seq, D_MODEL)
    p_in, q, k, v = _proj_call(xf, g_mix_pre, w_in, _rope_tables(seq), seq)
    attn = _attn_call(q, k, v, batch, seq).reshape(N_LANE_GROUPS, batch * seq, LANES)
    x1 = _mix_call(xf, p_in, attn, pool_bd, pool_scale, g_pool_out, g_attn_out, w_out, g_mix_post, seq)
    y = _ffn_call(x1, g_ffn_pre, w_up, conv_w, conv_b, w_down, g_ffn_post, seq)
    return y.reshape(batch, seq, D_MODEL)


def kernel(x_prompt, x_sample, g_mix_pre, g_mix_post, w_in, pool_w, pool_scale, g_pool_out, g_attn_out,
           w_out, g_ffn_pre, g_ffn_post, w_up, conv_w, conv_b, w_down):
    depth = w_in.shape[0]
    bf16 = jnp.bfloat16

    def layer_params(l):
        return (g_mix_pre[l][None], g_mix_post[l][None], w_in[l].astype(bf16),
                _block_diag(pool_w[l]).astype(bf16), pool_scale[l][None], g_pool_out[l][None],
                g_attn_out[l][None], w_out[l].astype(bf16), g_ffn_pre[l][None], g_ffn_post[l][None],
                w_up[l].astype(bf16), conv_w[l], conv_b[l][None], w_down[l].astype(bf16))

    params = [layer_params(l) for l in range(depth)]

    def run(x):
        for p in params:
            x = _layer(x, p)
        return x

    return (run(x_prompt), run(x_sample))
```

```python
import functools
import math

import jax
import jax.numpy as jnp
from jax import lax
from jax.experimental import pallas as pl
from jax.experimental.pallas import tpu as pltpu

D_MODEL = 1024
POOL_WIDTH = 256
POOL_WINDOWS = (2, 4, 8, 16)
POOL_GROUP_DIM = 64
ATTN_WIDTH = 768
HEAD_DIM = 64
DILATIONS = (1, 4, 16)
QKV_LAYOUTS = (4, 16)
BAND_RADIUS = 64
ROPE_THETA = 500000.0
ROPE_DIM = 16
D_FF = 2816
EPS = 1e-6
IN_WIDTH = POOL_WIDTH + 3 * ATTN_WIDTH

LANES = 128
N_LANE_GROUPS = ATTN_WIDTH // LANES
NEG_BIG = -1e30
Q_SCALE = HEAD_DIM ** -0.5 * math.log2(math.e)

PROJ_ROWS = 512
ATTN_ROWS = 2048
Q_CHUNK = 128
K_CHUNK = Q_CHUNK + 2 * BAND_RADIUS
MIX_ROWS = 512
FFN_ROWS = 512
FFN_CHUNK = 256
FFN_LOOKAHEAD = 2
FFN_U_RING = FFN_LOOKAHEAD + 1
FFN_DOWN_GROUP = 4
FFN_TAIL_BLOCKS = 4
HALO_ROWS = 8

VMEM_LIMIT = 56 * 1024 * 1024


def _rms(x, g):
    ms = jnp.mean(x * x, axis=-1, keepdims=True)
    return x * lax.rsqrt(ms + EPS) * g


def _proj_kernel(x_ref, g_ref, w_ref, tab_ref, p_ref, *refs):
    n_lay = len(QKV_LAYOUTS)
    outs = [refs[n_lay * i:n_lay * (i + 1)] for i in range(3)]
    stage = refs[3 * n_lay:]
    rows = PROJ_ROWS
    bf16 = jnp.bfloat16
    h = _rms(x_ref[...], g_ref[...]).astype(bf16)
    proj = jnp.dot(h, w_ref[...], preferred_element_type=jnp.float32)
    p_ref[...] = proj[:, :POOL_WIDTH]
    cos, sin_lo, sin_hi = tab_ref[0], tab_ref[1], tab_ref[2]

    def rope(t):
        return t * cos + pltpu.roll(t, LANES - 8, 1) * sin_lo + pltpu.roll(t, 8, 1) * sin_hi

    def emit(val, out_refs, g, slab_a, slab_b):
        o4, o16 = out_refs
        slab_a[...] = val
        for r_lo in range(4):
            plane = slab_a[pl.ds(r_lo, rows // 4, stride=4), :]
            o4[g, :, pl.ds(r_lo * LANES, LANES)] = plane.astype(bf16)
            slab_b[pl.ds(r_lo * (rows // 4), rows // 4), :] = plane
        for r_lo in range(4):
            for r_hi in range(4):
                piece = slab_b[pl.ds(r_lo * (rows // 4) + r_hi, rows // 16, stride=4), :]
                o16[g, :, pl.ds((4 * r_hi + r_lo) * LANES, LANES)] = piece.astype(bf16)

    for g in range(N_LANE_GROUPS):
        lo = POOL_WIDTH + g * LANES
        emit(rope(proj[:, lo:lo + LANES]) * Q_SCALE, outs[0], g, stage[6 * g], stage[6 * g + 1])
        lo += ATTN_WIDTH
        emit(rope(proj[:, lo:lo + LANES]), outs[1], g, stage[6 * g + 2], stage[6 * g + 3])
        lo += ATTN_WIDTH
        emit(proj[:, lo:lo + LANES], outs[2], g, stage[6 * g + 4], stage[6 * g + 5])


def _rope_tables(seq):
    pos = jnp.arange(seq, dtype=jnp.float32)
    inv_freq = ROPE_THETA ** (-jnp.arange(0, ROPE_DIM, 2, dtype=jnp.float32) / ROPE_DIM)
    half = ROPE_DIM // 2
    dim = jnp.arange(LANES) % HEAD_DIM
    ang = pos[:, None] * inv_freq[dim % half][None, :]
    cos, sin = jnp.cos(ang), jnp.sin(ang)
    first_half, rotated = (dim < half)[None, :], (dim < ROPE_DIM)[None, :]
    c = jnp.where(rotated, cos, 1.0)
    s1 = jnp.where(first_half, -sin, 0.0)
    s2 = jnp.where(rotated & ~first_half, sin, 0.0)
    return jnp.stack([c, s1, s2])


def _proj_call(x, g, w_bf16, tabs, seq):
    n = x.shape[0]
    t = PROJ_ROWS
    blocks_per_seq = seq // t
    qkv_shapes = [jax.ShapeDtypeStruct((N_LANE_GROUPS, n // d, d * LANES), jnp.bfloat16)
                  for d in QKV_LAYOUTS] * 3
    qkv_specs = [pl.BlockSpec((N_LANE_GROUPS, t // d, d * LANES), lambda i: (0, i, 0))
                 for d in QKV_LAYOUTS] * 3
    outs = pl.pallas_call(
        _proj_kernel,
        grid=(n // t,),
        in_specs=[
            pl.BlockSpec((t, D_MODEL), lambda i: (i, 0)),
            pl.BlockSpec((1, D_MODEL), lambda i: (0, 0)),
            pl.BlockSpec((D_MODEL, IN_WIDTH), lambda i: (0, 0)),
            pl.BlockSpec((3, t, LANES), lambda i: (0, i % blocks_per_seq, 0)),
        ],
        out_specs=[pl.BlockSpec((t, POOL_WIDTH), lambda i: (i, 0)), *qkv_specs],
        out_shape=[jax.ShapeDtypeStruct((n, POOL_WIDTH), jnp.float32), *qkv_shapes],
        scratch_shapes=[pltpu.VMEM((t, LANES), jnp.float32)] * (2 * 3 * N_LANE_GROUPS),
        compiler_params=pltpu.CompilerParams(
            dimension_semantics=("arbitrary",), vmem_limit_bytes=VMEM_LIMIT),
        name="proj",
    )(x, g, w_bf16, tabs)
    n_lay = len(QKV_LAYOUTS)
    return outs[0], outs[1:1 + n_lay], outs[1 + n_lay:1 + 2 * n_lay], outs[1 + 2 * n_lay:]


def _attn_kernel(*refs, n_blocks):
    n_lay = len(QKV_LAYOUTS)
    q_refs = dict(zip(QKV_LAYOUTS, refs[:n_lay]))
    k_refs = {d: refs[n_lay + 3 * i:n_lay + 3 * i + 3] for i, d in enumerate(QKV_LAYOUTS)}
    v_refs = {d: refs[4 * n_lay + 3 * i:4 * n_lay + 3 * i + 3] for i, d in enumerate(QKV_LAYOUTS)}
    o_ref, tbuf, mpat, mall, opat, lpat, bias_ref = refs[7 * n_lay:]
    t = pl.program_id(2)
    rows = ATTN_ROWS
    plane = rows // 4
    f32 = jnp.float32
    bf16 = jnp.bfloat16

    @pl.when((pl.program_id(0) == 0) & (pl.program_id(1) == 0) & (t == 0))
    def _():
        row = lax.broadcasted_iota(jnp.int32, (2 * Q_CHUNK, K_CHUNK), 0) % Q_CHUNK
        col = lax.broadcasted_iota(jnp.int32, (2 * Q_CHUNK, K_CHUNK), 1)
        q_rows, k_rows = Q_CHUNK // 4, K_CHUNK // 4
        delta = 4 * (col % k_rows - row % q_rows) + (col // k_rows - row // q_rows)
        families = (
            ((col >= row) & (col <= row + 2 * BAND_RADIUS), col, BAND_RADIUS, BAND_RADIUS + Q_CHUNK),
            ((delta >= 0) & (delta <= 2 * BAND_RADIUS), col % k_rows, BAND_RADIUS // 4,
             (BAND_RADIUS + Q_CHUNK) // 4),
        )
        for fam, (band, key_pos, lo, hi) in enumerate(families):
            for variant in range(4):
                ok = band
                if variant & 1:
                    ok = ok & (key_pos >= lo)
                if variant & 2:
                    ok = ok & (key_pos < hi)
                bias_ref[4 * fam + variant] = jnp.where(ok, 0.0, NEG_BIG).astype(f32)

    lane = lax.broadcasted_iota(jnp.int32, (Q_CHUNK, LANES), 1)
    first_head = lane < HEAD_DIM

    def pick_head(x):
        return jnp.where(first_head, x[:Q_CHUNK], x[Q_CHUNK:])

    def ext_rows(trio, lo, hi, block):
        prev_ref, main_ref, next_ref = trio
        lanes = pl.ds(block * LANES, LANES)
        n_main, n_halo = main_ref.shape[0], prev_ref.shape[0]
        parts = []
        if lo < 0:
            parts.append(prev_ref[pl.ds(n_halo + lo, -lo), lanes])
        parts.append(main_ref[pl.ds(max(lo, 0), min(hi, n_main) - max(lo, 0)), lanes])
        if hi > n_main:
            parts.append(next_ref[pl.ds(0, hi - n_main), lanes])
        return parts

    def rows_of(parts):
        return parts[0] if len(parts) == 1 else jnp.concatenate(parts, axis=0)

    tiles = []
    q_rows, k_rows = Q_CHUNK // 4, K_CHUNK // 4
    for c in range(rows // Q_CHUNK):
        lo = c * q_rows - BAND_RADIUS // 4
        q_parts = [q_refs[4][pl.ds(c * q_rows, q_rows), pl.ds(j * LANES, LANES)] for j in range(4)]
        keys = lambda trios, lo=lo: rows_of(
            [p for j in range(4) for p in ext_rows(trios[4], lo, lo + k_rows, j)])
        acc = [(j * q_rows, pl.ds(j * plane + c * q_rows, q_rows)) for j in range(4)]
        tiles.append((0, q_parts, keys, 1, c == 0, c == rows // Q_CHUNK - 1, acc))
    for r in range(4):
        for c in range(plane // Q_CHUNK):
            lo = c * Q_CHUNK - BAND_RADIUS
            q_parts = [q_refs[4][pl.ds(c * Q_CHUNK, Q_CHUNK), pl.ds(r * LANES, LANES)]]
            keys = lambda trios, lo=lo, r=r: rows_of(ext_rows(trios[4], lo, lo + K_CHUNK, r))
            acc = [(0, pl.ds(r * plane + c * Q_CHUNK, Q_CHUNK))]
            tiles.append((1, q_parts, keys, 0, c == 0, c == plane // Q_CHUNK - 1, acc))
    for r in range(16):
        q_parts = [q_refs[16][:, pl.ds(r * LANES, LANES)]]
        keys = lambda trios, r=r: rows_of(ext_rows(trios[16], -BAND_RADIUS, Q_CHUNK + BAND_RADIUS, r))
        acc = [(0, pl.ds((r % 4) * plane + r // 4, Q_CHUNK, stride=4))]
        tiles.append((2, q_parts, keys, 0, True, True, acc))

    def bias_index(family, at_start, at_end):
        if n_blocks == 1:
            return 4 * family + int(at_start) + 2 * int(at_end)
        first = (t == 0).astype(jnp.int32) if at_start else 0
        last = (t == n_blocks - 1).astype(jnp.int32) if at_end else 0
        return 4 * family + first + 2 * last

    def store_rows(ref, lead, acc, x):
        for start, idx in acc:
            ref[(*lead, idx, slice(None))] = x[start:start + idx.size]

    def load_rows(ref, lead, acc):
        return rows_of([ref[(*lead, idx, slice(None))] for _, idx in acc])

    dyn_zero = jnp.minimum(t, 0)

    for n, (pi, q_parts, keys, family, at_start, at_end, acc) in sorted(
            enumerate(tiles), key=lambda item: -item[1][0]):
        qc = rows_of(q_parts)
        zero = jnp.zeros_like(qc)
        q2 = jnp.concatenate([jnp.where(first_head, qc, zero), jnp.where(first_head, zero, qc)], axis=0)
        s = lax.dot_general(q2, keys(k_refs), (((1,), (1,)), ((), ())), preferred_element_type=f32)
        s = s + bias_ref[bias_index(family, at_start, at_end)]
        tbuf[n + dyn_zero] = s
        m = jnp.broadcast_to(jnp.max(s, axis=-1, keepdims=True), (2 * Q_CHUNK, LANES))
        for h in range(2):
            store_rows(mpat, (pi, h), acc, m[h * Q_CHUNK:(h + 1) * Q_CHUNK])

    pass_rows = 64
    for i in range(plane // q_rows):
        for r in range(4):
            sl = pl.ds(r * plane + i * q_rows, q_rows)
            for h in range(2):
                mall[h, sl, :] = jnp.maximum(jnp.maximum(mpat[0, h, sl, :], mpat[1, h, sl, :]),
                                             mpat[2, h, sl, :])

    ones = jnp.ones((K_CHUNK, LANES), bf16)
    for n, (pi, q_parts, keys, family, at_start, at_end, acc) in enumerate(tiles):
        mb = jnp.concatenate([load_rows(mall, (0,), acc), load_rows(mall, (1,), acc)], axis=0)
        p = jnp.exp2(tbuf[n + dyn_zero] - jnp.concatenate([mb, mb], axis=1)).astype(bf16)
        vc = jnp.concatenate([keys(v_refs), ones], axis=1)
        ol = jnp.dot(p, vc, preferred_element_type=f32)
        store_rows(opat, (pi,), acc, pick_head(ol[:, :LANES]))
        store_rows(lpat, (pi,), acc, pick_head(ol[:, LANES:]))

    for r in range(4):
        for i in range(plane // pass_rows):
            sl = pl.ds(r * plane + i * pass_rows, pass_rows)
            num = opat[0, sl, :] + opat[1, sl, :] + opat[2, sl, :]
            den = lpat[0, sl, :] + lpat[1, sl, :] + lpat[2, sl, :]
            o_ref[pl.ds(r + 4 * i * pass_rows, pass_rows, stride=4), :] = num / den


def _attn_call(q, k, v, batch, seq):
    rows = ATTN_ROWS
    n_blocks = seq // rows
    radius = BAND_RADIUS

    def view(a, d):
        return a.reshape(N_LANE_GROUPS, batch, seq // d, d * LANES)

    operands, in_specs = [], []
    for d, a in zip(QKV_LAYOUTS, q):
        operands.append(view(a, d))
        in_specs.append(pl.BlockSpec((None, None, rows // d, d * LANES), lambda b, g, t: (g, b, t, 0)))
    for arrs in (k, v):
        for d, a in zip(QKV_LAYOUTS, arrs):
            halo_per_block = rows // d // radius
            n_halo = seq // d // radius
            operands += [view(a, d)] * 3
            in_specs += [
                pl.BlockSpec((None, None, radius, d * LANES),
                             lambda b, g, t, hb=halo_per_block: (g, b, jnp.maximum(t * hb - 1, 0), 0)),
                pl.BlockSpec((None, None, rows // d, d * LANES), lambda b, g, t: (g, b, t, 0)),
                pl.BlockSpec((None, None, radius, d * LANES),
                             lambda b, g, t, hb=halo_per_block, nh=n_halo:
                             (g, b, jnp.minimum((t + 1) * hb, nh - 1), 0)),
            ]
    n_tiles = len(DILATIONS) * rows // Q_CHUNK
    return pl.pallas_call(
        functools.partial(_attn_kernel, n_blocks=n_blocks),
        grid=(batch, N_LANE_GROUPS, n_blocks),
        in_specs=in_specs,
        out_specs=pl.BlockSpec((None, None, rows, LANES), lambda b, g, t: (g, b, t, 0)),
        out_shape=jax.ShapeDtypeStruct((N_LANE_GROUPS, batch, seq, LANES), jnp.float32),
        scratch_shapes=[
            pltpu.VMEM((n_tiles, 2 * Q_CHUNK, K_CHUNK), jnp.float32),
            pltpu.VMEM((len(DILATIONS), 2, rows, LANES), jnp.float32),
            pltpu.VMEM((2, rows, LANES), jnp.float32),
            pltpu.VMEM((len(DILATIONS), rows, LANES), jnp.float32),
            pltpu.VMEM((len(DILATIONS), rows, LANES), jnp.float32),
            pltpu.VMEM((8, 2 * Q_CHUNK, K_CHUNK), jnp.float32),
        ],
        compiler_params=pltpu.CompilerParams(
            dimension_semantics=("arbitrary", "arbitrary", "arbitrary"), vmem_limit_bytes=VMEM_LIMIT),
        name="attn",
    )(*operands)


def _mix_kernel(x_ref, p_ref, pprev_ref, pnext_ref, a_ref, pw_ref, ps_ref, gp_ref, ga_ref,
                wo_ref, gpost_ref, o_ref, pbuf, mixed, *, seq):
    rows = MIX_ROWS
    f32 = jnp.float32
    i = pl.program_id(0)
    pos0 = (i * rows) % seq
    zero_rows = jnp.zeros((HALO_ROWS, POOL_WIDTH), f32)
    pbuf[pl.ds(0, HALO_ROWS), :] = jnp.where(pos0 > 0, pprev_ref[...], zero_rows)
    pbuf[pl.ds(HALO_ROWS, rows), :] = p_ref[...]
    pbuf[pl.ds(HALO_ROWS + rows, HALO_ROWS), :] = jnp.where(pos0 + rows < seq, pnext_ref[...], zero_rows)

    pos = pos0 + lax.broadcasted_iota(jnp.int32, (rows, LANES), 0)
    lane = lax.broadcasted_iota(jnp.int32, (rows, LANES), 1)
    means = []
    for half in range(2):
        w_small, w_big = POOL_WINDOWS[2 * half], POOL_WINDOWS[2 * half + 1]
        lanes = pl.ds(half * LANES, LANES)

        def wsum(lo, hi):
            acc = pbuf[pl.ds(HALO_ROWS + lo, rows), lanes]
            for off in range(lo + 1, hi):
                acc = acc + pbuf[pl.ds(HALO_ROWS + off, rows), lanes]
            return acc

        if half == 0:
            small = wsum(-(w_small // 2), w_small // 2)
            big = small + wsum(-(w_big // 2), -(w_small // 2)) + wsum(w_small // 2, w_big // 2)
        else:
            n_buf = rows + 2 * HALO_ROWS
            a = pbuf[:, lanes]
            for step in (1, 2, 4):
                a = a + pltpu.roll(a, n_buf - step, 0)
            small = pltpu.roll(a, w_small // 2, 0)[HALO_ROWS:HALO_ROWS + rows]
            big = (a + pltpu.roll(a, n_buf - w_small, 0))[:rows]

        def count(w):
            return (jnp.minimum(pos + w // 2, seq) - jnp.maximum(pos - w // 2, 0)).astype(f32)

        first = lane < POOL_GROUP_DIM
        total = jnp.where(first, small, big)
        cnt = jnp.where(first, count(w_small), count(w_big))
        means.append(total / cnt - pbuf[pl.ds(HALO_ROWS, rows), lanes])
    pooled = jnp.concatenate(means, axis=-1).astype(jnp.bfloat16)
    pool_out = jnp.dot(pooled, pw_ref[...], preferred_element_type=f32) * ps_ref[...]
    mixed[:, pl.ds(0, POOL_WIDTH)] = _rms(pool_out, gp_ref[...]).astype(jnp.bfloat16)

    sq = None
    for g in range(N_LANE_GROUPS):
        a = a_ref[g]
        sq = a * a if sq is None else sq + a * a
    inv = lax.rsqrt(jnp.sum(sq, axis=-1, keepdims=True) / ATTN_WIDTH + EPS)
    for g in range(N_LANE_GROUPS):
        ga = ga_ref[:, pl.ds(g * LANES, LANES)]
        mixed[:, pl.ds(POOL_WIDTH + g * LANES, LANES)] = (a_ref[g] * inv * ga).astype(jnp.bfloat16)

    y = jnp.dot(mixed[...], wo_ref[...], preferred_element_type=f32)
    o_ref[...] = x_ref[...] + _rms(y, gpost_ref[...])


def _mix_call(x, p_in, attn, pool_bd, pool_scale, g_pool, g_attn, w_out_bf16, g_post, seq):
    n = x.shape[0]
    rows = MIX_ROWS
    hb = rows // HALO_ROWS
    n_halo_blocks = n // HALO_ROWS
    const = lambda i: (0, 0)
    return pl.pallas_call(
        functools.partial(_mix_kernel, seq=seq),
        grid=(n // rows,),
        in_specs=[
            pl.BlockSpec((rows, D_MODEL), lambda i: (i, 0)),
            pl.BlockSpec((rows, POOL_WIDTH), lambda i: (i, 0)),
            pl.BlockSpec((HALO_ROWS, POOL_WIDTH), lambda i: (jnp.maximum(i * hb - 1, 0), 0)),
            pl.BlockSpec((HALO_ROWS, POOL_WIDTH),
                         lambda i: (jnp.minimum((i + 1) * hb, n_halo_blocks - 1), 0)),
            pl.BlockSpec((N_LANE_GROUPS, rows, LANES), lambda i: (0, i, 0)),
            pl.BlockSpec((POOL_WIDTH, POOL_WIDTH), const),
            pl.BlockSpec((1, POOL_WIDTH), const),
            pl.BlockSpec((1, POOL_WIDTH), const),
            pl.BlockSpec((1, ATTN_WIDTH), const),
            pl.BlockSpec((D_MODEL, D_MODEL), const),
            pl.BlockSpec((1, D_MODEL), const),
        ],
        out_specs=pl.BlockSpec((rows, D_MODEL), lambda i: (i, 0)),
        out_shape=jax.ShapeDtypeStruct((n, D_MODEL), jnp.float32),
        scratch_shapes=[
            pltpu.VMEM((rows + 2 * HALO_ROWS, POOL_WIDTH), jnp.float32),
            pltpu.VMEM((rows, D_MODEL), jnp.bfloat16),
        ],
        compiler_params=pltpu.CompilerParams(
            dimension_semantics=("arbitrary",), vmem_limit_bytes=VMEM_LIMIT),
        name="mix",
    )(x, p_in, p_in, p_in, attn, pool_bd, pool_scale, g_pool, g_attn, w_out_bf16, g_post)


def _ffn_kernel(x_ref, xprev_ref, xnext_ref, gpre_ref, wup_ref, cw_ref, cb_ref, wdn_ref, gpost_ref,
                o_ref, xs, hbuf, *bufs, seq):
    n_chunks = D_FF // FFN_CHUNK
    ubufs, abufs = bufs[:FFN_U_RING], bufs[FFN_U_RING:]
    rows = FFN_ROWS
    run = rows // HALO_ROWS
    n_slabs = D_MODEL // LANES
    f32 = jnp.float32
    i = pl.program_id(0)
    pos0 = (i * rows) % seq
    g = gpre_ref[...]

    for l in range(n_slabs):
        for s in range(HALO_ROWS):
            xs[l, pl.ds(s, run, stride=HALO_ROWS), :] = x_ref[pl.ds(s * run, run), pl.ds(l * LANES, LANES)]
    sq = None
    for l in range(n_slabs):
        xl = xs[l]
        sq = xl * xl if sq is None else sq + xl * xl
    inv = lax.rsqrt(jnp.sum(sq, axis=-1, keepdims=True) / D_MODEL + EPS)
    for l in range(n_slabs):
        lanes = pl.ds(l * LANES, LANES)
        hbuf[pl.ds(HALO_ROWS, rows), lanes] = xs[l] * inv * gpre_ref[:, lanes]
    hprev = jnp.where(pos0 > 0, _rms(xprev_ref[...], g), 0.0)
    hnext = jnp.where(pos0 + rows < seq, _rms(xnext_ref[...], g), 0.0)
    hbuf[pl.ds(0, HALO_ROWS), :] = jnp.broadcast_to(hprev[HALO_ROWS - 1:], (HALO_ROWS, D_MODEL))
    hbuf[pl.ds(HALO_ROWS + rows, HALO_ROWS), :] = jnp.broadcast_to(hnext[:1], (HALO_ROWS, D_MODEL))
    h = hbuf[...].astype(jnp.bfloat16)

    sub = lax.broadcasted_iota(jnp.int32, (HALO_ROWS, FFN_CHUNK), 0)

    def conv(c, half):
        col = half * D_FF + c * FFN_CHUNK
        w = cw_ref[:, pl.ds(col, FFN_CHUNK)]
        acc = cb_ref[:, pl.ds(col, FFN_CHUNK)]
        for j in range(3):
            u = ubufs[c % FFN_U_RING][pl.ds(HALO_ROWS * j, rows), pl.ds(half * FFN_CHUNK, FFN_CHUNK)]
            acc = acc + u * w[j:j + 1]
        return acc

    def up(c):
        buf = ubufs[c % FFN_U_RING]
        for half in range(2):
            col = half * D_FF + c * FFN_CHUNK
            cols = pl.ds(half * FFN_CHUNK, FFN_CHUNK)
            buf[:, cols] = jnp.dot(h, wup_ref[:, pl.ds(col, FFN_CHUNK)], preferred_element_type=f32)
            halo = jnp.broadcast_to(buf[pl.ds(0, 1), cols], (HALO_ROWS, FFN_CHUNK))
            last = pltpu.roll(buf[pl.ds(rows, HALO_ROWS), cols], 1, 0)
            buf[pl.ds(0, HALO_ROWS), cols] = jnp.where(sub == 0, halo, last)
            halo = jnp.broadcast_to(buf[pl.ds(rows + HALO_ROWS, 1), cols], (HALO_ROWS, FFN_CHUNK))
            first = pltpu.roll(buf[pl.ds(HALO_ROWS, HALO_ROWS), cols], HALO_ROWS - 1, 0)
            buf[pl.ds(rows + HALO_ROWS, HALO_ROWS), cols] = jnp.where(sub == HALO_ROWS - 1, halo, first)

    def down(g, n_in_group):
        width = n_in_group * FFN_CHUNK
        w = wdn_ref[pl.ds(g * FFN_DOWN_GROUP * FFN_CHUNK, width), :]
        return jnp.dot(abufs[g][:, pl.ds(0, width)], w, preferred_element_type=f32)

    for c in range(min(FFN_LOOKAHEAD, n_chunks)):
        up(c)
    y = None
    for c in range(n_chunks):
        if c + FFN_LOOKAHEAD < n_chunks:
            up(c + FFN_LOOKAHEAD)
        gate = conv(c, 0)
        val = conv(c, 1)
        gelu = 0.5 * gate * (1.0 + lax.erf(gate * (2.0 ** -0.5)))
        g, j = divmod(c, FFN_DOWN_GROUP)
        abufs[g][:, pl.ds(j * FFN_CHUNK, FFN_CHUNK)] = (gelu * val).astype(jnp.bfloat16)
        if j + 1 == FFN_DOWN_GROUP and c + 1 < n_chunks:
            part = down(g, j + 1)
            y = part if y is None else y + part

    g, n_last = divmod(n_chunks - 1, FFN_DOWN_GROUP)
    width = (n_last + 1) * FFN_CHUNK
    w_last = wdn_ref[pl.ds(g * FFN_DOWN_GROUP * FFN_CHUNK, width), :]
    blk = rows // FFN_TAIL_BLOCKS
    blk_run = run // FFN_TAIL_BLOCKS
    for b in range(FFN_TAIL_BLOCKS):
        rs = pl.ds(b * blk, blk)
        yb = jnp.dot(abufs[g][rs, pl.ds(0, width)], w_last, preferred_element_type=f32)
        if y is not None:
            yb = yb + y[b * blk:(b + 1) * blk]
        res = _rms(yb, gpost_ref[...])
        for l in range(n_slabs):
            xs[l, rs, :] = xs[l, rs, :] + res[:, l * LANES:(l + 1) * LANES]
        for l in range(n_slabs):
            for s in range(HALO_ROWS):
                o_ref[pl.ds(s * run + b * blk_run, blk_run), pl.ds(l * LANES, LANES)] = (
                    xs[l, pl.ds(b * blk + s, blk_run, stride=HALO_ROWS), :])


def _ffn_call(x1, g_pre, w_up_bf16, conv_w, conv_b, w_down_bf16, g_post, seq):
    n = x1.shape[0]
    rows = FFN_ROWS
    hb = rows // HALO_ROWS
    n_halo_blocks = n // HALO_ROWS
    const = lambda i: (0, 0)
    return pl.pallas_call(
        functools.partial(_ffn_kernel, seq=seq),
        grid=(n // rows,),
        in_specs=[
            pl.BlockSpec((rows, D_MODEL), lambda i: (i, 0)),
            pl.BlockSpec((HALO_ROWS, D_MODEL), lambda i: (jnp.maximum(i * hb - 1, 0), 0)),
            pl.BlockSpec((HALO_ROWS, D_MODEL),
                         lambda i: (jnp.minimum((i + 1) * hb, n_halo_blocks - 1), 0)),
            pl.BlockSpec((1, D_MODEL), const),
            pl.BlockSpec((D_MODEL, 2 * D_FF), const, pipeline_mode=pl.Buffered(1)),
            pl.BlockSpec((3, 2 * D_FF), const),
            pl.BlockSpec((1, 2 * D_FF), const),
            pl.BlockSpec((D_FF, D_MODEL), const, pipeline_mode=pl.Buffered(1)),
            pl.BlockSpec((1, D_MODEL), const),
        ],
        out_specs=pl.BlockSpec((rows, D_MODEL), lambda i: (i, 0)),
        out_shape=jax.ShapeDtypeStruct((n, D_MODEL), jnp.float32),
        scratch_shapes=[
            pltpu.VMEM((D_MODEL // LANES, rows, LANES), jnp.float32),
            pltpu.VMEM((rows + 2 * HALO_ROWS, D_MODEL), jnp.float32),
            *[pltpu.VMEM((rows + 2 * HALO_ROWS, 2 * FFN_CHUNK), jnp.float32)] * FFN_U_RING,
            *[pltpu.VMEM((rows, FFN_DOWN_GROUP * FFN_CHUNK), jnp.bfloat16)]
            * pl.cdiv(D_FF // FFN_CHUNK, FFN_DOWN_GROUP),
        ],
        compiler_params=pltpu.CompilerParams(
            dimension_semantics=("arbitrary",), vmem_limit_bytes=VMEM_LIMIT),
        name="ffn",
    )(x1, x1, x1, g_pre, w_up_bf16, conv_w, conv_b, w_down_bf16, g_post)


def _block_diag(pool_w):
    groups = pool_w.shape[0]
    bd = jnp.zeros((POOL_WIDTH, POOL_WIDTH), pool_w.dtype)
    for g in range(groups):
        lo = g * POOL_GROUP_DIM
        bd = bd.at[lo:lo + POOL_GROUP_DIM, lo:lo + POOL_GROUP_DIM].set(pool_w[g])
    return bd


def _layer(x, params):
    (g_mix_pre, g_mix_post, w_in, pool_bd, pool_scale, g_pool_out, g_attn_out, w_out,
     g_ffn_pre, g_ffn_post, w_up, conv_w, conv_b, w_down) = params
    batch, seq, _ = x.shape
    assert seq % ATTN_ROWS == 0 and seq % PROJ_ROWS == 0 and seq % MIX_ROWS == 0 and seq % FFN_ROWS == 0
    xf = x.reshape(batch * seq, D_MODEL)
    p_in, q, k, v = _proj_call(xf, g_mix_pre, w_in, _rope_tables(seq), seq)
    attn = _attn_call(q, k, v, batch, seq).reshape(N_LANE_GROUPS, batch * seq, LANES)
    x1 = _mix_call(xf, p_in, attn, pool_bd, pool_scale, g_pool_out, g_attn_out, w_out, g_mix_post, seq)
    y = _ffn_call(x1, g_ffn_pre, w_up, conv_w, conv_b, w_down, g_ffn_post, seq)
    return y.reshape(batch, seq, D_MODEL)


def kernel(x_prompt, x_sample, g_mix_pre, g_mix_post, w_in, pool_w, pool_scale, g_pool_out, g_attn_out,
           w_out, g_ffn_pre, g_ffn_post, w_up, conv_w, conv_b, w_down):
    depth = w_in.shape[0]
    bf16 = jnp.bfloat16

    def layer_params(l):
        return (g_mix_pre[l][None], g_mix_post[l][None], w_in[l].astype(bf16),
                _block_diag(pool_w[l]).astype(bf16), pool_scale[l][None], g_pool_out[l][None],
                g_attn_out[l][None], w_out[l].astype(bf16), g_ffn_pre[l][None], g_ffn_post[l][None],
                w_up[l].astype(bf16), conv_w[l], conv_b[l][None], w_down[l].astype(bf16))

    params = [layer_params(l) for l in range(depth)]

    def run(x):
        for p in params:
            x = _layer(x, p)
        return x

    return (run(x_prompt), run(x_sample))
```

```python
import functools
import math

import jax
import jax.numpy as jnp
from jax import lax
from jax.experimental import pallas as pl
from jax.experimental.pallas import tpu as pltpu

D_MODEL = 1024
POOL_WIDTH = 256
POOL_WINDOWS = (2, 4, 8, 16)
POOL_GROUP_DIM = 64
ATTN_WIDTH = 768
HEAD_DIM = 64
DILATIONS = (1, 4, 16)
QKV_LAYOUTS = (4, 16)
BAND_RADIUS = 64
ROPE_THETA = 500000.0
ROPE_DIM = 16
D_FF = 2816
EPS = 1e-6
IN_WIDTH = POOL_WIDTH + 3 * ATTN_WIDTH

LANES = 128
N_LANE_GROUPS = ATTN_WIDTH // LANES
NEG_BIG = -1e30
Q_SCALE = HEAD_DIM ** -0.5 * math.log2(math.e)

PROJ_ROWS = 512
ATTN_ROWS = 2048
Q_CHUNK = 128
K_CHUNK = Q_CHUNK + 2 * BAND_RADIUS
MIX_ROWS = 1024
FFN_ROWS = 512
FFN_CHUNK = 256
FFN_LOOKAHEAD = 3
FFN_U_RING = FFN_LOOKAHEAD + 1
FFN_DOWN_GROUP = 4
FFN_TAIL_BLOCKS = 4
HALO_ROWS = 8

VMEM_LIMIT = 56 * 1024 * 1024


def _rms(x, g):
    ms = jnp.mean(x * x, axis=-1, keepdims=True)
    return x * lax.rsqrt(ms + EPS) * g


def _proj_kernel(x_ref, g_ref, w_ref, tab_ref, p_ref, *refs):
    n_lay = len(QKV_LAYOUTS)
    outs = [refs[n_lay * i:n_lay * (i + 1)] for i in range(3)]
    stage = refs[3 * n_lay:]
    rows = PROJ_ROWS
    bf16 = jnp.bfloat16
    h = _rms(x_ref[...], g_ref[...]).astype(bf16)
    proj = jnp.dot(h, w_ref[...], preferred_element_type=jnp.float32)
    p_ref[...] = proj[:, :POOL_WIDTH]
    cos, sin_lo, sin_hi = tab_ref[0], tab_ref[1], tab_ref[2]

    def rope(t):
        return t * cos + pltpu.roll(t, LANES - 8, 1) * sin_lo + pltpu.roll(t, 8, 1) * sin_hi

    def emit(val, out_refs, g, slab_a, slab_b):
        o4, o16 = out_refs
        slab_a[...] = val
        for r_lo in range(4):
            plane = slab_a[pl.ds(r_lo, rows // 4, stride=4), :]
            o4[g, :, pl.ds(r_lo * LANES, LANES)] = plane.astype(bf16)
            slab_b[pl.ds(r_lo * (rows // 4), rows // 4), :] = plane
        for r_lo in range(4):
            for r_hi in range(4):
                piece = slab_b[pl.ds(r_lo * (rows // 4) + r_hi, rows // 16, stride=4), :]
                o16[g, :, pl.ds((4 * r_hi + r_lo) * LANES, LANES)] = piece.astype(bf16)

    for g in range(N_LANE_GROUPS):
        lo = POOL_WIDTH + g * LANES
        emit(rope(proj[:, lo:lo + LANES]) * Q_SCALE, outs[0], g, stage[6 * g], stage[6 * g + 1])
        lo += ATTN_WIDTH
        emit(rope(proj[:, lo:lo + LANES]), outs[1], g, stage[6 * g + 2], stage[6 * g + 3])
        lo += ATTN_WIDTH
        emit(proj[:, lo:lo + LANES], outs[2], g, stage[6 * g + 4], stage[6 * g + 5])


def _rope_tables(seq):
    pos = jnp.arange(seq, dtype=jnp.float32)
    inv_freq = ROPE_THETA ** (-jnp.arange(0, ROPE_DIM, 2, dtype=jnp.float32) / ROPE_DIM)
    half = ROPE_DIM // 2
    dim = jnp.arange(LANES) % HEAD_DIM
    ang = pos[:, None] * inv_freq[dim % half][None, :]
    cos, sin = jnp.cos(ang), jnp.sin(ang)
    first_half, rotated = (dim < half)[None, :], (dim < ROPE_DIM)[None, :]
    c = jnp.where(rotated, cos, 1.0)
    s1 = jnp.where(first_half, -sin, 0.0)
    s2 = jnp.where(rotated & ~first_half, sin, 0.0)
    return jnp.stack([c, s1, s2])


def _proj_call(x, g, w_bf16, tabs, seq):
    n = x.shape[0]
    t = PROJ_ROWS
    blocks_per_seq = seq // t
    qkv_shapes = [jax.ShapeDtypeStruct((N_LANE_GROUPS, n // d, d * LANES), jnp.bfloat16)
                  for d in QKV_LAYOUTS] * 3
    qkv_specs = [pl.BlockSpec((N_LANE_GROUPS, t // d, d * LANES), lambda i: (0, i, 0))
                 for d in QKV_LAYOUTS] * 3
    outs = pl.pallas_call(
        _proj_kernel,
        grid=(n // t,),
        in_specs=[
            pl.BlockSpec((t, D_MODEL), lambda i: (i, 0)),
            pl.BlockSpec((1, D_MODEL), lambda i: (0, 0)),
            pl.BlockSpec((D_MODEL, IN_WIDTH), lambda i: (0, 0)),
            pl.BlockSpec((3, t, LANES), lambda i: (0, i % blocks_per_seq, 0)),
        ],
        out_specs=[pl.BlockSpec((t, POOL_WIDTH), lambda i: (i, 0)), *qkv_specs],
        out_shape=[jax.ShapeDtypeStruct((n, POOL_WIDTH), jnp.float32), *qkv_shapes],
        scratch_shapes=[pltpu.VMEM((t, LANES), jnp.float32)] * (2 * 3 * N_LANE_GROUPS),
        compiler_params=pltpu.CompilerParams(
            dimension_semantics=("arbitrary",), vmem_limit_bytes=VMEM_LIMIT),
        name="proj",
    )(x, g, w_bf16, tabs)
    n_lay = len(QKV_LAYOUTS)
    return outs[0], outs[1:1 + n_lay], outs[1 + n_lay:1 + 2 * n_lay], outs[1 + 2 * n_lay:]


def _attn_kernel(*refs, n_blocks):
    n_lay = len(QKV_LAYOUTS)
    q_refs = dict(zip(QKV_LAYOUTS, refs[:n_lay]))
    k_refs = {d: refs[n_lay + 3 * i:n_lay + 3 * i + 3] for i, d in enumerate(QKV_LAYOUTS)}
    v_refs = {d: refs[4 * n_lay + 3 * i:4 * n_lay + 3 * i + 3] for i, d in enumerate(QKV_LAYOUTS)}
    o_ref, tbuf, mpat, mall, opat, lpat, bias_ref = refs[7 * n_lay:]
    t = pl.program_id(2)
    rows = ATTN_ROWS
    plane = rows // 4
    f32 = jnp.float32
    bf16 = jnp.bfloat16

    @pl.when((pl.program_id(0) == 0) & (pl.program_id(1) == 0) & (t == 0))
    def _():
        row = lax.broadcasted_iota(jnp.int32, (2 * Q_CHUNK, K_CHUNK), 0) % Q_CHUNK
        col = lax.broadcasted_iota(jnp.int32, (2 * Q_CHUNK, K_CHUNK), 1)
        q_rows, k_rows = Q_CHUNK // 4, K_CHUNK // 4
        delta = 4 * (col % k_rows - row % q_rows) + (col // k_rows - row // q_rows)
        families = (
            ((col >= row) & (col <= row + 2 * BAND_RADIUS), col, BAND_RADIUS, BAND_RADIUS + Q_CHUNK),
            ((delta >= 0) & (delta <= 2 * BAND_RADIUS), col % k_rows, BAND_RADIUS // 4,
             (BAND_RADIUS + Q_CHUNK) // 4),
        )
        for fam, (band, key_pos, lo, hi) in enumerate(families):
            for variant in range(4):
                ok = band
                if variant & 1:
                    ok = ok & (key_pos >= lo)
                if variant & 2:
                    ok = ok & (key_pos < hi)
                bias_ref[4 * fam + variant] = jnp.where(ok, 0.0, NEG_BIG).astype(f32)

    lane = lax.broadcasted_iota(jnp.int32, (Q_CHUNK, LANES), 1)
    first_head = lane < HEAD_DIM

    def pick_head(x):
        return jnp.where(first_head, x[:Q_CHUNK], x[Q_CHUNK:])

    def ext_rows(trio, lo, hi, block):
        prev_ref, main_ref, next_ref = trio
        lanes = pl.ds(block * LANES, LANES)
        n_main, n_halo = main_ref.shape[0], prev_ref.shape[0]
        parts = []
        if lo < 0:
            parts.append(prev_ref[pl.ds(n_halo + lo, -lo), lanes])
        parts.append(main_ref[pl.ds(max(lo, 0), min(hi, n_main) - max(lo, 0)), lanes])
        if hi > n_main:
            parts.append(next_ref[pl.ds(0, hi - n_main), lanes])
        return parts

    def rows_of(parts):
        return parts[0] if len(parts) == 1 else jnp.concatenate(parts, axis=0)

    tiles = []
    q_rows, k_rows = Q_CHUNK // 4, K_CHUNK // 4
    for c in range(rows // Q_CHUNK):
        lo = c * q_rows - BAND_RADIUS // 4
        q_parts = [q_refs[4][pl.ds(c * q_rows, q_rows), pl.ds(j * LANES, LANES)] for j in range(4)]
        keys = lambda trios, lo=lo: rows_of(
            [p for j in range(4) for p in ext_rows(trios[4], lo, lo + k_rows, j)])
        acc = [(j * q_rows, pl.ds(j * plane + c * q_rows, q_rows)) for j in range(4)]
        tiles.append((0, q_parts, keys, 1, c == 0, c == rows // Q_CHUNK - 1, acc))
    for r in range(4):
        for c in range(plane // Q_CHUNK):
            lo = c * Q_CHUNK - BAND_RADIUS
            q_parts = [q_refs[4][pl.ds(c * Q_CHUNK, Q_CHUNK), pl.ds(r * LANES, LANES)]]
            keys = lambda trios, lo=lo, r=r: rows_of(ext_rows(trios[4], lo, lo + K_CHUNK, r))
            acc = [(0, pl.ds(r * plane + c * Q_CHUNK, Q_CHUNK))]
            tiles.append((1, q_parts, keys, 0, c == 0, c == plane // Q_CHUNK - 1, acc))
    for r in range(16):
        q_parts = [q_refs[16][:, pl.ds(r * LANES, LANES)]]
        keys = lambda trios, r=r: rows_of(ext_rows(trios[16], -BAND_RADIUS, Q_CHUNK + BAND_RADIUS, r))
        acc = [(0, pl.ds((r % 4) * plane + r // 4, Q_CHUNK, stride=4))]
        tiles.append((2, q_parts, keys, 0, True, True, acc))

    def bias_index(family, at_start, at_end):
        if n_blocks == 1:
            return 4 * family + int(at_start) + 2 * int(at_end)
        first = (t == 0).astype(jnp.int32) if at_start else 0
        last = (t == n_blocks - 1).astype(jnp.int32) if at_end else 0
        return 4 * family + first + 2 * last

    def store_rows(ref, lead, acc, x):
        for start, idx in acc:
            ref[(*lead, idx, slice(None))] = x[start:start + idx.size]

    def load_rows(ref, lead, acc):
        return rows_of([ref[(*lead, idx, slice(None))] for _, idx in acc])

    dyn_zero = jnp.minimum(t, 0)

    for n, (pi, q_parts, keys, family, at_start, at_end, acc) in sorted(
            enumerate(tiles), key=lambda item: -item[1][0]):
        qc = rows_of(q_parts)
        zero = jnp.zeros_like(qc)
        q2 = jnp.concatenate([jnp.where(first_head, qc, zero), jnp.where(first_head, zero, qc)], axis=0)
        s = lax.dot_general(q2, keys(k_refs), (((1,), (1,)), ((), ())), preferred_element_type=f32)
        s = s + bias_ref[bias_index(family, at_start, at_end)]
        tbuf[n + dyn_zero] = s
        m = jnp.broadcast_to(jnp.max(s, axis=-1, keepdims=True), (2 * Q_CHUNK, LANES))
        for h in range(2):
            store_rows(mpat, (pi, h), acc, m[h * Q_CHUNK:(h + 1) * Q_CHUNK])

    pass_rows = 64
    for i in range(plane // q_rows):
        for r in range(4):
            sl = pl.ds(r * plane + i * q_rows, q_rows)
            for h in range(2):
                mall[h, sl, :] = jnp.maximum(jnp.maximum(mpat[0, h, sl, :], mpat[1, h, sl, :]),
                                             mpat[2, h, sl, :])

    ones = jnp.ones((K_CHUNK, LANES), bf16)
    for n, (pi, q_parts, keys, family, at_start, at_end, acc) in enumerate(tiles):
        mb = jnp.concatenate([load_rows(mall, (0,), acc), load_rows(mall, (1,), acc)], axis=0)
        p = jnp.exp2(tbuf[n + dyn_zero] - jnp.concatenate([mb, mb], axis=1)).astype(bf16)
        vc = jnp.concatenate([keys(v_refs), ones], axis=1)
        ol = jnp.dot(p, vc, preferred_element_type=f32)
        store_rows(opat, (pi,), acc, pick_head(ol[:, :LANES]))
        store_rows(lpat, (pi,), acc, pick_head(ol[:, LANES:]))

    for r in range(4):
        for i in range(plane // pass_rows):
            sl = pl.ds(r * plane + i * pass_rows, pass_rows)
            num = opat[0, sl, :] + opat[1, sl, :] + opat[2, sl, :]
            den = lpat[0, sl, :] + lpat[1, sl, :] + lpat[2, sl, :]
            o_ref[pl.ds(r + 4 * i * pass_rows, pass_rows, stride=4), :] = num / den


def _attn_call(q, k, v, batch, seq):
    rows = ATTN_ROWS
    n_blocks = seq // rows
    radius = BAND_RADIUS

    def view(a, d):
        return a.reshape(N_LANE_GROUPS, batch, seq // d, d * LANES)

    operands, in_specs = [], []
    for d, a in zip(QKV_LAYOUTS, q):
        operands.append(view(a, d))
        in_specs.append(pl.BlockSpec((None, None, rows // d, d * LANES), lambda b, g, t: (g, b, t, 0)))
    for arrs in (k, v):
        for d, a in zip(QKV_LAYOUTS, arrs):
            halo_per_block = rows // d // radius
            n_halo = seq // d // radius
            operands += [view(a, d)] * 3
            in_specs += [
                pl.BlockSpec((None, None, radius, d * LANES),
                             lambda b, g, t, hb=halo_per_block: (g, b, jnp.maximum(t * hb - 1, 0), 0)),
                pl.BlockSpec((None, None, rows // d, d * LANES), lambda b, g, t: (g, b, t, 0)),
                pl.BlockSpec((None, None, radius, d * LANES),
                             lambda b, g, t, hb=halo_per_block, nh=n_halo:
                             (g, b, jnp.minimum((t + 1) * hb, nh - 1), 0)),
            ]
    n_tiles = len(DILATIONS) * rows // Q_CHUNK
    return pl.pallas_call(
        functools.partial(_attn_kernel, n_blocks=n_blocks),
        grid=(batch, N_LANE_GROUPS, n_blocks),
        in_specs=in_specs,
        out_specs=pl.BlockSpec((None, None, rows, LANES), lambda b, g, t: (g, b, t, 0)),
        out_shape=jax.ShapeDtypeStruct((N_LANE_GROUPS, batch, seq, LANES), jnp.float32),
        scratch_shapes=[
            pltpu.VMEM((n_tiles, 2 * Q_CHUNK, K_CHUNK), jnp.float32),
            pltpu.VMEM((len(DILATIONS), 2, rows, LANES), jnp.float32),
            pltpu.VMEM((2, rows, LANES), jnp.float32),
            pltpu.VMEM((len(DILATIONS), rows, LANES), jnp.float32),
            pltpu.VMEM((len(DILATIONS), rows, LANES), jnp.float32),
            pltpu.VMEM((8, 2 * Q_CHUNK, K_CHUNK), jnp.float32),
        ],
        compiler_params=pltpu.CompilerParams(
            dimension_semantics=("arbitrary", "arbitrary", "arbitrary"), vmem_limit_bytes=VMEM_LIMIT),
        name="attn",
    )(*operands)


def _mix_kernel(x_ref, p_ref, pprev_ref, pnext_ref, a_ref, pw_ref, ps_ref, gp_ref, ga_ref,
                wo_ref, gpost_ref, o_ref, pbuf, mixed, *, seq):
    rows = MIX_ROWS
    f32 = jnp.float32
    i = pl.program_id(0)
    pos0 = (i * rows) % seq
    zero_rows = jnp.zeros((HALO_ROWS, POOL_WIDTH), f32)
    pbuf[pl.ds(0, HALO_ROWS), :] = jnp.where(pos0 > 0, pprev_ref[...], zero_rows)
    pbuf[pl.ds(HALO_ROWS, rows), :] = p_ref[...]
    pbuf[pl.ds(HALO_ROWS + rows, HALO_ROWS), :] = jnp.where(pos0 + rows < seq, pnext_ref[...], zero_rows)

    pos = pos0 + lax.broadcasted_iota(jnp.int32, (rows, 1), 0)
    lane = lax.broadcasted_iota(jnp.int32, (rows, LANES), 1)
    means = []
    for half in range(2):
        w_small, w_big = POOL_WINDOWS[2 * half], POOL_WINDOWS[2 * half + 1]
        lanes = pl.ds(half * LANES, LANES)

        def wsum(lo, hi):
            acc = pbuf[pl.ds(HALO_ROWS + lo, rows), lanes]
            for off in range(lo + 1, hi):
                acc = acc + pbuf[pl.ds(HALO_ROWS + off, rows), lanes]
            return acc

        if half == 0:
            small = wsum(-(w_small // 2), w_small // 2)
            big = small + wsum(-(w_big // 2), -(w_small // 2)) + wsum(w_small // 2, w_big // 2)
        else:
            n_buf = rows + 2 * HALO_ROWS
            a = pbuf[:, lanes]
            for step in (1, 2, 4):
                a = a + pltpu.roll(a, n_buf - step, 0)
            small = pltpu.roll(a, w_small // 2, 0)[HALO_ROWS:HALO_ROWS + rows]
            big = (a + pltpu.roll(a, n_buf - w_small, 0))[:rows]

        def count(w):
            return (jnp.minimum(pos + w // 2, seq) - jnp.maximum(pos - w // 2, 0)).astype(f32)

        first = lane < POOL_GROUP_DIM
        total = jnp.where(first, small, big)
        cnt = jnp.where(first, count(w_small), count(w_big))
        means.append(total / cnt - pbuf[pl.ds(HALO_ROWS, rows), lanes])
    pooled = jnp.concatenate(means, axis=-1).astype(jnp.bfloat16)
    pool_out = jnp.dot(pooled, pw_ref[...], preferred_element_type=f32) * ps_ref[...]
    mixed[:, pl.ds(0, POOL_WIDTH)] = _rms(pool_out, gp_ref[...]).astype(jnp.bfloat16)

    sq = None
    for g in range(N_LANE_GROUPS):
        a = a_ref[g]
        sq = a * a if sq is None else sq + a * a
    inv = lax.rsqrt(jnp.sum(sq, axis=-1, keepdims=True) / ATTN_WIDTH + EPS)
    for g in range(N_LANE_GROUPS):
        ga = ga_ref[:, pl.ds(g * LANES, LANES)]
        mixed[:, pl.ds(POOL_WIDTH + g * LANES, LANES)] = (a_ref[g] * inv * ga).astype(jnp.bfloat16)

    y = jnp.dot(mixed[...], wo_ref[...], preferred_element_type=f32)
    o_ref[...] = x_ref[...] + _rms(y, gpost_ref[...])


def _mix_call(x, p_in, attn, pool_bd, pool_scale, g_pool, g_attn, w_out_bf16, g_post, seq):
    n = x.shape[0]
    rows = MIX_ROWS
    hb = rows // HALO_ROWS
    n_halo_blocks = n // HALO_ROWS
    const = lambda i: (0, 0)
    return pl.pallas_call(
        functools.partial(_mix_kernel, seq=seq),
        grid=(n // rows,),
        in_specs=[
            pl.BlockSpec((rows, D_MODEL), lambda i: (i, 0)),
            pl.BlockSpec((rows, POOL_WIDTH), lambda i: (i, 0)),
            pl.BlockSpec((HALO_ROWS, POOL_WIDTH), lambda i: (jnp.maximum(i * hb - 1, 0), 0)),
            pl.BlockSpec((HALO_ROWS, POOL_WIDTH),
                         lambda i: (jnp.minimum((i + 1) * hb, n_halo_blocks - 1), 0)),
            pl.BlockSpec((N_LANE_GROUPS, rows, LANES), lambda i: (0, i, 0)),
            pl.BlockSpec((POOL_WIDTH, POOL_WIDTH), const),
            pl.BlockSpec((1, POOL_WIDTH), const),
            pl.BlockSpec((1, POOL_WIDTH), const),
            pl.BlockSpec((1, ATTN_WIDTH), const),
            pl.BlockSpec((D_MODEL, D_MODEL), const),
            pl.BlockSpec((1, D_MODEL), const),
        ],
        out_specs=pl.BlockSpec((rows, D_MODEL), lambda i: (i, 0)),
        out_shape=jax.ShapeDtypeStruct((n, D_MODEL), jnp.float32),
        scratch_shapes=[
            pltpu.VMEM((rows + 2 * HALO_ROWS, POOL_WIDTH), jnp.float32),
            pltpu.VMEM((rows, D_MODEL), jnp.bfloat16),
        ],
        compiler_params=pltpu.CompilerParams(
            dimension_semantics=("arbitrary",), vmem_limit_bytes=VMEM_LIMIT),
        name="mix",
    )(x, p_in, p_in, p_in, attn, pool_bd, pool_scale, g_pool, g_attn, w_out_bf16, g_post)


def _ffn_kernel(x_ref, xprev_ref, xnext_ref, gpre_ref, wup_ref, cw_ref, cb_ref, wdn_ref, gpost_ref,
                o_ref, xs, hbuf, *bufs, seq):
    n_chunks = D_FF // FFN_CHUNK
    ubufs, abufs = bufs[:FFN_U_RING], bufs[FFN_U_RING:]
    rows = FFN_ROWS
    run = rows // HALO_ROWS
    n_slabs = D_MODEL // LANES
    f32 = jnp.float32
    i = pl.program_id(0)
    pos0 = (i * rows) % seq
    g = gpre_ref[...]

    for l in range(n_slabs):
        for s in range(HALO_ROWS):
            xs[l, pl.ds(s, run, stride=HALO_ROWS), :] = x_ref[pl.ds(s * run, run), pl.ds(l * LANES, LANES)]
    sq = None
    for l in range(n_slabs):
        xl = xs[l]
        sq = xl * xl if sq is None else sq + xl * xl
    inv = lax.rsqrt(jnp.sum(sq, axis=-1, keepdims=True) / D_MODEL + EPS)
    for l in range(n_slabs):
        lanes = pl.ds(l * LANES, LANES)
        hbuf[pl.ds(HALO_ROWS, rows), lanes] = xs[l] * inv * gpre_ref[:, lanes]
    hprev = jnp.where(pos0 > 0, _rms(xprev_ref[...], g), 0.0)
    hnext = jnp.where(pos0 + rows < seq, _rms(xnext_ref[...], g), 0.0)
    hbuf[pl.ds(0, HALO_ROWS), :] = jnp.broadcast_to(hprev[HALO_ROWS - 1:], (HALO_ROWS, D_MODEL))
    hbuf[pl.ds(HALO_ROWS + rows, HALO_ROWS), :] = jnp.broadcast_to(hnext[:1], (HALO_ROWS, D_MODEL))
    h = hbuf[...].astype(jnp.bfloat16)

    sub = lax.broadcasted_iota(jnp.int32, (HALO_ROWS, FFN_CHUNK), 0)

    def conv(c, half):
        col = half * D_FF + c * FFN_CHUNK
        w = cw_ref[:, pl.ds(col, FFN_CHUNK)]
        acc = cb_ref[:, pl.ds(col, FFN_CHUNK)]
        for j in range(3):
            u = ubufs[c % FFN_U_RING][pl.ds(HALO_ROWS * j, rows), pl.ds(half * FFN_CHUNK, FFN_CHUNK)]
            acc = acc + u * w[j:j + 1]
        return acc

    def up(c):
        buf = ubufs[c % FFN_U_RING]
        for half in range(2):
            col = half * D_FF + c * FFN_CHUNK
            cols = pl.ds(half * FFN_CHUNK, FFN_CHUNK)
            buf[:, cols] = jnp.dot(h, wup_ref[:, pl.ds(col, FFN_CHUNK)], preferred_element_type=f32)
            halo = jnp.broadcast_to(buf[pl.ds(0, 1), cols], (HALO_ROWS, FFN_CHUNK))
            last = pltpu.roll(buf[pl.ds(rows, HALO_ROWS), cols], 1, 0)
            buf[pl.ds(0, HALO_ROWS), cols] = jnp.where(sub == 0, halo, last)
            halo = jnp.broadcast_to(buf[pl.ds(rows + HALO_ROWS, 1), cols], (HALO_ROWS, FFN_CHUNK))
            first = pltpu.roll(buf[pl.ds(HALO_ROWS, HALO_ROWS), cols], HALO_ROWS - 1, 0)
            buf[pl.ds(rows + HALO_ROWS, HALO_ROWS), cols] = jnp.where(sub == HALO_ROWS - 1, halo, first)

    def down(g, n_in_group):
        width = n_in_group * FFN_CHUNK
        w = wdn_ref[pl.ds(g * FFN_DOWN_GROUP * FFN_CHUNK, width), :]
        return jnp.dot(abufs[g][:, pl.ds(0, width)], w, preferred_element_type=f32)

    for c in range(min(FFN_LOOKAHEAD, n_chunks)):
        up(c)
    y = None
    for c in range(n_chunks):
        if c + FFN_LOOKAHEAD < n_chunks:
            up(c + FFN_LOOKAHEAD)
        gate = conv(c, 0)
        val = conv(c, 1)
        gelu = 0.5 * gate * (1.0 + lax.erf(gate * (2.0 ** -0.5)))
        g, j = divmod(c, FFN_DOWN_GROUP)
        abufs[g][:, pl.ds(j * FFN_CHUNK, FFN_CHUNK)] = (gelu * val).astype(jnp.bfloat16)
        if j + 1 == FFN_DOWN_GROUP and c + 1 < n_chunks:
            part = down(g, j + 1)
            y = part if y is None else y + part

    g, n_last = divmod(n_chunks - 1, FFN_DOWN_GROUP)
    width = (n_last + 1) * FFN_CHUNK
    w_last = wdn_ref[pl.ds(g * FFN_DOWN_GROUP * FFN_CHUNK, width), :]
    blk = rows // FFN_TAIL_BLOCKS
    blk_run = run // FFN_TAIL_BLOCKS
    for b in range(FFN_TAIL_BLOCKS):
        rs = pl.ds(b * blk, blk)
        yb = jnp.dot(abufs[g][rs, pl.ds(0, width)], w_last, preferred_element_type=f32)
        if y is not None:
            yb = yb + y[b * blk:(b + 1) * blk]
        res = _rms(yb, gpost_ref[...])
        for l in range(n_slabs):
            xs[l, rs, :] = xs[l, rs, :] + res[:, l * LANES:(l + 1) * LANES]
        for l in range(n_slabs):
            for s in range(HALO_ROWS):
                o_ref[pl.ds(s * run + b * blk_run, blk_run), pl.ds(l * LANES, LANES)] = (
                    xs[l, pl.ds(b * blk + s, blk_run, stride=HALO_ROWS), :])


def _ffn_call(x1, g_pre, w_up_bf16, conv_w, conv_b, w_down_bf16, g_post, seq):
    n = x1.shape[0]
    rows = FFN_ROWS
    hb = rows // HALO_ROWS
    n_halo_blocks = n // HALO_ROWS
    const = lambda i: (0, 0)
    return pl.pallas_call(
        functools.partial(_ffn_kernel, seq=seq),
        grid=(n // rows,),
        in_specs=[
            pl.BlockSpec((rows, D_MODEL), lambda i: (i, 0)),
            pl.BlockSpec((HALO_ROWS, D_MODEL), lambda i: (jnp.maximum(i * hb - 1, 0), 0)),
            pl.BlockSpec((HALO_ROWS, D_MODEL),
                         lambda i: (jnp.minimum((i + 1) * hb, n_halo_blocks - 1), 0)),
            pl.BlockSpec((1, D_MODEL), const),
            pl.BlockSpec((D_MODEL, 2 * D_FF), const, pipeline_mode=pl.Buffered(1)),
            pl.BlockSpec((3, 2 * D_FF), const),
            pl.BlockSpec((1, 2 * D_FF), const),
            pl.BlockSpec((D_FF, D_MODEL), const, pipeline_mode=pl.Buffered(1)),
            pl.BlockSpec((1, D_MODEL), const),
        ],
        out_specs=pl.BlockSpec((rows, D_MODEL), lambda i: (i, 0)),
        out_shape=jax.ShapeDtypeStruct((n, D_MODEL), jnp.float32),
        scratch_shapes=[
            pltpu.VMEM((D_MODEL // LANES, rows, LANES), jnp.float32),
            pltpu.VMEM((rows + 2 * HALO_ROWS, D_MODEL), jnp.float32),
            *[pltpu.VMEM((rows + 2 * HALO_ROWS, 2 * FFN_CHUNK), jnp.float32)] * FFN_U_RING,
            *[pltpu.VMEM((rows, FFN_DOWN_GROUP * FFN_CHUNK), jnp.bfloat16)]
            * pl.cdiv(D_FF // FFN_CHUNK, FFN_DOWN_GROUP),
        ],
        compiler_params=pltpu.CompilerParams(
            dimension_semantics=("arbitrary",), vmem_limit_bytes=VMEM_LIMIT),
        name="ffn",
    )(x1, x1, x1, g_pre, w_up_bf16, conv_w, conv_b, w_down_bf16, g_post)


def _block_diag(pool_w):
    groups = pool_w.shape[0]
    bd = jnp.zeros((POOL_WIDTH, POOL_WIDTH), pool_w.dtype)
    for g in range(groups):
        lo = g * POOL_GROUP_DIM
        bd = bd.at[lo:lo + POOL_GROUP_DIM, lo:lo + POOL_GROUP_DIM].set(pool_w[g])
    return bd


def _layer(x, params):
    (g_mix_pre, g_mix_post, w_in, pool_bd, pool_scale, g_pool_out, g_attn_out, w_out,
     g_ffn_pre, g_ffn_post, w_up, conv_w, conv_b, w_down) = params
    batch, seq, _ = x.shape
    assert seq % ATTN_ROWS == 0 and seq % PROJ_ROWS == 0 and seq % MIX_ROWS == 0 and seq % FFN_ROWS == 0
    xf = x.reshape(batch * seq, D_MODEL)
    p_in, q, k, v = _proj_call(xf, g_mix_pre, w_in, _rope_tables(seq), seq)
    attn = _attn_call(q, k, v, batch, seq).reshape(N_LANE_GROUPS, batch * seq, LANES)
    x1 = _mix_call(xf, p_in, attn, pool_bd, pool_scale, g_pool_out, g_attn_out, w_out, g_mix_post, seq)
    y = _ffn_call(x1, g_ffn_pre, w_up, conv_w, conv_b, w_down, g_ffn_post, seq)
    return y.reshape(batch, seq, D_MODEL)


def kernel(x_prompt, x_sample, g_mix_pre, g_mix_post, w_in, pool_w, pool_scale, g_pool_out, g_attn_out,
           w_out, g_ffn_pre, g_ffn_post, w_up, conv_w, conv_b, w_down):
    depth = w_in.shape[0]
    bf16 = jnp.bfloat16

    def layer_params(l):
        return (g_mix_pre[l][None], g_mix_post[l][None], w_in[l].astype(bf16),
                _block_diag(pool_w[l]).astype(bf16), pool_scale[l][None], g_pool_out[l][None],
                g_attn_out[l][None], w_out[l].astype(bf16), g_ffn_pre[l][None], g_ffn_post[l][None],
                w_up[l].astype(bf16), conv_w[l], conv_b[l][None], w_down[l].astype(bf16))

    params = [layer_params(l) for l in range(depth)]

    def run(x):
        for p in params:
            x = _layer(x, p)
        return x

    return (run(x_prompt), run(x_sample))
```

```python
import functools
import math

import jax
import jax.numpy as jnp
from jax import lax
from jax.experimental import pallas as pl
from jax.experimental.pallas import tpu as pltpu

D_MODEL = 1024
POOL_WIDTH = 256
POOL_WINDOWS = (2, 4, 8, 16)
POOL_GROUP_DIM = 64
ATTN_WIDTH = 768
HEAD_DIM = 64
DILATIONS = (1, 4, 16)
QKV_LAYOUTS = (4, 16)
BAND_RADIUS = 64
ROPE_THETA = 500000.0
ROPE_DIM = 16
D_FF = 2816
EPS = 1e-6
IN_WIDTH = POOL_WIDTH + 3 * ATTN_WIDTH

LANES = 128
N_LANE_GROUPS = ATTN_WIDTH // LANES
NEG_BIG = -1e30
Q_SCALE = HEAD_DIM ** -0.5 * math.log2(math.e)

PROJ_ROWS = 512
ATTN_ROWS = 2048
Q_CHUNK = 128
K_CHUNK = Q_CHUNK + 2 * BAND_RADIUS
MIX_ROWS = 1024
FFN_ROWS = 512
FFN_CHUNK = 256
FFN_LOOKAHEAD = 3
FFN_U_RING = FFN_LOOKAHEAD + 1
FFN_DOWN_GROUP = 4
FFN_TAIL_BLOCKS = 4
HALO_ROWS = 8

V7X_VMEM_BYTES = 64 * 1024 * 1024
VMEM_LIMIT = V7X_VMEM_BYTES - 8 * 1024 * 1024


def _rms(x, g):
    ms = jnp.mean(x * x, axis=-1, keepdims=True)
    return x * lax.rsqrt(ms + EPS) * g


def _proj_kernel(x_ref, g_ref, w_ref, tab_ref, p_ref, *refs):
    n_lay = len(QKV_LAYOUTS)
    outs = [refs[n_lay * i:n_lay * (i + 1)] for i in range(3)]
    stage = refs[3 * n_lay:]
    rows = PROJ_ROWS
    bf16 = jnp.bfloat16
    h = _rms(x_ref[...], g_ref[...]).astype(bf16)
    proj = jnp.dot(h, w_ref[...], preferred_element_type=jnp.float32)
    p_ref[...] = proj[:, :POOL_WIDTH]
    cos, sin_lo, sin_hi = tab_ref[0], tab_ref[1], tab_ref[2]

    def rope(t):
        return t * cos + pltpu.roll(t, LANES - 8, 1) * sin_lo + pltpu.roll(t, 8, 1) * sin_hi

    def emit(val, out_refs, g, slab_a, slab_b):
        o4, o16 = out_refs
        slab_a[...] = val
        for r_lo in range(4):
            plane = slab_a[pl.ds(r_lo, rows // 4, stride=4), :]
            o4[g, :, pl.ds(r_lo * LANES, LANES)] = plane.astype(bf16)
            slab_b[pl.ds(r_lo * (rows // 4), rows // 4), :] = plane
        for r_lo in range(4):
            for r_hi in range(4):
                piece = slab_b[pl.ds(r_lo * (rows // 4) + r_hi, rows // 16, stride=4), :]
                o16[g, :, pl.ds((4 * r_hi + r_lo) * LANES, LANES)] = piece.astype(bf16)

    for g in range(N_LANE_GROUPS):
        lo = POOL_WIDTH + g * LANES
        emit(rope(proj[:, lo:lo + LANES]) * Q_SCALE, outs[0], g, stage[6 * g], stage[6 * g + 1])
        lo += ATTN_WIDTH
        emit(rope(proj[:, lo:lo + LANES]), outs[1], g, stage[6 * g + 2], stage[6 * g + 3])
        lo += ATTN_WIDTH
        emit(proj[:, lo:lo + LANES], outs[2], g, stage[6 * g + 4], stage[6 * g + 5])


def _rope_tables(seq):
    pos = jnp.arange(seq, dtype=jnp.float32)
    inv_freq = ROPE_THETA ** (-jnp.arange(0, ROPE_DIM, 2, dtype=jnp.float32) / ROPE_DIM)
    half = ROPE_DIM // 2
    dim = jnp.arange(LANES) % HEAD_DIM
    ang = pos[:, None] * inv_freq[dim % half][None, :]
    cos, sin = jnp.cos(ang), jnp.sin(ang)
    first_half, rotated = (dim < half)[None, :], (dim < ROPE_DIM)[None, :]
    c = jnp.where(rotated, cos, 1.0)
    s1 = jnp.where(first_half, -sin, 0.0)
    s2 = jnp.where(rotated & ~first_half, sin, 0.0)
    return jnp.stack([c, s1, s2])


def _proj_call(x, g, w_bf16, tabs, seq):
    n = x.shape[0]
    t = PROJ_ROWS
    blocks_per_seq = seq // t
    qkv_shapes = [jax.ShapeDtypeStruct((N_LANE_GROUPS, n // d, d * LANES), jnp.bfloat16)
                  for d in QKV_LAYOUTS] * 3
    qkv_specs = [pl.BlockSpec((N_LANE_GROUPS, t // d, d * LANES), lambda i: (0, i, 0))
                 for d in QKV_LAYOUTS] * 3
    outs = pl.pallas_call(
        _proj_kernel,
        grid=(n // t,),
        in_specs=[
            pl.BlockSpec((t, D_MODEL), lambda i: (i, 0)),
            pl.BlockSpec((1, D_MODEL), lambda i: (0, 0)),
            pl.BlockSpec((D_MODEL, IN_WIDTH), lambda i: (0, 0)),
            pl.BlockSpec((3, t, LANES), lambda i: (0, i % blocks_per_seq, 0)),
        ],
        out_specs=[pl.BlockSpec((t, POOL_WIDTH), lambda i: (i, 0)), *qkv_specs],
        out_shape=[jax.ShapeDtypeStruct((n, POOL_WIDTH), jnp.float32), *qkv_shapes],
        scratch_shapes=[pltpu.VMEM((t, LANES), jnp.float32)] * (2 * 3 * N_LANE_GROUPS),
        compiler_params=pltpu.CompilerParams(
            dimension_semantics=("arbitrary",), vmem_limit_bytes=VMEM_LIMIT),
        name="proj",
    )(x, g, w_bf16, tabs)
    n_lay = len(QKV_LAYOUTS)
    return outs[0], outs[1:1 + n_lay], outs[1 + n_lay:1 + 2 * n_lay], outs[1 + 2 * n_lay:]


def _attn_kernel(*refs, n_blocks):
    n_lay = len(QKV_LAYOUTS)
    q_refs = dict(zip(QKV_LAYOUTS, refs[:n_lay]))
    k_refs = {d: refs[n_lay + 3 * i:n_lay + 3 * i + 3] for i, d in enumerate(QKV_LAYOUTS)}
    v_refs = {d: refs[4 * n_lay + 3 * i:4 * n_lay + 3 * i + 3] for i, d in enumerate(QKV_LAYOUTS)}
    o_ref, tbuf, mpat, mall, opat, lpat, bias_ref = refs[7 * n_lay:]
    t = pl.program_id(2)
    rows = ATTN_ROWS
    plane = rows // 4
    f32 = jnp.float32
    bf16 = jnp.bfloat16

    @pl.when((pl.program_id(0) == 0) & (pl.program_id(1) == 0) & (t == 0))
    def _():
        row = lax.broadcasted_iota(jnp.int32, (2 * Q_CHUNK, K_CHUNK), 0) % Q_CHUNK
        col = lax.broadcasted_iota(jnp.int32, (2 * Q_CHUNK, K_CHUNK), 1)
        q_rows, k_rows = Q_CHUNK // 4, K_CHUNK // 4
        delta = 4 * (col % k_rows - row % q_rows) + (col // k_rows - row // q_rows)
        families = (
            ((col >= row) & (col <= row + 2 * BAND_RADIUS), col, BAND_RADIUS, BAND_RADIUS + Q_CHUNK),
            ((delta >= 0) & (delta <= 2 * BAND_RADIUS), col % k_rows, BAND_RADIUS // 4,
             (BAND_RADIUS + Q_CHUNK) // 4),
        )
        for fam, (band, key_pos, lo, hi) in enumerate(families):
            for variant in range(4):
                ok = band
                if variant & 1:
                    ok = ok & (key_pos >= lo)
                if variant & 2:
                    ok = ok & (key_pos < hi)
                bias_ref[4 * fam + variant] = jnp.where(ok, 0.0, NEG_BIG).astype(f32)

    lane = lax.broadcasted_iota(jnp.int32, (Q_CHUNK, LANES), 1)
    first_head = lane < HEAD_DIM

    def pick_head(x):
        return jnp.where(first_head, x[:Q_CHUNK], x[Q_CHUNK:])

    def ext_rows(trio, lo, hi, block):
        prev_ref, main_ref, next_ref = trio
        lanes = pl.ds(block * LANES, LANES)
        n_main, n_halo = main_ref.shape[0], prev_ref.shape[0]
        parts = []
        if lo < 0:
            parts.append(prev_ref[pl.ds(n_halo + lo, -lo), lanes])
        parts.append(main_ref[pl.ds(max(lo, 0), min(hi, n_main) - max(lo, 0)), lanes])
        if hi > n_main:
            parts.append(next_ref[pl.ds(0, hi - n_main), lanes])
        return parts

    def rows_of(parts):
        return parts[0] if len(parts) == 1 else jnp.concatenate(parts, axis=0)

    tiles = []
    q_rows, k_rows = Q_CHUNK // 4, K_CHUNK // 4
    for c in range(rows // Q_CHUNK):
        lo = c * q_rows - BAND_RADIUS // 4
        q_parts = [q_refs[4][pl.ds(c * q_rows, q_rows), pl.ds(j * LANES, LANES)] for j in range(4)]
        keys = lambda trios, lo=lo: rows_of(
            [p for j in range(4) for p in ext_rows(trios[4], lo, lo + k_rows, j)])
        acc = [(j * q_rows, pl.ds(j * plane + c * q_rows, q_rows)) for j in range(4)]
        tiles.append((0, q_parts, keys, 1, c == 0, c == rows // Q_CHUNK - 1, acc))
    for r in range(4):
        for c in range(plane // Q_CHUNK):
            lo = c * Q_CHUNK - BAND_RADIUS
            q_parts = [q_refs[4][pl.ds(c * Q_CHUNK, Q_CHUNK), pl.ds(r * LANES, LANES)]]
            keys = lambda trios, lo=lo, r=r: rows_of(ext_rows(trios[4], lo, lo + K_CHUNK, r))
            acc = [(0, pl.ds(r * plane + c * Q_CHUNK, Q_CHUNK))]
            tiles.append((1, q_parts, keys, 0, c == 0, c == plane // Q_CHUNK - 1, acc))
    for r in range(16):
        q_parts = [q_refs[16][:, pl.ds(r * LANES, LANES)]]
        keys = lambda trios, r=r: rows_of(ext_rows(trios[16], -BAND_RADIUS, Q_CHUNK + BAND_RADIUS, r))
        acc = [(0, pl.ds((r % 4) * plane + r // 4, Q_CHUNK, stride=4))]
        tiles.append((2, q_parts, keys, 0, True, True, acc))

    def bias_index(family, at_start, at_end):
        if n_blocks == 1:
            return 4 * family + int(at_start) + 2 * int(at_end)
        first = (t == 0).astype(jnp.int32) if at_start else 0
        last = (t == n_blocks - 1).astype(jnp.int32) if at_end else 0
        return 4 * family + first + 2 * last

    def store_rows(ref, lead, acc, x):
        for start, idx in acc:
            ref[(*lead, idx, slice(None))] = x[start:start + idx.size]

    def load_rows(ref, lead, acc):
        return rows_of([ref[(*lead, idx, slice(None))] for _, idx in acc])

    dyn_zero = jnp.minimum(t, 0)

    for n, (pi, q_parts, keys, family, at_start, at_end, acc) in sorted(
            enumerate(tiles), key=lambda item: -item[1][0]):
        qc = rows_of(q_parts)
        zero = jnp.zeros_like(qc)
        q2 = jnp.concatenate([jnp.where(first_head, qc, zero), jnp.where(first_head, zero, qc)], axis=0)
        s = lax.dot_general(q2, keys(k_refs), (((1,), (1,)), ((), ())), preferred_element_type=f32)
        s = s + bias_ref[bias_index(family, at_start, at_end)]
        tbuf[n + dyn_zero] = s
        m = jnp.broadcast_to(jnp.max(s, axis=-1, keepdims=True), (2 * Q_CHUNK, LANES))
        for h in range(2):
            store_rows(mpat, (pi, h), acc, m[h * Q_CHUNK:(h + 1) * Q_CHUNK])

    for i in range(plane // q_rows):
        for r in range(4):
            sl = pl.ds(r * plane + i * q_rows, q_rows)
            for h in range(2):
                mall[h, sl, :] = jnp.maximum(jnp.maximum(mpat[0, h, sl, :], mpat[1, h, sl, :]),
                                             mpat[2, h, sl, :])

    ones = jnp.ones((K_CHUNK, LANES), bf16)
    for n, (pi, q_parts, keys, family, at_start, at_end, acc) in enumerate(tiles):
        mb = jnp.concatenate([load_rows(mall, (0,), acc), load_rows(mall, (1,), acc)], axis=0)
        p = jnp.exp2(tbuf[n + dyn_zero] - jnp.concatenate([mb, mb], axis=1)).astype(bf16)
        vc = jnp.concatenate([keys(v_refs), ones], axis=1)
        ol = jnp.dot(p, vc, preferred_element_type=f32)
        store_rows(opat, (pi,), acc, pick_head(ol[:, :LANES]))
        store_rows(lpat, (pi,), acc, pick_head(ol[:, LANES:]))

    pass_rows = 64
    for r in range(4):
        for i in range(plane // pass_rows):
            sl = pl.ds(r * plane + i * pass_rows, pass_rows)
            num = opat[0, sl, :] + opat[1, sl, :] + opat[2, sl, :]
            den = lpat[0, sl, :] + lpat[1, sl, :] + lpat[2, sl, :]
            o_ref[pl.ds(r + 4 * i * pass_rows, pass_rows, stride=4), :] = num / den


def _attn_call(q, k, v, batch, seq):
    rows = ATTN_ROWS
    n_blocks = seq // rows
    radius = BAND_RADIUS

    def view(a, d):
        return a.reshape(N_LANE_GROUPS, batch, seq // d, d * LANES)

    operands, in_specs = [], []
    for d, a in zip(QKV_LAYOUTS, q):
        operands.append(view(a, d))
        in_specs.append(pl.BlockSpec((None, None, rows // d, d * LANES), lambda b, g, t: (g, b, t, 0)))
    for arrs in (k, v):
        for d, a in zip(QKV_LAYOUTS, arrs):
            halo_per_block = rows // d // radius
            n_halo = seq // d // radius
            operands += [view(a, d)] * 3
            in_specs += [
                pl.BlockSpec((None, None, radius, d * LANES),
                             lambda b, g, t, hb=halo_per_block: (g, b, jnp.maximum(t * hb - 1, 0), 0)),
                pl.BlockSpec((None, None, rows // d, d * LANES), lambda b, g, t: (g, b, t, 0)),
                pl.BlockSpec((None, None, radius, d * LANES),
                             lambda b, g, t, hb=halo_per_block, nh=n_halo:
                             (g, b, jnp.minimum((t + 1) * hb, nh - 1), 0)),
            ]
    n_tiles = len(DILATIONS) * rows // Q_CHUNK
    return pl.pallas_call(
        functools.partial(_attn_kernel, n_blocks=n_blocks),
        grid=(batch, N_LANE_GROUPS, n_blocks),
        in_specs=in_specs,
        out_specs=pl.BlockSpec((None, None, rows, LANES), lambda b, g, t: (g, b, t, 0)),
        out_shape=jax.ShapeDtypeStruct((N_LANE_GROUPS, batch, seq, LANES), jnp.float32),
        scratch_shapes=[
            pltpu.VMEM((n_tiles, 2 * Q_CHUNK, K_CHUNK), jnp.float32),
            pltpu.VMEM((len(DILATIONS), 2, rows, LANES), jnp.float32),
            pltpu.VMEM((2, rows, LANES), jnp.float32),
            pltpu.VMEM((len(DILATIONS), rows, LANES), jnp.float32),
            pltpu.VMEM((len(DILATIONS), rows, LANES), jnp.float32),
            pltpu.VMEM((8, 2 * Q_CHUNK, K_CHUNK), jnp.float32),
        ],
        compiler_params=pltpu.CompilerParams(
            dimension_semantics=("arbitrary", "arbitrary", "arbitrary"), vmem_limit_bytes=VMEM_LIMIT),
        name="attn",
    )(*operands)


def _mix_kernel(x_ref, p_ref, pprev_ref, pnext_ref, a_ref, pw_ref, ps_ref, gp_ref, ga_ref,
                wo_ref, gpost_ref, o_ref, pbuf, mixed, *, seq):
    rows = MIX_ROWS
    f32 = jnp.float32
    i = pl.program_id(0)
    pos0 = (i * rows) % seq
    zero_rows = jnp.zeros((HALO_ROWS, POOL_WIDTH), f32)
    pbuf[pl.ds(0, HALO_ROWS), :] = jnp.where(pos0 > 0, pprev_ref[...], zero_rows)
    pbuf[pl.ds(HALO_ROWS, rows), :] = p_ref[...]
    pbuf[pl.ds(HALO_ROWS + rows, HALO_ROWS), :] = jnp.where(pos0 + rows < seq, pnext_ref[...], zero_rows)

    pos = pos0 + lax.broadcasted_iota(jnp.int32, (rows, 1), 0)
    lane = lax.broadcasted_iota(jnp.int32, (rows, LANES), 1)
    means = []
    for half in range(2):
        w_small, w_big = POOL_WINDOWS[2 * half], POOL_WINDOWS[2 * half + 1]
        lanes = pl.ds(half * LANES, LANES)

        def wsum(lo, hi):
            acc = pbuf[pl.ds(HALO_ROWS + lo, rows), lanes]
            for off in range(lo + 1, hi):
                acc = acc + pbuf[pl.ds(HALO_ROWS + off, rows), lanes]
            return acc

        if half == 0:
            small = wsum(-(w_small // 2), w_small // 2)
            big = small + wsum(-(w_big // 2), -(w_small // 2)) + wsum(w_small // 2, w_big // 2)
        else:
            n_buf = rows + 2 * HALO_ROWS
            a = pbuf[:, lanes]
            for step in (1, 2, 4):
                a = a + pltpu.roll(a, n_buf - step, 0)
            small = pltpu.roll(a, w_small // 2, 0)[HALO_ROWS:HALO_ROWS + rows]
            big = (a + pltpu.roll(a, n_buf - w_small, 0))[:rows]

        def count(w):
            return (jnp.minimum(pos + w // 2, seq) - jnp.maximum(pos - w // 2, 0)).astype(f32)

        first = lane < POOL_GROUP_DIM
        total = jnp.where(first, small, big)
        cnt = jnp.where(first, count(w_small), count(w_big))
        means.append(total / cnt - pbuf[pl.ds(HALO_ROWS, rows), lanes])
    pooled = jnp.concatenate(means, axis=-1).astype(jnp.bfloat16)
    pool_out = jnp.dot(pooled, pw_ref[...], preferred_element_type=f32) * ps_ref[...]
    mixed[:, pl.ds(0, POOL_WIDTH)] = _rms(pool_out, gp_ref[...]).astype(jnp.bfloat16)

    sq = None
    for g in range(N_LANE_GROUPS):
        a = a_ref[g]
        sq = a * a if sq is None else sq + a * a
    inv = lax.rsqrt(jnp.sum(sq, axis=-1, keepdims=True) / ATTN_WIDTH + EPS)
    for g in range(N_LANE_GROUPS):
        ga = ga_ref[:, pl.ds(g * LANES, LANES)]
        mixed[:, pl.ds(POOL_WIDTH + g * LANES, LANES)] = (a_ref[g] * inv * ga).astype(jnp.bfloat16)

    y = jnp.dot(mixed[...], wo_ref[...], preferred_element_type=f32)
    o_ref[...] = x_ref[...] + _rms(y, gpost_ref[...])


def _mix_call(x, p_in, attn, pool_bd, pool_scale, g_pool, g_attn, w_out_bf16, g_post, seq):
    n = x.shape[0]
    rows = MIX_ROWS
    hb = rows // HALO_ROWS
    n_halo_blocks = n // HALO_ROWS
    const = lambda i: (0, 0)
    return pl.pallas_call(
        functools.partial(_mix_kernel, seq=seq),
        grid=(n // rows,),
        in_specs=[
            pl.BlockSpec((rows, D_MODEL), lambda i: (i, 0)),
            pl.BlockSpec((rows, POOL_WIDTH), lambda i: (i, 0)),
            pl.BlockSpec((HALO_ROWS, POOL_WIDTH), lambda i: (jnp.maximum(i * hb - 1, 0), 0)),
            pl.BlockSpec((HALO_ROWS, POOL_WIDTH),
                         lambda i: (jnp.minimum((i + 1) * hb, n_halo_blocks - 1), 0)),
            pl.BlockSpec((N_LANE_GROUPS, rows, LANES), lambda i: (0, i, 0)),
            pl.BlockSpec((POOL_WIDTH, POOL_WIDTH), const),
            pl.BlockSpec((1, POOL_WIDTH), const),
            pl.BlockSpec((1, POOL_WIDTH), const),
            pl.BlockSpec((1, ATTN_WIDTH), const),
            pl.BlockSpec((D_MODEL, D_MODEL), const),
            pl.BlockSpec((1, D_MODEL), const),
        ],
        out_specs=pl.BlockSpec((rows, D_MODEL), lambda i: (i, 0)),
        out_shape=jax.ShapeDtypeStruct((n, D_MODEL), jnp.float32),
        scratch_shapes=[
            pltpu.VMEM((rows + 2 * HALO_ROWS, POOL_WIDTH), jnp.float32),
            pltpu.VMEM((rows, D_MODEL), jnp.bfloat16),
        ],
        compiler_params=pltpu.CompilerParams(
            dimension_semantics=("arbitrary",), vmem_limit_bytes=VMEM_LIMIT),
        name="mix",
    )(x, p_in, p_in, p_in, attn, pool_bd, pool_scale, g_pool, g_attn, w_out_bf16, g_post)


def _ffn_kernel(x_ref, xprev_ref, xnext_ref, gpre_ref, wup_ref, cw_ref, cb_ref, wdn_ref, gpost_ref,
                o_ref, xs, hbuf, *bufs, seq):
    n_chunks = D_FF // FFN_CHUNK
    ubufs, abufs = bufs[:FFN_U_RING], bufs[FFN_U_RING:]
    rows = FFN_ROWS
    run = rows // HALO_ROWS
    n_slabs = D_MODEL // LANES
    f32 = jnp.float32
    i = pl.program_id(0)
    pos0 = (i * rows) % seq
    g = gpre_ref[...]

    for l in range(n_slabs):
        for s in range(HALO_ROWS):
            xs[l, pl.ds(s, run, stride=HALO_ROWS), :] = x_ref[pl.ds(s * run, run), pl.ds(l * LANES, LANES)]
    sq = None
    for l in range(n_slabs):
        xl = xs[l]
        sq = xl * xl if sq is None else sq + xl * xl
    inv = lax.rsqrt(jnp.sum(sq, axis=-1, keepdims=True) / D_MODEL + EPS)
    for l in range(n_slabs):
        lanes = pl.ds(l * LANES, LANES)
        hbuf[pl.ds(HALO_ROWS, rows), lanes] = xs[l] * inv * gpre_ref[:, lanes]
    hprev = jnp.where(pos0 > 0, _rms(xprev_ref[...], g), 0.0)
    hnext = jnp.where(pos0 + rows < seq, _rms(xnext_ref[...], g), 0.0)
    hbuf[pl.ds(0, HALO_ROWS), :] = jnp.broadcast_to(hprev[HALO_ROWS - 1:], (HALO_ROWS, D_MODEL))
    hbuf[pl.ds(HALO_ROWS + rows, HALO_ROWS), :] = jnp.broadcast_to(hnext[:1], (HALO_ROWS, D_MODEL))
    h = hbuf[...].astype(jnp.bfloat16)

    sub = lax.broadcasted_iota(jnp.int32, (HALO_ROWS, FFN_CHUNK), 0)

    def conv(c, half):
        col = half * D_FF + c * FFN_CHUNK
        w = cw_ref[:, pl.ds(col, FFN_CHUNK)]
        acc = cb_ref[:, pl.ds(col, FFN_CHUNK)]
        for j in range(3):
            u = ubufs[c % FFN_U_RING][pl.ds(HALO_ROWS * j, rows), pl.ds(half * FFN_CHUNK, FFN_CHUNK)]
            acc = acc + u * w[j:j + 1]
        return acc

    def up(c):
        buf = ubufs[c % FFN_U_RING]
        for half in range(2):
            col = half * D_FF + c * FFN_CHUNK
            cols = pl.ds(half * FFN_CHUNK, FFN_CHUNK)
            buf[:, cols] = jnp.dot(h, wup_ref[:, pl.ds(col, FFN_CHUNK)], preferred_element_type=f32)
            halo = jnp.broadcast_to(buf[pl.ds(0, 1), cols], (HALO_ROWS, FFN_CHUNK))
            last = pltpu.roll(buf[pl.ds(rows, HALO_ROWS), cols], 1, 0)
            buf[pl.ds(0, HALO_ROWS), cols] = jnp.where(sub == 0, halo, last)
            halo = jnp.broadcast_to(buf[pl.ds(rows + HALO_ROWS, 1), cols], (HALO_ROWS, FFN_CHUNK))
            first = pltpu.roll(buf[pl.ds(HALO_ROWS, HALO_ROWS), cols], HALO_ROWS - 1, 0)
            buf[pl.ds(rows + HALO_ROWS, HALO_ROWS), cols] = jnp.where(sub == HALO_ROWS - 1, halo, first)

    def down(g, n_in_group):
        width = n_in_group * FFN_CHUNK
        w = wdn_ref[pl.ds(g * FFN_DOWN_GROUP * FFN_CHUNK, width), :]
        return jnp.dot(abufs[g][:, pl.ds(0, width)], w, preferred_element_type=f32)

    for c in range(min(FFN_LOOKAHEAD, n_chunks)):
        up(c)
    y = None
    for c in range(n_chunks):
        if c + FFN_LOOKAHEAD < n_chunks:
            up(c + FFN_LOOKAHEAD)
        gate = conv(c, 0)
        val = conv(c, 1)
        gelu = 0.5 * gate * (1.0 + lax.erf(gate * (2.0 ** -0.5)))
        g, j = divmod(c, FFN_DOWN_GROUP)
        abufs[g][:, pl.ds(j * FFN_CHUNK, FFN_CHUNK)] = (gelu * val).astype(jnp.bfloat16)
        if j + 1 == FFN_DOWN_GROUP and c + 1 < n_chunks:
            part = down(g, j + 1)
            y = part if y is None else y + part

    g, n_last = divmod(n_chunks - 1, FFN_DOWN_GROUP)
    width = (n_last + 1) * FFN_CHUNK
    w_last = wdn_ref[pl.ds(g * FFN_DOWN_GROUP * FFN_CHUNK, width), :]
    blk = rows // FFN_TAIL_BLOCKS
    blk_run = run // FFN_TAIL_BLOCKS
    for b in range(FFN_TAIL_BLOCKS):
        rs = pl.ds(b * blk, blk)
        yb = jnp.dot(abufs[g][rs, pl.ds(0, width)], w_last, preferred_element_type=f32)
        if y is not None:
            yb = yb + y[b * blk:(b + 1) * blk]
        res = _rms(yb, gpost_ref[...])
        for l in range(n_slabs):
            xs[l, rs, :] = xs[l, rs, :] + res[:, l * LANES:(l + 1) * LANES]
        for l in range(n_slabs):
            for s in range(HALO_ROWS):
                o_ref[pl.ds(s * run + b * blk_run, blk_run), pl.ds(l * LANES, LANES)] = (
                    xs[l, pl.ds(b * blk + s, blk_run, stride=HALO_ROWS), :])


def _ffn_call(x1, g_pre, w_up_bf16, conv_w, conv_b, w_down_bf16, g_post, seq):
    n = x1.shape[0]
    rows = FFN_ROWS
    hb = rows // HALO_ROWS
    n_halo_blocks = n // HALO_ROWS
    const = lambda i: (0, 0)
    return pl.pallas_call(
        functools.partial(_ffn_kernel, seq=seq),
        grid=(n // rows,),
        in_specs=[
            pl.BlockSpec((rows, D_MODEL), lambda i: (i, 0)),
            pl.BlockSpec((HALO_ROWS, D_MODEL), lambda i: (jnp.maximum(i * hb - 1, 0), 0)),
            pl.BlockSpec((HALO_ROWS, D_MODEL),
                         lambda i: (jnp.minimum((i + 1) * hb, n_halo_blocks - 1), 0)),
            pl.BlockSpec((1, D_MODEL), const),
            pl.BlockSpec((D_MODEL, 2 * D_FF), const, pipeline_mode=pl.Buffered(1)),
            pl.BlockSpec((3, 2 * D_FF), const),
            pl.BlockSpec((1, 2 * D_FF), const),
            pl.BlockSpec((D_FF, D_MODEL), const, pipeline_mode=pl.Buffered(1)),
            pl.BlockSpec((1, D_MODEL), const),
        ],
        out_specs=pl.BlockSpec((rows, D_MODEL), lambda i: (i, 0)),
        out_shape=jax.ShapeDtypeStruct((n, D_MODEL), jnp.float32),
        scratch_shapes=[
            pltpu.VMEM((D_MODEL // LANES, rows, LANES), jnp.float32),
            pltpu.VMEM((rows + 2 * HALO_ROWS, D_MODEL), jnp.float32),
            *[pltpu.VMEM((rows + 2 * HALO_ROWS, 2 * FFN_CHUNK), jnp.float32)] * FFN_U_RING,
            *[pltpu.VMEM((rows, FFN_DOWN_GROUP * FFN_CHUNK), jnp.bfloat16)]
            * pl.cdiv(D_FF // FFN_CHUNK, FFN_DOWN_GROUP),
        ],
        compiler_params=pltpu.CompilerParams(
            dimension_semantics=("arbitrary",), vmem_limit_bytes=VMEM_LIMIT),
        name="ffn",
    )(x1, x1, x1, g_pre, w_up_bf16, conv_w, conv_b, w_down_bf16, g_post)


def _block_diag(pool_w):
    groups = pool_w.shape[0]
    bd = jnp.zeros((POOL_WIDTH, POOL_WIDTH), pool_w.dtype)
    for g in range(groups):
        lo = g * POOL_GROUP_DIM
        bd = bd.at[lo:lo + POOL_GROUP_DIM, lo:lo + POOL_GROUP_DIM].set(pool_w[g])
    return bd


def _layer(x, params):
    (g_mix_pre, g_mix_post, w_in, pool_bd, pool_scale, g_pool_out, g_attn_out, w_out,
     g_ffn_pre, g_ffn_post, w_up, conv_w, conv_b, w_down) = params
    batch, seq, _ = x.shape
    assert seq % ATTN_ROWS == 0 and seq % PROJ_ROWS == 0 and seq % MIX_ROWS == 0 and seq % FFN_ROWS == 0
    xf = x.reshape(batch * seq, D_MODEL)
    p_in, q, k, v = _proj_call(xf, g_mix_pre, w_in, _rope_tables(seq), seq)
    attn = _attn_call(q, k, v, batch, seq).reshape(N_LANE_GROUPS, batch * seq, LANES)
    x1 = _mix_call(xf, p_in, attn, pool_bd, pool_scale, g_pool_out, g_attn_out, w_out, g_mix_post, seq)
    y = _ffn_call(x1, g_ffn_pre, w_up, conv_w, conv_b, w_down, g_ffn_post, seq)
    return y.reshape(batch, seq, D_MODEL)


def kernel(x_prompt, x_sample, g_mix_pre, g_mix_post, w_in, pool_w, pool_scale, g_pool_out, g_attn_out,
           w_out, g_ffn_pre, g_ffn_post, w_up, conv_w, conv_b, w_down):
    depth = w_in.shape[0]
    bf16 = jnp.bfloat16

    def layer_params(l):
        return (g_mix_pre[l][None], g_mix_post[l][None], w_in[l].astype(bf16),
                _block_diag(pool_w[l]).astype(bf16), pool_scale[l][None], g_pool_out[l][None],
                g_attn_out[l][None], w_out[l].astype(bf16), g_ffn_pre[l][None], g_ffn_post[l][None],
                w_up[l].astype(bf16), conv_w[l], conv_b[l][None], w_down[l].astype(bf16))

    params = [layer_params(l) for l in range(depth)]

    def run(x):
        for p in params:
            x = _layer(x, p)
        return x

    return (run(x_prompt), run(x_sample))
```

```python
import functools
import math

import jax
import jax.numpy as jnp
from jax import lax
from jax.experimental import pallas as pl
from jax.experimental.pallas import tpu as pltpu

D_MODEL = 1024
POOL_WIDTH = 256
POOL_WINDOWS = (2, 4, 8, 16)
POOL_GROUP_DIM = 64
ATTN_WIDTH = 768
HEAD_DIM = 64
DILATIONS = (1, 4, 16)
QKV_LAYOUTS = (4, 16)
BAND_RADIUS = 64
ROPE_THETA = 500000.0
ROPE_DIM = 16
D_FF = 2816
EPS = 1e-6
IN_WIDTH = POOL_WIDTH + 3 * ATTN_WIDTH

LANES = 128
N_LANE_GROUPS = ATTN_WIDTH // LANES
NEG_BIG = -1e30
Q_SCALE = HEAD_DIM ** -0.5 * math.log2(math.e)

PROJ_ROWS = 512
ATTN_ROWS = 2048
Q_CHUNK = 128
K_CHUNK = Q_CHUNK + 2 * BAND_RADIUS
MIX_ROWS = 1024
FFN_ROWS = 512
FFN_CHUNK = 256
FFN_LOOKAHEAD = 3
FFN_U_RING = FFN_LOOKAHEAD + 1
FFN_DOWN_GROUP = 4
FFN_TAIL_BLOCKS = 4
HALO_ROWS = 8

V7X_VMEM_BYTES = 64 * 1024 * 1024
VMEM_LIMIT = V7X_VMEM_BYTES - 8 * 1024 * 1024


def _rms(x, g):
    ms = jnp.mean(x * x, axis=-1, keepdims=True)
    return x * lax.rsqrt(ms + EPS) * g


def _proj_kernel(x_ref, g_ref, w_ref, tab_ref, p_ref, *refs):
    n_lay = len(QKV_LAYOUTS)
    outs = [refs[n_lay * i:n_lay * (i + 1)] for i in range(3)]
    stage = refs[3 * n_lay:]
    rows = PROJ_ROWS
    bf16 = jnp.bfloat16
    h = _rms(x_ref[...], g_ref[...]).astype(bf16)
    proj = jnp.dot(h, w_ref[...], preferred_element_type=jnp.float32)
    p_ref[...] = proj[:, :POOL_WIDTH]
    cos, sin_lo, sin_hi = tab_ref[0], tab_ref[1], tab_ref[2]

    def rope(t):
        return t * cos + pltpu.roll(t, LANES - 8, 1) * sin_lo + pltpu.roll(t, 8, 1) * sin_hi

    def emit(val, out_refs, g, slab_a, slab_b):
        o4, o16 = out_refs
        slab_a[...] = val
        for r_lo in range(4):
            plane = slab_a[pl.ds(r_lo, rows // 4, stride=4), :]
            o4[g, :, pl.ds(r_lo * LANES, LANES)] = plane.astype(bf16)
            slab_b[pl.ds(r_lo * (rows // 4), rows // 4), :] = plane
        for r_lo in range(4):
            for r_hi in range(4):
                piece = slab_b[pl.ds(r_lo * (rows // 4) + r_hi, rows // 16, stride=4), :]
                o16[g, :, pl.ds((4 * r_hi + r_lo) * LANES, LANES)] = piece.astype(bf16)

    for g in range(N_LANE_GROUPS):
        lo = POOL_WIDTH + g * LANES
        emit(rope(proj[:, lo:lo + LANES]) * Q_SCALE, outs[0], g, stage[6 * g], stage[6 * g + 1])
        lo += ATTN_WIDTH
        emit(rope(proj[:, lo:lo + LANES]), outs[1], g, stage[6 * g + 2], stage[6 * g + 3])
        lo += ATTN_WIDTH
        emit(proj[:, lo:lo + LANES], outs[2], g, stage[6 * g + 4], stage[6 * g + 5])


def _rope_tables(seq):
    pos = jnp.arange(seq, dtype=jnp.float32)
    inv_freq = ROPE_THETA ** (-jnp.arange(0, ROPE_DIM, 2, dtype=jnp.float32) / ROPE_DIM)
    half = ROPE_DIM // 2
    dim = jnp.arange(LANES) % HEAD_DIM
    ang = pos[:, None] * inv_freq[dim % half][None, :]
    cos, sin = jnp.cos(ang), jnp.sin(ang)
    first_half, rotated = (dim < half)[None, :], (dim < ROPE_DIM)[None, :]
    c = jnp.where(rotated, cos, 1.0)
    s1 = jnp.where(first_half, -sin, 0.0)
    s2 = jnp.where(rotated & ~first_half, sin, 0.0)
    return jnp.stack([c, s1, s2])


def _proj_call(x, g, w_bf16, tabs, seq):
    n = x.shape[0]
    t = PROJ_ROWS
    blocks_per_seq = seq // t
    qkv_shapes = [jax.ShapeDtypeStruct((N_LANE_GROUPS, n // d, d * LANES), jnp.bfloat16)
                  for d in QKV_LAYOUTS] * 3
    qkv_specs = [pl.BlockSpec((N_LANE_GROUPS, t // d, d * LANES), lambda i: (0, i, 0))
                 for d in QKV_LAYOUTS] * 3
    outs = pl.pallas_call(
        _proj_kernel,
        grid=(n // t,),
        in_specs=[
            pl.BlockSpec((t, D_MODEL), lambda i: (i, 0)),
            pl.BlockSpec((1, D_MODEL), lambda i: (0, 0)),
            pl.BlockSpec((D_MODEL, IN_WIDTH), lambda i: (0, 0)),
            pl.BlockSpec((3, t, LANES), lambda i: (0, i % blocks_per_seq, 0)),
        ],
        out_specs=[pl.BlockSpec((t, POOL_WIDTH), lambda i: (i, 0)), *qkv_specs],
        out_shape=[jax.ShapeDtypeStruct((n, POOL_WIDTH), jnp.float32), *qkv_shapes],
        scratch_shapes=[pltpu.VMEM((t, LANES), jnp.float32)] * (2 * 3 * N_LANE_GROUPS),
        compiler_params=pltpu.CompilerParams(
            dimension_semantics=("arbitrary",), vmem_limit_bytes=VMEM_LIMIT),
        name="proj",
    )(x, g, w_bf16, tabs)
    n_lay = len(QKV_LAYOUTS)
    return outs[0], outs[1:1 + n_lay], outs[1 + n_lay:1 + 2 * n_lay], outs[1 + 2 * n_lay:]


def _attn_kernel(*refs, n_blocks):
    n_lay = len(QKV_LAYOUTS)
    q_refs = dict(zip(QKV_LAYOUTS, refs[:n_lay]))
    k_refs = {d: refs[n_lay + 3 * i:n_lay + 3 * i + 3] for i, d in enumerate(QKV_LAYOUTS)}
    v_refs = {d: refs[4 * n_lay + 3 * i:4 * n_lay + 3 * i + 3] for i, d in enumerate(QKV_LAYOUTS)}
    o_ref, tbuf, mpat, mall, opat, lpat, bias_ref = refs[7 * n_lay:]
    t = pl.program_id(2)
    rows = ATTN_ROWS
    plane = rows // 4
    f32 = jnp.float32
    bf16 = jnp.bfloat16

    @pl.when((pl.program_id(0) == 0) & (pl.program_id(1) == 0) & (t == 0))
    def _():
        row = lax.broadcasted_iota(jnp.int32, (2 * Q_CHUNK, K_CHUNK), 0) % Q_CHUNK
        col = lax.broadcasted_iota(jnp.int32, (2 * Q_CHUNK, K_CHUNK), 1)
        q_rows, k_rows = Q_CHUNK // 4, K_CHUNK // 4
        delta = 4 * (col % k_rows - row % q_rows) + (col // k_rows - row // q_rows)
        families = (
            ((col >= row) & (col <= row + 2 * BAND_RADIUS), col, BAND_RADIUS, BAND_RADIUS + Q_CHUNK),
            ((delta >= 0) & (delta <= 2 * BAND_RADIUS), col % k_rows, BAND_RADIUS // 4,
             (BAND_RADIUS + Q_CHUNK) // 4),
        )
        for fam, (band, key_pos, lo, hi) in enumerate(families):
            for variant in range(4):
                ok = band
                if variant & 1:
                    ok = ok & (key_pos >= lo)
                if variant & 2:
                    ok = ok & (key_pos < hi)
                bias_ref[4 * fam + variant] = jnp.where(ok, 0.0, NEG_BIG).astype(f32)

    lane = lax.broadcasted_iota(jnp.int32, (Q_CHUNK, LANES), 1)
    first_head = lane < HEAD_DIM

    def pick_head(x):
        return jnp.where(first_head, x[:Q_CHUNK], x[Q_CHUNK:])

    def ext_rows(trio, lo, hi, block):
        prev_ref, main_ref, next_ref = trio
        lanes = pl.ds(block * LANES, LANES)
        n_main, n_halo = main_ref.shape[0], prev_ref.shape[0]
        parts = []
        if lo < 0:
            parts.append(prev_ref[pl.ds(n_halo + lo, -lo), lanes])
        parts.append(main_ref[pl.ds(max(lo, 0), min(hi, n_main) - max(lo, 0)), lanes])
        if hi > n_main:
            parts.append(next_ref[pl.ds(0, hi - n_main), lanes])
        return parts

    def rows_of(parts):
        return parts[0] if len(parts) == 1 else jnp.concatenate(parts, axis=0)

    tiles = []
    q_rows, k_rows = Q_CHUNK // 4, K_CHUNK // 4
    for c in range(rows // Q_CHUNK):
        lo = c * q_rows - BAND_RADIUS // 4
        q_parts = [q_refs[4][pl.ds(c * q_rows, q_rows), pl.ds(j * LANES, LANES)] for j in range(4)]
        keys = lambda trios, lo=lo: rows_of(
            [p for j in range(4) for p in ext_rows(trios[4], lo, lo + k_rows, j)])
        acc = [(j * q_rows, pl.ds(j * plane + c * q_rows, q_rows)) for j in range(4)]
        tiles.append((0, q_parts, keys, 1, c == 0, c == rows // Q_CHUNK - 1, acc))
    for r in range(4):
        for c in range(plane // Q_CHUNK):
            lo = c * Q_CHUNK - BAND_RADIUS
            q_parts = [q_refs[4][pl.ds(c * Q_CHUNK, Q_CHUNK), pl.ds(r * LANES, LANES)]]
            keys = lambda trios, lo=lo, r=r: rows_of(ext_rows(trios[4], lo, lo + K_CHUNK, r))
            acc = [(0, pl.ds(r * plane + c * Q_CHUNK, Q_CHUNK))]
            tiles.append((1, q_parts, keys, 0, c == 0, c == plane // Q_CHUNK - 1, acc))
    for r in (4 * r_hi + r_lo for r_lo in range(4) for r_hi in range(4)):
        q_parts = [q_refs[16][:, pl.ds(r * LANES, LANES)]]
        keys = lambda trios, r=r: rows_of(ext_rows(trios[16], -BAND_RADIUS, Q_CHUNK + BAND_RADIUS, r))
        acc = [(0, pl.ds((r % 4) * plane + r // 4, Q_CHUNK, stride=4))]
        tiles.append((2, q_parts, keys, 0, True, True, acc))

    def bias_index(family, at_start, at_end):
        if n_blocks == 1:
            return 4 * family + int(at_start) + 2 * int(at_end)
        first = (t == 0).astype(jnp.int32) if at_start else 0
        last = (t == n_blocks - 1).astype(jnp.int32) if at_end else 0
        return 4 * family + first + 2 * last

    def store_rows(ref, lead, acc, x):
        for start, idx in acc:
            ref[(*lead, idx, slice(None))] = x[start:start + idx.size]

    def load_rows(ref, lead, acc):
        return rows_of([ref[(*lead, idx, slice(None))] for _, idx in acc])

    dyn_zero = jnp.minimum(t, 0)

    for n, (pi, q_parts, keys, family, at_start, at_end, acc) in sorted(
            enumerate(tiles), key=lambda item: -item[1][0]):
        qc = rows_of(q_parts)
        zero = jnp.zeros_like(qc)
        q2 = jnp.concatenate([jnp.where(first_head, qc, zero), jnp.where(first_head, zero, qc)], axis=0)
        s = lax.dot_general(q2, keys(k_refs), (((1,), (1,)), ((), ())), preferred_element_type=f32)
        s = s + bias_ref[bias_index(family, at_start, at_end)]
        tbuf[n + dyn_zero] = s
        m = jnp.broadcast_to(jnp.max(s, axis=-1, keepdims=True), (2 * Q_CHUNK, LANES))
        for h in range(2):
            store_rows(mpat, (pi, h), acc, m[h * Q_CHUNK:(h + 1) * Q_CHUNK])

    for i in range(plane // q_rows):
        for r in range(4):
            sl = pl.ds(r * plane + i * q_rows, q_rows)
            for h in range(2):
                mall[h, sl, :] = jnp.maximum(jnp.maximum(mpat[0, h, sl, :], mpat[1, h, sl, :]),
                                             mpat[2, h, sl, :])

    ones = jnp.ones((K_CHUNK, LANES), bf16)
    for n, (pi, q_parts, keys, family, at_start, at_end, acc) in enumerate(tiles):
        mb = jnp.concatenate([load_rows(mall, (0,), acc), load_rows(mall, (1,), acc)], axis=0)
        p = jnp.exp2(tbuf[n + dyn_zero] - jnp.concatenate([mb, mb], axis=1)).astype(bf16)
        vc = jnp.concatenate([keys(v_refs), ones], axis=1)
        ol = jnp.dot(p, vc, preferred_element_type=f32)
        store_rows(opat, (pi,), acc, pick_head(ol[:, :LANES]))
        store_rows(lpat, (pi,), acc, pick_head(ol[:, LANES:]))

    pass_rows = 64
    for r in range(4):
        for i in range(plane // pass_rows):
            sl = pl.ds(r * plane + i * pass_rows, pass_rows)
            num = opat[0, sl, :] + opat[1, sl, :] + opat[2, sl, :]
            den = lpat[0, sl, :] + lpat[1, sl, :] + lpat[2, sl, :]
            o_ref[pl.ds(r + 4 * i * pass_rows, pass_rows, stride=4), :] = num / den


def _attn_call(q, k, v, batch, seq):
    rows = ATTN_ROWS
    n_blocks = seq // rows
    radius = BAND_RADIUS

    def view(a, d):
        return a.reshape(N_LANE_GROUPS, batch, seq // d, d * LANES)

    operands, in_specs = [], []
    for d, a in zip(QKV_LAYOUTS, q):
        operands.append(view(a, d))
        in_specs.append(pl.BlockSpec((None, None, rows // d, d * LANES), lambda b, g, t: (g, b, t, 0)))
    for arrs in (k, v):
        for d, a in zip(QKV_LAYOUTS, arrs):
            halo_per_block = rows // d // radius
            n_halo = seq // d // radius
            operands += [view(a, d)] * 3
            in_specs += [
                pl.BlockSpec((None, None, radius, d * LANES),
                             lambda b, g, t, hb=halo_per_block: (g, b, jnp.maximum(t * hb - 1, 0), 0)),
                pl.BlockSpec((None, None, rows // d, d * LANES), lambda b, g, t: (g, b, t, 0)),
                pl.BlockSpec((None, None, radius, d * LANES),
                             lambda b, g, t, hb=halo_per_block, nh=n_halo:
                             (g, b, jnp.minimum((t + 1) * hb, nh - 1), 0)),
            ]
    n_tiles = len(DILATIONS) * rows // Q_CHUNK
    return pl.pallas_call(
        functools.partial(_attn_kernel, n_blocks=n_blocks),
        grid=(batch, N_LANE_GROUPS, n_blocks),
        in_specs=in_specs,
        out_specs=pl.BlockSpec((None, None, rows, LANES), lambda b, g, t: (g, b, t, 0)),
        out_shape=jax.ShapeDtypeStruct((N_LANE_GROUPS, batch, seq, LANES), jnp.float32),
        scratch_shapes=[
            pltpu.VMEM((n_tiles, 2 * Q_CHUNK, K_CHUNK), jnp.float32),
            pltpu.VMEM((len(DILATIONS), 2, rows, LANES), jnp.float32),
            pltpu.VMEM((2, rows, LANES), jnp.float32),
            pltpu.VMEM((len(DILATIONS), rows, LANES), jnp.float32),
            pltpu.VMEM((len(DILATIONS), rows, LANES), jnp.float32),
            pltpu.VMEM((8, 2 * Q_CHUNK, K_CHUNK), jnp.float32),
        ],
        compiler_params=pltpu.CompilerParams(
            dimension_semantics=("arbitrary", "arbitrary", "arbitrary"), vmem_limit_bytes=VMEM_LIMIT),
        name="attn",
    )(*operands)


def _mix_kernel(x_ref, p_ref, pprev_ref, pnext_ref, a_ref, pw_ref, ps_ref, gp_ref, ga_ref,
                wo_ref, gpost_ref, o_ref, pbuf, mixed, *, seq):
    rows = MIX_ROWS
    f32 = jnp.float32
    i = pl.program_id(0)
    pos0 = (i * rows) % seq
    zero_rows = jnp.zeros((HALO_ROWS, POOL_WIDTH), f32)
    pbuf[pl.ds(0, HALO_ROWS), :] = jnp.where(pos0 > 0, pprev_ref[...], zero_rows)
    pbuf[pl.ds(HALO_ROWS, rows), :] = p_ref[...]
    pbuf[pl.ds(HALO_ROWS + rows, HALO_ROWS), :] = jnp.where(pos0 + rows < seq, pnext_ref[...], zero_rows)

    pos = pos0 + lax.broadcasted_iota(jnp.int32, (rows, 1), 0)
    lane = lax.broadcasted_iota(jnp.int32, (rows, LANES), 1)
    means = []
    for half in range(2):
        w_small, w_big = POOL_WINDOWS[2 * half], POOL_WINDOWS[2 * half + 1]
        lanes = pl.ds(half * LANES, LANES)

        def wsum(lo, hi):
            acc = pbuf[pl.ds(HALO_ROWS + lo, rows), lanes]
            for off in range(lo + 1, hi):
                acc = acc + pbuf[pl.ds(HALO_ROWS + off, rows), lanes]
            return acc

        if half == 0:
            small = wsum(-(w_small // 2), w_small // 2)
            big = small + wsum(-(w_big // 2), -(w_small // 2)) + wsum(w_small // 2, w_big // 2)
        else:
            n_buf = rows + 2 * HALO_ROWS
            a = pbuf[:, lanes]
            for step in (1, 2, 4):
                a = a + pltpu.roll(a, n_buf - step, 0)
            small = pltpu.roll(a, w_small // 2, 0)[HALO_ROWS:HALO_ROWS + rows]
            big = (a + pltpu.roll(a, n_buf - w_small, 0))[:rows]

        def count(w):
            return (jnp.minimum(pos + w // 2, seq) - jnp.maximum(pos - w // 2, 0)).astype(f32)

        first = lane < POOL_GROUP_DIM
        total = jnp.where(first, small, big)
        cnt = jnp.where(first, count(w_small), count(w_big))
        means.append(total / cnt - pbuf[pl.ds(HALO_ROWS, rows), lanes])
    pooled = jnp.concatenate(means, axis=-1).astype(jnp.bfloat16)
    pool_out = jnp.dot(pooled, pw_ref[...], preferred_element_type=f32) * ps_ref[...]
    mixed[:, pl.ds(0, POOL_WIDTH)] = _rms(pool_out, gp_ref[...]).astype(jnp.bfloat16)

    sq = None
    for g in range(N_LANE_GROUPS):
        a = a_ref[g]
        sq = a * a if sq is None else sq + a * a
    inv = lax.rsqrt(jnp.sum(sq, axis=-1, keepdims=True) / ATTN_WIDTH + EPS)
    for g in range(N_LANE_GROUPS):
        ga = ga_ref[:, pl.ds(g * LANES, LANES)]
        mixed[:, pl.ds(POOL_WIDTH + g * LANES, LANES)] = (a_ref[g] * inv * ga).astype(jnp.bfloat16)

    y = jnp.dot(mixed[...], wo_ref[...], preferred_element_type=f32)
    o_ref[...] = x_ref[...] + _rms(y, gpost_ref[...])


def _mix_call(x, p_in, attn, pool_bd, pool_scale, g_pool, g_attn, w_out_bf16, g_post, seq):
    n = x.shape[0]
    rows = MIX_ROWS
    hb = rows // HALO_ROWS
    n_halo_blocks = n // HALO_ROWS
    const = lambda i: (0, 0)
    return pl.pallas_call(
        functools.partial(_mix_kernel, seq=seq),
        grid=(n // rows,),
        in_specs=[
            pl.BlockSpec((rows, D_MODEL), lambda i: (i, 0)),
            pl.BlockSpec((rows, POOL_WIDTH), lambda i: (i, 0)),
            pl.BlockSpec((HALO_ROWS, POOL_WIDTH), lambda i: (jnp.maximum(i * hb - 1, 0), 0)),
            pl.BlockSpec((HALO_ROWS, POOL_WIDTH),
                         lambda i: (jnp.minimum((i + 1) * hb, n_halo_blocks - 1), 0)),
            pl.BlockSpec((N_LANE_GROUPS, rows, LANES), lambda i: (0, i, 0)),
            pl.BlockSpec((POOL_WIDTH, POOL_WIDTH), const),
            pl.BlockSpec((1, POOL_WIDTH), const),
            pl.BlockSpec((1, POOL_WIDTH), const),
            pl.BlockSpec((1, ATTN_WIDTH), const),
            pl.BlockSpec((D_MODEL, D_MODEL), const),
            pl.BlockSpec((1, D_MODEL), const),
        ],
        out_specs=pl.BlockSpec((rows, D_MODEL), lambda i: (i, 0)),
        out_shape=jax.ShapeDtypeStruct((n, D_MODEL), jnp.float32),
        scratch_shapes=[
            pltpu.VMEM((rows + 2 * HALO_ROWS, POOL_WIDTH), jnp.float32),
            pltpu.VMEM((rows, D_MODEL), jnp.bfloat16),
        ],
        compiler_params=pltpu.CompilerParams(
            dimension_semantics=("arbitrary",), vmem_limit_bytes=VMEM_LIMIT),
        name="mix",
    )(x, p_in, p_in, p_in, attn, pool_bd, pool_scale, g_pool, g_attn, w_out_bf16, g_post)


def _ffn_kernel(x_ref, xprev_ref, xnext_ref, gpre_ref, wup_ref, cw_ref, cb_ref, wdn_ref, gpost_ref,
                o_ref, xs, hbuf, *bufs, seq):
    n_chunks = D_FF // FFN_CHUNK
    ubufs, abufs = bufs[:FFN_U_RING], bufs[FFN_U_RING:]
    rows = FFN_ROWS
    run = rows // HALO_ROWS
    n_slabs = D_MODEL // LANES
    f32 = jnp.float32
    i = pl.program_id(0)
    pos0 = (i * rows) % seq
    g = gpre_ref[...]

    for l in range(n_slabs):
        for s in range(HALO_ROWS):
            xs[l, pl.ds(s, run, stride=HALO_ROWS), :] = x_ref[pl.ds(s * run, run), pl.ds(l * LANES, LANES)]
    sq = None
    for l in range(n_slabs):
        xl = xs[l]
        sq = xl * xl if sq is None else sq + xl * xl
    inv = lax.rsqrt(jnp.sum(sq, axis=-1, keepdims=True) / D_MODEL + EPS)
    for l in range(n_slabs):
        lanes = pl.ds(l * LANES, LANES)
        hbuf[pl.ds(HALO_ROWS, rows), lanes] = xs[l] * inv * gpre_ref[:, lanes]
    hprev = jnp.where(pos0 > 0, _rms(xprev_ref[...], g), 0.0)
    hnext = jnp.where(pos0 + rows < seq, _rms(xnext_ref[...], g), 0.0)
    hbuf[pl.ds(0, HALO_ROWS), :] = jnp.broadcast_to(hprev[HALO_ROWS - 1:], (HALO_ROWS, D_MODEL))
    hbuf[pl.ds(HALO_ROWS + rows, HALO_ROWS), :] = jnp.broadcast_to(hnext[:1], (HALO_ROWS, D_MODEL))
    h = hbuf[...].astype(jnp.bfloat16)

    sub = lax.broadcasted_iota(jnp.int32, (HALO_ROWS, FFN_CHUNK), 0)

    def conv(c, half):
        col = half * D_FF + c * FFN_CHUNK
        w = cw_ref[:, pl.ds(col, FFN_CHUNK)]
        acc = cb_ref[:, pl.ds(col, FFN_CHUNK)]
        for j in range(3):
            u = ubufs[c % FFN_U_RING][pl.ds(HALO_ROWS * j, rows), pl.ds(half * FFN_CHUNK, FFN_CHUNK)]
            acc = acc + u * w[j:j + 1]
        return acc

    def up(c):
        buf = ubufs[c % FFN_U_RING]
        for half in range(2):
            col = half * D_FF + c * FFN_CHUNK
            cols = pl.ds(half * FFN_CHUNK, FFN_CHUNK)
            buf[:, cols] = jnp.dot(h, wup_ref[:, pl.ds(col, FFN_CHUNK)], preferred_element_type=f32)
            halo = jnp.broadcast_to(buf[pl.ds(0, 1), cols], (HALO_ROWS, FFN_CHUNK))
            last = pltpu.roll(buf[pl.ds(rows, HALO_ROWS), cols], 1, 0)
            buf[pl.ds(0, HALO_ROWS), cols] = jnp.where(sub == 0, halo, last)
            halo = jnp.broadcast_to(buf[pl.ds(rows + HALO_ROWS, 1), cols], (HALO_ROWS, FFN_CHUNK))
            first = pltpu.roll(buf[pl.ds(HALO_ROWS, HALO_ROWS), cols], HALO_ROWS - 1, 0)
            buf[pl.ds(rows + HALO_ROWS, HALO_ROWS), cols] = jnp.where(sub == HALO_ROWS - 1, halo, first)

    def down(g, n_in_group):
        width = n_in_group * FFN_CHUNK
        w = wdn_ref[pl.ds(g * FFN_DOWN_GROUP * FFN_CHUNK, width), :]
        return jnp.dot(abufs[g][:, pl.ds(0, width)], w, preferred_element_type=f32)

    for c in range(min(FFN_LOOKAHEAD, n_chunks)):
        up(c)
    y = None
    for c in range(n_chunks):
        if c + FFN_LOOKAHEAD < n_chunks:
            up(c + FFN_LOOKAHEAD)
        gate = conv(c, 0)
        val = conv(c, 1)
        gelu = 0.5 * gate * (1.0 + lax.erf(gate * (2.0 ** -0.5)))
        g, j = divmod(c, FFN_DOWN_GROUP)
        abufs[g][:, pl.ds(j * FFN_CHUNK, FFN_CHUNK)] = (gelu * val).astype(jnp.bfloat16)
        if j + 1 == FFN_DOWN_GROUP and c + 1 < n_chunks:
            part = down(g, j + 1)
            y = part if y is None else y + part

    g, n_last = divmod(n_chunks - 1, FFN_DOWN_GROUP)
    width = (n_last + 1) * FFN_CHUNK
    w_last = wdn_ref[pl.ds(g * FFN_DOWN_GROUP * FFN_CHUNK, width), :]
    blk = rows // FFN_TAIL_BLOCKS
    blk_run = run // FFN_TAIL_BLOCKS
    for b in range(FFN_TAIL_BLOCKS):
        rs = pl.ds(b * blk, blk)
        yb = jnp.dot(abufs[g][rs, pl.ds(0, width)], w_last, preferred_element_type=f32)
        if y is not None:
            yb = yb + y[b * blk:(b + 1) * blk]
        res = _rms(yb, gpost_ref[...])
        for l in range(n_slabs):
            xs[l, rs, :] = xs[l, rs, :] + res[:, l * LANES:(l + 1) * LANES]
        for l in range(n_slabs):
            for s in range(HALO_ROWS):
                o_ref[pl.ds(s * run + b * blk_run, blk_run), pl.ds(l * LANES, LANES)] = (
                    xs[l, pl.ds(b * blk + s, blk_run, stride=HALO_ROWS), :])


def _ffn_call(x1, g_pre, w_up_bf16, conv_w, conv_b, w_down_bf16, g_post, seq):
    n = x1.shape[0]
    rows = FFN_ROWS
    hb = rows // HALO_ROWS
    n_halo_blocks = n // HALO_ROWS
    const = lambda i: (0, 0)
    return pl.pallas_call(
        functools.partial(_ffn_kernel, seq=seq),
        grid=(n // rows,),
        in_specs=[
            pl.BlockSpec((rows, D_MODEL), lambda i: (i, 0)),
            pl.BlockSpec((HALO_ROWS, D_MODEL), lambda i: (jnp.maximum(i * hb - 1, 0), 0)),
            pl.BlockSpec((HALO_ROWS, D_MODEL),
                         lambda i: (jnp.minimum((i + 1) * hb, n_halo_blocks - 1), 0)),
            pl.BlockSpec((1, D_MODEL), const),
            pl.BlockSpec((D_MODEL, 2 * D_FF), const, pipeline_mode=pl.Buffered(1)),
            pl.BlockSpec((3, 2 * D_FF), const),
            pl.BlockSpec((1, 2 * D_FF), const),
            pl.BlockSpec((D_FF, D_MODEL), const, pipeline_mode=pl.Buffered(1)),
            pl.BlockSpec((1, D_MODEL), const),
        ],
        out_specs=pl.BlockSpec((rows, D_MODEL), lambda i: (i, 0)),
        out_shape=jax.ShapeDtypeStruct((n, D_MODEL), jnp.float32),
        scratch_shapes=[
            pltpu.VMEM((D_MODEL // LANES, rows, LANES), jnp.float32),
            pltpu.VMEM((rows + 2 * HALO_ROWS, D_MODEL), jnp.float32),
            *[pltpu.VMEM((rows + 2 * HALO_ROWS, 2 * FFN_CHUNK), jnp.float32)] * FFN_U_RING,
            *[pltpu.VMEM((rows, FFN_DOWN_GROUP * FFN_CHUNK), jnp.bfloat16)]
            * pl.cdiv(D_FF // FFN_CHUNK, FFN_DOWN_GROUP),
        ],
        compiler_params=pltpu.CompilerParams(
            dimension_semantics=("arbitrary",), vmem_limit_bytes=VMEM_LIMIT),
        name="ffn",
    )(x1, x1, x1, g_pre, w_up_bf16, conv_w, conv_b, w_down_bf16, g_post)


def _block_diag(pool_w):
    groups = pool_w.shape[0]
    bd = jnp.zeros((POOL_WIDTH, POOL_WIDTH), pool_w.dtype)
    for g in range(groups):
        lo = g * POOL_GROUP_DIM
        bd = bd.at[lo:lo + POOL_GROUP_DIM, lo:lo + POOL_GROUP_DIM].set(pool_w[g])
    return bd


def _layer(x, params):
    (g_mix_pre, g_mix_post, w_in, pool_bd, pool_scale, g_pool_out, g_attn_out, w_out,
     g_ffn_pre, g_ffn_post, w_up, conv_w, conv_b, w_down) = params
    batch, seq, _ = x.shape
    assert seq % ATTN_ROWS == 0 and seq % PROJ_ROWS == 0 and seq % MIX_ROWS == 0 and seq % FFN_ROWS == 0
    xf = x.reshape(batch * seq, D_MODEL)
    p_in, q, k, v = _proj_call(xf, g_mix_pre, w_in, _rope_tables(seq), seq)
    attn = _attn_call(q, k, v, batch, seq).reshape(N_LANE_GROUPS, batch * seq, LANES)
    x1 = _mix_call(xf, p_in, attn, pool_bd, pool_scale, g_pool_out, g_attn_out, w_out, g_mix_post, seq)
    y = _ffn_call(x1, g_ffn_pre, w_up, conv_w, conv_b, w_down, g_ffn_post, seq)
    return y.reshape(batch, seq, D_MODEL)


def kernel(x_prompt, x_sample, g_mix_pre, g_mix_post, w_in, pool_w, pool_scale, g_pool_out, g_attn_out,
           w_out, g_ffn_pre, g_ffn_post, w_up, conv_w, conv_b, w_down):
    depth = w_in.shape[0]
    bf16 = jnp.bfloat16

    def layer_params(l):
        return (g_mix_pre[l][None], g_mix_post[l][None], w_in[l].astype(bf16),
                _block_diag(pool_w[l]).astype(bf16), pool_scale[l][None], g_pool_out[l][None],
                g_attn_out[l][None], w_out[l].astype(bf16), g_ffn_pre[l][None], g_ffn_post[l][None],
                w_up[l].astype(bf16), conv_w[l], conv_b[l][None], w_down[l].astype(bf16))

    params = [layer_params(l) for l in range(depth)]

    def run(x):
        for p in params:
            x = _layer(x, p)
        return x

    return (run(x_prompt), run(x_sample))
```

```python
import functools
import math

import jax
import jax.numpy as jnp
from jax import lax
from jax.experimental import pallas as pl
from jax.experimental.pallas import tpu as pltpu

D_MODEL = 1024
POOL_WIDTH = 256
POOL_WINDOWS = (2, 4, 8, 16)
POOL_GROUP_DIM = 64
ATTN_WIDTH = 768
HEAD_DIM = 64
DILATIONS = (1, 4, 16)
QKV_LAYOUTS = (4, 16)
BAND_RADIUS = 64
ROPE_THETA = 500000.0
ROPE_DIM = 16
D_FF = 2816
EPS = 1e-6
IN_WIDTH = POOL_WIDTH + 3 * ATTN_WIDTH

LANES = 128
N_LANE_GROUPS = ATTN_WIDTH // LANES
NEG_BIG = -1e30
Q_SCALE = HEAD_DIM ** -0.5 * math.log2(math.e)

PROJ_ROWS = 512
ATTN_ROWS = 2048
Q_CHUNK = 128
K_CHUNK = Q_CHUNK + 2 * BAND_RADIUS
MIX_ROWS = 1024
MIX_SPLIT = 2
FFN_ROWS = 512
FFN_CHUNK = 256
FFN_LOOKAHEAD = 3
FFN_U_RING = FFN_LOOKAHEAD + 1
FFN_DOWN_GROUP = 4
FFN_TAIL_BLOCKS = 4
HALO_ROWS = 8

V7X_VMEM_BYTES = 64 * 1024 * 1024
VMEM_LIMIT = V7X_VMEM_BYTES - 8 * 1024 * 1024


def _rms(x, g):
    ms = jnp.mean(x * x, axis=-1, keepdims=True)
    return x * lax.rsqrt(ms + EPS) * g


def _proj_kernel(x_ref, g_ref, w_ref, tab_ref, p_ref, *refs):
    n_lay = len(QKV_LAYOUTS)
    outs = [refs[n_lay * i:n_lay * (i + 1)] for i in range(3)]
    stage = refs[3 * n_lay:]
    rows = PROJ_ROWS
    bf16 = jnp.bfloat16
    h = _rms(x_ref[...], g_ref[...]).astype(bf16)
    proj = jnp.dot(h, w_ref[...], preferred_element_type=jnp.float32)
    p_ref[...] = proj[:, :POOL_WIDTH]
    cos, sin_lo, sin_hi = tab_ref[0], tab_ref[1], tab_ref[2]

    def rope(t):
        return t * cos + pltpu.roll(t, LANES - 8, 1) * sin_lo + pltpu.roll(t, 8, 1) * sin_hi

    def emit(val, out_refs, g, slab_a, slab_b):
        o4, o16 = out_refs
        slab_a[...] = val
        for r_lo in range(4):
            plane = slab_a[pl.ds(r_lo, rows // 4, stride=4), :]
            o4[g, :, pl.ds(r_lo * LANES, LANES)] = plane.astype(bf16)
            slab_b[pl.ds(r_lo * (rows // 4), rows // 4), :] = plane
        for r_lo in range(4):
            for r_hi in range(4):
                piece = slab_b[pl.ds(r_lo * (rows // 4) + r_hi, rows // 16, stride=4), :]
                o16[g, :, pl.ds((4 * r_hi + r_lo) * LANES, LANES)] = piece.astype(bf16)

    for g in range(N_LANE_GROUPS):
        lo = POOL_WIDTH + g * LANES
        emit(rope(proj[:, lo:lo + LANES]) * Q_SCALE, outs[0], g, stage[6 * g], stage[6 * g + 1])
        lo += ATTN_WIDTH
        emit(rope(proj[:, lo:lo + LANES]), outs[1], g, stage[6 * g + 2], stage[6 * g + 3])
        lo += ATTN_WIDTH
        emit(proj[:, lo:lo + LANES], outs[2], g, stage[6 * g + 4], stage[6 * g + 5])


def _rope_tables(seq):
    pos = jnp.arange(seq, dtype=jnp.float32)
    inv_freq = ROPE_THETA ** (-jnp.arange(0, ROPE_DIM, 2, dtype=jnp.float32) / ROPE_DIM)
    half = ROPE_DIM // 2
    dim = jnp.arange(LANES) % HEAD_DIM
    ang = pos[:, None] * inv_freq[dim % half][None, :]
    cos, sin = jnp.cos(ang), jnp.sin(ang)
    first_half, rotated = (dim < half)[None, :], (dim < ROPE_DIM)[None, :]
    c = jnp.where(rotated, cos, 1.0)
    s1 = jnp.where(first_half, -sin, 0.0)
    s2 = jnp.where(rotated & ~first_half, sin, 0.0)
    return jnp.stack([c, s1, s2])


def _proj_call(x, g, w_bf16, tabs, seq):
    n = x.shape[0]
    t = PROJ_ROWS
    blocks_per_seq = seq // t
    qkv_shapes = [jax.ShapeDtypeStruct((N_LANE_GROUPS, n // d, d * LANES), jnp.bfloat16)
                  for d in QKV_LAYOUTS] * 3
    qkv_specs = [pl.BlockSpec((N_LANE_GROUPS, t // d, d * LANES), lambda i: (0, i, 0))
                 for d in QKV_LAYOUTS] * 3
    outs = pl.pallas_call(
        _proj_kernel,
        grid=(n // t,),
        in_specs=[
            pl.BlockSpec((t, D_MODEL), lambda i: (i, 0)),
            pl.BlockSpec((1, D_MODEL), lambda i: (0, 0)),
            pl.BlockSpec((D_MODEL, IN_WIDTH), lambda i: (0, 0)),
            pl.BlockSpec((3, t, LANES), lambda i: (0, i % blocks_per_seq, 0)),
        ],
        out_specs=[pl.BlockSpec((t, POOL_WIDTH), lambda i: (i, 0)), *qkv_specs],
        out_shape=[jax.ShapeDtypeStruct((n, POOL_WIDTH), jnp.float32), *qkv_shapes],
        scratch_shapes=[pltpu.VMEM((t, LANES), jnp.float32)] * (2 * 3 * N_LANE_GROUPS),
        compiler_params=pltpu.CompilerParams(
            dimension_semantics=("arbitrary",), vmem_limit_bytes=VMEM_LIMIT),
        name="proj",
    )(x, g, w_bf16, tabs)
    n_lay = len(QKV_LAYOUTS)
    return outs[0], outs[1:1 + n_lay], outs[1 + n_lay:1 + 2 * n_lay], outs[1 + 2 * n_lay:]


def _attn_kernel(*refs, n_blocks):
    n_lay = len(QKV_LAYOUTS)
    q_refs = dict(zip(QKV_LAYOUTS, refs[:n_lay]))
    k_refs = {d: refs[n_lay + 3 * i:n_lay + 3 * i + 3] for i, d in enumerate(QKV_LAYOUTS)}
    v_refs = {d: refs[4 * n_lay + 3 * i:4 * n_lay + 3 * i + 3] for i, d in enumerate(QKV_LAYOUTS)}
    o_ref, tbuf, mpat, mall, opat, lpat, bias_ref = refs[7 * n_lay:]
    t = pl.program_id(2)
    rows = ATTN_ROWS
    plane = rows // 4
    f32 = jnp.float32
    bf16 = jnp.bfloat16

    @pl.when((pl.program_id(0) == 0) & (pl.program_id(1) == 0) & (t == 0))
    def _():
        row = lax.broadcasted_iota(jnp.int32, (2 * Q_CHUNK, K_CHUNK), 0) % Q_CHUNK
        col = lax.broadcasted_iota(jnp.int32, (2 * Q_CHUNK, K_CHUNK), 1)
        q_rows, k_rows = Q_CHUNK // 4, K_CHUNK // 4
        delta = 4 * (col % k_rows - row % q_rows) + (col // k_rows - row // q_rows)
        families = (
            ((col >= row) & (col <= row + 2 * BAND_RADIUS), col, BAND_RADIUS, BAND_RADIUS + Q_CHUNK),
            ((delta >= 0) & (delta <= 2 * BAND_RADIUS), col % k_rows, BAND_RADIUS // 4,
             (BAND_RADIUS + Q_CHUNK) // 4),
        )
        for fam, (band, key_pos, lo, hi) in enumerate(families):
            for variant in range(4):
                ok = band
                if variant & 1:
                    ok = ok & (key_pos >= lo)
                if variant & 2:
                    ok = ok & (key_pos < hi)
                bias_ref[4 * fam + variant] = jnp.where(ok, 0.0, NEG_BIG).astype(f32)

    lane = lax.broadcasted_iota(jnp.int32, (Q_CHUNK, LANES), 1)
    first_head = lane < HEAD_DIM

    def pick_head(x):
        return jnp.where(first_head, x[:Q_CHUNK], x[Q_CHUNK:])

    def ext_rows(trio, lo, hi, block):
        prev_ref, main_ref, next_ref = trio
        lanes = pl.ds(block * LANES, LANES)
        n_main, n_halo = main_ref.shape[0], prev_ref.shape[0]
        parts = []
        if lo < 0:
            parts.append(prev_ref[pl.ds(n_halo + lo, -lo), lanes])
        parts.append(main_ref[pl.ds(max(lo, 0), min(hi, n_main) - max(lo, 0)), lanes])
        if hi > n_main:
            parts.append(next_ref[pl.ds(0, hi - n_main), lanes])
        return parts

    def rows_of(parts):
        return parts[0] if len(parts) == 1 else jnp.concatenate(parts, axis=0)

    tiles = []
    q_rows, k_rows = Q_CHUNK // 4, K_CHUNK // 4
    for c in range(rows // Q_CHUNK):
        lo = c * q_rows - BAND_RADIUS // 4
        q_parts = [q_refs[4][pl.ds(c * q_rows, q_rows), pl.ds(j * LANES, LANES)] for j in range(4)]
        keys = lambda trios, lo=lo: rows_of(
            [p for j in range(4) for p in ext_rows(trios[4], lo, lo + k_rows, j)])
        acc = [(j * q_rows, pl.ds(j * plane + c * q_rows, q_rows)) for j in range(4)]
        tiles.append((0, q_parts, keys, 1, c == 0, c == rows // Q_CHUNK - 1, acc))
    for r in range(4):
        for c in range(plane // Q_CHUNK):
            lo = c * Q_CHUNK - BAND_RADIUS
            q_parts = [q_refs[4][pl.ds(c * Q_CHUNK, Q_CHUNK), pl.ds(r * LANES, LANES)]]
            keys = lambda trios, lo=lo, r=r: rows_of(ext_rows(trios[4], lo, lo + K_CHUNK, r))
            acc = [(0, pl.ds(r * plane + c * Q_CHUNK, Q_CHUNK))]
            tiles.append((1, q_parts, keys, 0, c == 0, c == plane // Q_CHUNK - 1, acc))
    for r in (4 * r_hi + r_lo for r_lo in range(4) for r_hi in range(4)):
        q_parts = [q_refs[16][:, pl.ds(r * LANES, LANES)]]
        keys = lambda trios, r=r: rows_of(ext_rows(trios[16], -BAND_RADIUS, Q_CHUNK + BAND_RADIUS, r))
        acc = [(0, pl.ds((r % 4) * plane + r // 4, Q_CHUNK, stride=4))]
        tiles.append((2, q_parts, keys, 0, True, True, acc))

    def bias_index(family, at_start, at_end):
        if n_blocks == 1:
            return 4 * family + int(at_start) + 2 * int(at_end)
        first = (t == 0).astype(jnp.int32) if at_start else 0
        last = (t == n_blocks - 1).astype(jnp.int32) if at_end else 0
        return 4 * family + first + 2 * last

    def store_rows(ref, lead, acc, x):
        for start, idx in acc:
            ref[(*lead, idx, slice(None))] = x[start:start + idx.size]

    def load_rows(ref, lead, acc):
        return rows_of([ref[(*lead, idx, slice(None))] for _, idx in acc])

    dyn_zero = jnp.minimum(t, 0)

    for n, (pi, q_parts, keys, family, at_start, at_end, acc) in sorted(
            enumerate(tiles), key=lambda item: -item[1][0]):
        qc = rows_of(q_parts)
        zero = jnp.zeros_like(qc)
        q2 = jnp.concatenate([jnp.where(first_head, qc, zero), jnp.where(first_head, zero, qc)], axis=0)
        s = lax.dot_general(q2, keys(k_refs), (((1,), (1,)), ((), ())), preferred_element_type=f32)
        s = s + bias_ref[bias_index(family, at_start, at_end)]
        tbuf[n + dyn_zero] = s
        m = jnp.broadcast_to(jnp.max(s, axis=-1, keepdims=True), (2 * Q_CHUNK, LANES))
        for h in range(2):
            store_rows(mpat, (pi, h), acc, m[h * Q_CHUNK:(h + 1) * Q_CHUNK])

    for i in range(plane // q_rows):
        for r in range(4):
            sl = pl.ds(r * plane + i * q_rows, q_rows)
            for h in range(2):
                mall[h, sl, :] = jnp.maximum(jnp.maximum(mpat[0, h, sl, :], mpat[1, h, sl, :]),
                                             mpat[2, h, sl, :])

    ones = jnp.ones((K_CHUNK, LANES), bf16)
    for n, (pi, q_parts, keys, family, at_start, at_end, acc) in enumerate(tiles):
        mb = jnp.concatenate([load_rows(mall, (0,), acc), load_rows(mall, (1,), acc)], axis=0)
        p = jnp.exp2(tbuf[n + dyn_zero] - jnp.concatenate([mb, mb], axis=1)).astype(bf16)
        vc = jnp.concatenate([keys(v_refs), ones], axis=1)
        ol = jnp.dot(p, vc, preferred_element_type=f32)
        store_rows(opat, (pi,), acc, pick_head(ol[:, :LANES]))
        store_rows(lpat, (pi,), acc, pick_head(ol[:, LANES:]))

    pass_rows = 64
    for r in range(4):
        for i in range(plane // pass_rows):
            sl = pl.ds(r * plane + i * pass_rows, pass_rows)
            num = opat[0, sl, :] + opat[1, sl, :] + opat[2, sl, :]
            den = lpat[0, sl, :] + lpat[1, sl, :] + lpat[2, sl, :]
            o_ref[pl.ds(r + 4 * i * pass_rows, pass_rows, stride=4), :] = num / den


def _attn_call(q, k, v, batch, seq):
    rows = ATTN_ROWS
    n_blocks = seq // rows
    radius = BAND_RADIUS

    def view(a, d):
        return a.reshape(N_LANE_GROUPS, batch, seq // d, d * LANES)

    operands, in_specs = [], []
    for d, a in zip(QKV_LAYOUTS, q):
        operands.append(view(a, d))
        in_specs.append(pl.BlockSpec((None, None, rows // d, d * LANES), lambda b, g, t: (g, b, t, 0)))
    for arrs in (k, v):
        for d, a in zip(QKV_LAYOUTS, arrs):
            halo_per_block = rows // d // radius
            n_halo = seq // d // radius
            operands += [view(a, d)] * 3
            in_specs += [
                pl.BlockSpec((None, None, radius, d * LANES),
                             lambda b, g, t, hb=halo_per_block: (g, b, jnp.maximum(t * hb - 1, 0), 0)),
                pl.BlockSpec((None, None, rows // d, d * LANES), lambda b, g, t: (g, b, t, 0)),
                pl.BlockSpec((None, None, radius, d * LANES),
                             lambda b, g, t, hb=halo_per_block, nh=n_halo:
                             (g, b, jnp.minimum((t + 1) * hb, nh - 1), 0)),
            ]
    n_tiles = len(DILATIONS) * rows // Q_CHUNK
    return pl.pallas_call(
        functools.partial(_attn_kernel, n_blocks=n_blocks),
        grid=(batch, N_LANE_GROUPS, n_blocks),
        in_specs=in_specs,
        out_specs=pl.BlockSpec((None, None, rows, LANES), lambda b, g, t: (g, b, t, 0)),
        out_shape=jax.ShapeDtypeStruct((N_LANE_GROUPS, batch, seq, LANES), jnp.float32),
        scratch_shapes=[
            pltpu.VMEM((n_tiles, 2 * Q_CHUNK, K_CHUNK), jnp.float32),
            pltpu.VMEM((len(DILATIONS), 2, rows, LANES), jnp.float32),
            pltpu.VMEM((2, rows, LANES), jnp.float32),
            pltpu.VMEM((len(DILATIONS), rows, LANES), jnp.float32),
            pltpu.VMEM((len(DILATIONS), rows, LANES), jnp.float32),
            pltpu.VMEM((8, 2 * Q_CHUNK, K_CHUNK), jnp.float32),
        ],
        compiler_params=pltpu.CompilerParams(
            dimension_semantics=("arbitrary", "arbitrary", "arbitrary"), vmem_limit_bytes=VMEM_LIMIT),
        name="attn",
    )(*operands)


def _mix_kernel(x_ref, p_ref, pprev_ref, pnext_ref, a_ref, pw_ref, ps_ref, gp_ref, ga_ref,
                wo_ref, gpost_ref, o_ref, pbuf, mixed, *, seq):
    rows = MIX_ROWS
    f32 = jnp.float32
    i = pl.program_id(0)
    pos0 = (i * rows) % seq
    zero_rows = jnp.zeros((HALO_ROWS, POOL_WIDTH), f32)
    pbuf[pl.ds(0, HALO_ROWS), :] = jnp.where(pos0 > 0, pprev_ref[...], zero_rows)
    pbuf[pl.ds(HALO_ROWS, rows), :] = p_ref[...]
    pbuf[pl.ds(HALO_ROWS + rows, HALO_ROWS), :] = jnp.where(pos0 + rows < seq, pnext_ref[...], zero_rows)

    part = rows // MIX_SPLIT
    for base in range(0, rows, part):
        _mix_part(base, part, pos0 + base, seq, x_ref, a_ref, pw_ref, ps_ref, gp_ref, ga_ref, wo_ref,
                  gpost_ref, o_ref, pbuf, mixed)


def _mix_part(base, rows, pos0, seq, x_ref, a_ref, pw_ref, ps_ref, gp_ref, ga_ref, wo_ref, gpost_ref,
              o_ref, pbuf, mixed):
    f32 = jnp.float32
    rs = pl.ds(base, rows)
    pos = pos0 + lax.broadcasted_iota(jnp.int32, (rows, 1), 0)
    lane = lax.broadcasted_iota(jnp.int32, (rows, LANES), 1)
    means = []
    for half in range(2):
        w_small, w_big = POOL_WINDOWS[2 * half], POOL_WINDOWS[2 * half + 1]
        lanes = pl.ds(half * LANES, LANES)

        def wsum(lo, hi):
            acc = pbuf[pl.ds(base + HALO_ROWS + lo, rows), lanes]
            for off in range(lo + 1, hi):
                acc = acc + pbuf[pl.ds(base + HALO_ROWS + off, rows), lanes]
            return acc

        if half == 0:
            small = wsum(-(w_small // 2), w_small // 2)
            big = small + wsum(-(w_big // 2), -(w_small // 2)) + wsum(w_small // 2, w_big // 2)
        else:
            n_buf = rows + 2 * HALO_ROWS
            a = pbuf[pl.ds(base, n_buf), lanes]
            for step in (1, 2, 4):
                a = a + pltpu.roll(a, n_buf - step, 0)
            small = pltpu.roll(a, w_small // 2, 0)[HALO_ROWS:HALO_ROWS + rows]
            big = (a + pltpu.roll(a, n_buf - w_small, 0))[:rows]

        def count(w):
            return (jnp.minimum(pos + w // 2, seq) - jnp.maximum(pos - w // 2, 0)).astype(f32)

        first = lane < POOL_GROUP_DIM
        total = jnp.where(first, small, big)
        cnt = jnp.where(first, count(w_small), count(w_big))
        means.append(total / cnt - pbuf[pl.ds(base + HALO_ROWS, rows), lanes])
    pooled = jnp.concatenate(means, axis=-1).astype(jnp.bfloat16)
    pool_out = jnp.dot(pooled, pw_ref[...], preferred_element_type=f32) * ps_ref[...]
    mixed[rs, pl.ds(0, POOL_WIDTH)] = _rms(pool_out, gp_ref[...]).astype(jnp.bfloat16)

    sq = None
    for g in range(N_LANE_GROUPS):
        a = a_ref[g, rs, :]
        sq = a * a if sq is None else sq + a * a
    inv = lax.rsqrt(jnp.sum(sq, axis=-1, keepdims=True) / ATTN_WIDTH + EPS)
    for g in range(N_LANE_GROUPS):
        ga = ga_ref[:, pl.ds(g * LANES, LANES)]
        mixed[rs, pl.ds(POOL_WIDTH + g * LANES, LANES)] = (a_ref[g, rs, :] * inv * ga).astype(jnp.bfloat16)

    y = jnp.dot(mixed[rs, :], wo_ref[...], preferred_element_type=f32)
    o_ref[rs, :] = x_ref[rs, :] + _rms(y, gpost_ref[...])


def _mix_call(x, p_in, attn, pool_bd, pool_scale, g_pool, g_attn, w_out_bf16, g_post, seq):
    n = x.shape[0]
    rows = MIX_ROWS
    hb = rows // HALO_ROWS
    n_halo_blocks = n // HALO_ROWS
    const = lambda i: (0, 0)
    return pl.pallas_call(
        functools.partial(_mix_kernel, seq=seq),
        grid=(n // rows,),
        in_specs=[
            pl.BlockSpec((rows, D_MODEL), lambda i: (i, 0)),
            pl.BlockSpec((rows, POOL_WIDTH), lambda i: (i, 0)),
            pl.BlockSpec((HALO_ROWS, POOL_WIDTH), lambda i: (jnp.maximum(i * hb - 1, 0), 0)),
            pl.BlockSpec((HALO_ROWS, POOL_WIDTH),
                         lambda i: (jnp.minimum((i + 1) * hb, n_halo_blocks - 1), 0)),
            pl.BlockSpec((N_LANE_GROUPS, rows, LANES), lambda i: (0, i, 0)),
            pl.BlockSpec((POOL_WIDTH, POOL_WIDTH), const),
            pl.BlockSpec((1, POOL_WIDTH), const),
            pl.BlockSpec((1, POOL_WIDTH), const),
            pl.BlockSpec((1, ATTN_WIDTH), const),
            pl.BlockSpec((D_MODEL, D_MODEL), const),
            pl.BlockSpec((1, D_MODEL), const),
        ],
        out_specs=pl.BlockSpec((rows, D_MODEL), lambda i: (i, 0)),
        out_shape=jax.ShapeDtypeStruct((n, D_MODEL), jnp.float32),
        scratch_shapes=[
            pltpu.VMEM((rows + 2 * HALO_ROWS, POOL_WIDTH), jnp.float32),
            pltpu.VMEM((rows, D_MODEL), jnp.bfloat16),
        ],
        compiler_params=pltpu.CompilerParams(
            dimension_semantics=("arbitrary",), vmem_limit_bytes=VMEM_LIMIT),
        name="mix",
    )(x, p_in, p_in, p_in, attn, pool_bd, pool_scale, g_pool, g_attn, w_out_bf16, g_post)


def _ffn_kernel(x_ref, xprev_ref, xnext_ref, gpre_ref, wup_ref, cw_ref, cb_ref, wdn_ref, gpost_ref,
                o_ref, xs, hbuf, *bufs, seq):
    n_chunks = D_FF // FFN_CHUNK
    ubufs, abufs = bufs[:FFN_U_RING], bufs[FFN_U_RING:]
    rows = FFN_ROWS
    run = rows // HALO_ROWS
    n_slabs = D_MODEL // LANES
    f32 = jnp.float32
    i = pl.program_id(0)
    pos0 = (i * rows) % seq
    g = gpre_ref[...]

    for l in range(n_slabs):
        for s in range(HALO_ROWS):
            xs[l, pl.ds(s, run, stride=HALO_ROWS), :] = x_ref[pl.ds(s * run, run), pl.ds(l * LANES, LANES)]
    sq = None
    for l in range(n_slabs):
        xl = xs[l]
        sq = xl * xl if sq is None else sq + xl * xl
    inv = lax.rsqrt(jnp.sum(sq, axis=-1, keepdims=True) / D_MODEL + EPS)
    for l in range(n_slabs):
        lanes = pl.ds(l * LANES, LANES)
        hbuf[pl.ds(HALO_ROWS, rows), lanes] = xs[l] * inv * gpre_ref[:, lanes]
    hprev = jnp.where(pos0 > 0, _rms(xprev_ref[...], g), 0.0)
    hnext = jnp.where(pos0 + rows < seq, _rms(xnext_ref[...], g), 0.0)
    hbuf[pl.ds(0, HALO_ROWS), :] = jnp.broadcast_to(hprev[HALO_ROWS - 1:], (HALO_ROWS, D_MODEL))
    hbuf[pl.ds(HALO_ROWS + rows, HALO_ROWS), :] = jnp.broadcast_to(hnext[:1], (HALO_ROWS, D_MODEL))
    h = hbuf[...].astype(jnp.bfloat16)

    sub = lax.broadcasted_iota(jnp.int32, (HALO_ROWS, FFN_CHUNK), 0)

    def conv(c, half):
        col = half * D_FF + c * FFN_CHUNK
        w = cw_ref[:, pl.ds(col, FFN_CHUNK)]
        acc = cb_ref[:, pl.ds(col, FFN_CHUNK)]
        for j in range(3):
            u = ubufs[c % FFN_U_RING][pl.ds(HALO_ROWS * j, rows), pl.ds(half * FFN_CHUNK, FFN_CHUNK)]
            acc = acc + u * w[j:j + 1]
        return acc

    def up(c):
        buf = ubufs[c % FFN_U_RING]
        for half in range(2):
            col = half * D_FF + c * FFN_CHUNK
            cols = pl.ds(half * FFN_CHUNK, FFN_CHUNK)
            buf[:, cols] = jnp.dot(h, wup_ref[:, pl.ds(col, FFN_CHUNK)], preferred_element_type=f32)
            halo = jnp.broadcast_to(buf[pl.ds(0, 1), cols], (HALO_ROWS, FFN_CHUNK))
            last = pltpu.roll(buf[pl.ds(rows, HALO_ROWS), cols], 1, 0)
            buf[pl.ds(0, HALO_ROWS), cols] = jnp.where(sub == 0, halo, last)
            halo = jnp.broadcast_to(buf[pl.ds(rows + HALO_ROWS, 1), cols], (HALO_ROWS, FFN_CHUNK))
            first = pltpu.roll(buf[pl.ds(HALO_ROWS, HALO_ROWS), cols], HALO_ROWS - 1, 0)
            buf[pl.ds(rows + HALO_ROWS, HALO_ROWS), cols] = jnp.where(sub == HALO_ROWS - 1, halo, first)

    def down(g, n_in_group):
        width = n_in_group * FFN_CHUNK
        w = wdn_ref[pl.ds(g * FFN_DOWN_GROUP * FFN_CHUNK, width), :]
        return jnp.dot(abufs[g][:, pl.ds(0, width)], w, preferred_element_type=f32)

    for c in range(min(FFN_LOOKAHEAD, n_chunks)):
        up(c)
    y = None
    for c in range(n_chunks):
        if c + FFN_LOOKAHEAD < n_chunks:
            up(c + FFN_LOOKAHEAD)
        gate = conv(c, 0)
        val = conv(c, 1)
        gelu = 0.5 * gate * (1.0 + lax.erf(gate * (2.0 ** -0.5)))
        g, j = divmod(c, FFN_DOWN_GROUP)
        abufs[g][:, pl.ds(j * FFN_CHUNK, FFN_CHUNK)] = (gelu * val).astype(jnp.bfloat16)
        if j + 1 == FFN_DOWN_GROUP and c + 1 < n_chunks:
            part = down(g, j + 1)
            y = part if y is None else y + part

    g, n_last = divmod(n_chunks - 1, FFN_DOWN_GROUP)
    width = (n_last + 1) * FFN_CHUNK
    w_last = wdn_ref[pl.ds(g * FFN_DOWN_GROUP * FFN_CHUNK, width), :]
    blk = rows // FFN_TAIL_BLOCKS
    blk_run = run // FFN_TAIL_BLOCKS
    for b in range(FFN_TAIL_BLOCKS):
        rs = pl.ds(b * blk, blk)
        yb = jnp.dot(abufs[g][rs, pl.ds(0, width)], w_last, preferred_element_type=f32)
        if y is not None:
            yb = yb + y[b * blk:(b + 1) * blk]
        res = _rms(yb, gpost_ref[...])
        for l in range(n_slabs):
            xs[l, rs, :] = xs[l, rs, :] + res[:, l * LANES:(l + 1) * LANES]
        for l in range(n_slabs):
            for s in range(HALO_ROWS):
                o_ref[pl.ds(s * run + b * blk_run, blk_run), pl.ds(l * LANES, LANES)] = (
                    xs[l, pl.ds(b * blk + s, blk_run, stride=HALO_ROWS), :])


def _ffn_call(x1, g_pre, w_up_bf16, conv_w, conv_b, w_down_bf16, g_post, seq):
    n = x1.shape[0]
    rows = FFN_ROWS
    hb = rows // HALO_ROWS
    n_halo_blocks = n // HALO_ROWS
    const = lambda i: (0, 0)
    return pl.pallas_call(
        functools.partial(_ffn_kernel, seq=seq),
        grid=(n // rows,),
        in_specs=[
            pl.BlockSpec((rows, D_MODEL), lambda i: (i, 0)),
            pl.BlockSpec((HALO_ROWS, D_MODEL), lambda i: (jnp.maximum(i * hb - 1, 0), 0)),
            pl.BlockSpec((HALO_ROWS, D_MODEL),
                         lambda i: (jnp.minimum((i + 1) * hb, n_halo_blocks - 1), 0)),
            pl.BlockSpec((1, D_MODEL), const),
            pl.BlockSpec((D_MODEL, 2 * D_FF), const, pipeline_mode=pl.Buffered(1)),
            pl.BlockSpec((3, 2 * D_FF), const),
            pl.BlockSpec((1, 2 * D_FF), const),
            pl.BlockSpec((D_FF, D_MODEL), const, pipeline_mode=pl.Buffered(1)),
            pl.BlockSpec((1, D_MODEL), const),
        ],
        out_specs=pl.BlockSpec((rows, D_MODEL), lambda i: (i, 0)),
        out_shape=jax.ShapeDtypeStruct((n, D_MODEL), jnp.float32),
        scratch_shapes=[
            pltpu.VMEM((D_MODEL // LANES, rows, LANES), jnp.float32),
            pltpu.VMEM((rows + 2 * HALO_ROWS, D_MODEL), jnp.float32),
            *[pltpu.VMEM((rows + 2 * HALO_ROWS, 2 * FFN_CHUNK), jnp.float32)] * FFN_U_RING,
            *[pltpu.VMEM((rows, FFN_DOWN_GROUP * FFN_CHUNK), jnp.bfloat16)]
            * pl.cdiv(D_FF // FFN_CHUNK, FFN_DOWN_GROUP),
        ],
        compiler_params=pltpu.CompilerParams(
            dimension_semantics=("arbitrary",), vmem_limit_bytes=VMEM_LIMIT),
        name="ffn",
    )(x1, x1, x1, g_pre, w_up_bf16, conv_w, conv_b, w_down_bf16, g_post)


def _block_diag(pool_w):
    groups = pool_w.shape[0]
    bd = jnp.zeros((POOL_WIDTH, POOL_WIDTH), pool_w.dtype)
    for g in range(groups):
        lo = g * POOL_GROUP_DIM
        bd = bd.at[lo:lo + POOL_GROUP_DIM, lo:lo + POOL_GROUP_DIM].set(pool_w[g])
    return bd


def _layer(x, params):
    (g_mix_pre, g_mix_post, w_in, pool_bd, pool_scale, g_pool_out, g_attn_out, w_out,
     g_ffn_pre, g_ffn_post, w_up, conv_w, conv_b, w_down) = params
    batch, seq, _ = x.shape
    assert seq % ATTN_ROWS == 0 and seq % PROJ_ROWS == 0 and seq % MIX_ROWS == 0 and seq % FFN_ROWS == 0
    xf = x.reshape(batch * seq, D_MODEL)
    p_in, q, k, v = _proj_call(xf, g_mix_pre, w_in, _rope_tables(seq), seq)
    attn = _attn_call(q, k, v, batch, seq).reshape(N_LANE_GROUPS, batch * seq, LANES)
    x1 = _mix_call(xf, p_in, attn, pool_bd, pool_scale, g_pool_out, g_attn_out, w_out, g_mix_post, seq)
    y = _ffn_call(x1, g_ffn_pre, w_up, conv_w, conv_b, w_down, g_ffn_post, seq)
    return y.reshape(batch, seq, D_MODEL)


def kernel(x_prompt, x_sample, g_mix_pre, g_mix_post, w_in, pool_w, pool_scale, g_pool_out, g_attn_out,
           w_out, g_ffn_pre, g_ffn_post, w_up, conv_w, conv_b, w_down):
    depth = w_in.shape[0]
    bf16 = jnp.bfloat16

    def layer_params(l):
        return (g_mix_pre[l][None], g_mix_post[l][None], w_in[l].astype(bf16),
                _block_diag(pool_w[l]).astype(bf16), pool_scale[l][None], g_pool_out[l][None],
                g_attn_out[l][None], w_out[l].astype(bf16), g_ffn_pre[l][None], g_ffn_post[l][None],
                w_up[l].astype(bf16), conv_w[l], conv_b[l][None], w_down[l].astype(bf16))

    params = [layer_params(l) for l in range(depth)]

    def run(x):
        for p in params:
            x = _layer(x, p)
        return x

    return (run(x_prompt), run(x_sample))
```
